```python
import math
import jax
import jax.numpy as jnp
from jax import lax
import numpy as np

D_MODEL = 1024
BATCH = 2
SEQ = 16384
DEPTH = 2

GRID_W = 64
CTX_LEN = 256
N_MIXERS = 2
N_NA_LAYERS = (DEPTH + N_MIXERS - 1) // N_MIXERS
N_DIFF_LAYERS = DEPTH // N_MIXERS
NA_HEADS = 16
NA_HEAD_DIM = D_MODEL // NA_HEADS
NA_WIN_ROWS = 8
NA_WIN_COLS = 16
DIFF_HEADS = 8
DIFF_HEAD_DIM = D_MODEL // DIFF_HEADS // 2
DIFF_EPS = 1e-5
Q_BLOCK = 128
ROPE_BASE = 10000.0
N_EXPERTS = 32
TOP_K = 4
D_FF = D_MODEL
SWIGLU_LIMIT = 7.0
SWIGLU_ALPHA = 1.702
EXPERT_BLOCK = 128
NORM_EPS = 1e-6

kernel_name = 'hybrid_natten_diffattn_moe_dit'


def rmsnorm(x, w, eps=NORM_EPS):
    xf = x.astype(jnp.float32)
    y = xf * lax.rsqrt(jnp.mean(xf * xf, axis=-1, keepdims=True) + eps)
    return (y * w.astype(jnp.float32)).astype(x.dtype)


def modulate(h, shift, scale):
    return h * (1 + scale) + shift


def softmax_f32(s):
    return jax.nn.softmax(s.astype(jnp.float32), axis=-1)


def axial_rope_tables(n_tokens, head_dim):
    t = jnp.arange(n_tokens, dtype=jnp.int32)
    row = (t // GRID_W).astype(jnp.float32)
    col = (t % GRID_W).astype(jnp.float32)
    axis_dim = head_dim // 2
    inv_freq = ROPE_BASE ** (-jnp.arange(0, axis_dim, 2, dtype=jnp.float32) / axis_dim)
    ang = jnp.concatenate([row[:, None] * inv_freq, col[:, None] * inv_freq], axis=-1)
    return jnp.cos(ang), jnp.sin(ang)


def apply_rope(x, cos, sin):
    xf = x.astype(jnp.float32).reshape(x.shape[:-1] + (x.shape[-1] // 2, 2))
    x1, x2 = xf[..., 0], xf[..., 1]
    cs, sn = cos[None, :, None, :], sin[None, :, None, :]
    out = jnp.stack([x1 * cs - x2 * sn, x1 * sn + x2 * cs], axis=-1)
    return out.reshape(x.shape).astype(x.dtype)


def neighbourhood_attention(h, hc, w_qkv, w_o, rpb, need_ctx_out):
    B, S, D = h.shape
    L = hc.shape[1]
    H, dh = NA_HEADS, NA_HEAD_DIM
    rows = S // GRID_W
    wr = min(NA_WIN_ROWS, rows)
    scale = dh ** -0.5
    q, k, v = jnp.split(h @ w_qkv, 3, axis=-1)
    qg = (q * scale).reshape(B, rows, GRID_W, H, dh)
    kg = k.reshape(B, rows, GRID_W, H, dh)
    vg = v.reshape(B, rows, GRID_W, H, dh)
    kc = (hc @ w_qkv[:, D:2 * D]).reshape(B, L, H, dh)
    vc = (hc @ w_qkv[:, 2 * D:]).reshape(B, L, H, dh)
    col = jnp.arange(GRID_W, dtype=jnp.int32)
    c0 = jnp.clip(col - NA_WIN_COLS // 2, 0, GRID_W - NA_WIN_COLS)
    col_idx = c0[:, None] + jnp.arange(NA_WIN_COLS, dtype=jnp.int32)[None, :]
    dc_idx = col_idx - col[:, None] + (NA_WIN_COLS - 1)
    n_nb = wr * NA_WIN_COLS

    def row_block(r):
        r0 = jnp.clip(r - wr // 2, 0, rows - wr)
        qr = lax.dynamic_index_in_dim(qg, r, axis=1, keepdims=False)
        kr = lax.dynamic_slice_in_dim(kg, r0, wr, axis=1)
        vr = lax.dynamic_slice_in_dim(vg, r0, wr, axis=1)
        kn = jnp.take(kr, col_idx, axis=2)
        vn = jnp.take(vr, col_idx, axis=2)
        dr_idx = r0 + jnp.arange(wr, dtype=jnp.int32) - r + (NA_WIN_ROWS - 1)
        bias = rpb[:, dr_idx[None, :, None], dc_idx[:, None, :]]
        s_nb = jnp.einsum('bqhd,brqjhd->bhqrj', qr, kn) + bias[None]
        s_ctx = jnp.einsum('bqhd,bkhd->bhqk', qr, kc)
        p = softmax_f32(jnp.concatenate([s_nb.reshape(B, H, GRID_W, n_nb), s_ctx], axis=-1)).astype(h.dtype)
        p_nb = p[..., :n_nb].reshape(B, H, GRID_W, wr, NA_WIN_COLS)
        o = jnp.einsum('bhqrj,brqjhd->bqhd', p_nb, vn) + jnp.einsum('bhqk,bkhd->bqhd', p[..., n_nb:], vc)
        return o.reshape(B, GRID_W, D)

    o = lax.map(row_block, jnp.arange(rows, dtype=jnp.int32))
    out = o.transpose(1, 0, 2, 3).reshape(B, S, D) @ w_o
    oc = None
    if need_ctx_out:
        qc = (hc @ w_qkv[:, :D]).reshape(B, L, H, dh) * scale
        pc = softmax_f32(jnp.einsum('bqhd,bkhd->bhqk', qc, kc)).astype(h.dtype)
        oc = jnp.einsum('bhqk,bkhd->bqhd', pc, vc).reshape(B, L, D) @ w_o
    return out, oc


def diff_attention(h, hc, w_qkv, w_o, lq1, lk1, lq2, lk2, subln, lambda_init, need_ctx_out):
    B, S, D = h.shape
    L = hc.shape[1]
    H, dh = DIFF_HEADS, DIFF_HEAD_DIM
    scale = dh ** -0.5
    q, k, v = jnp.split(h @ w_qkv, 3, axis=-1)
    cos, sin = axial_rope_tables(S, dh)
    q = apply_rope(q.reshape(B, S, 2 * H, dh), cos, sin) * scale
    k = apply_rope(k.reshape(B, S, 2 * H, dh), cos, sin)
    v = v.reshape(B, S, H, 2 * dh)
    kc = (hc @ w_qkv[:, D:2 * D]).reshape(B, L, 2 * H, dh)
    vc = (hc @ w_qkv[:, 2 * D:]).reshape(B, L, H, 2 * dh)
    f32 = jnp.float32
    lam = (jnp.exp(jnp.sum(lq1.astype(f32) * lk1.astype(f32)))
           - jnp.exp(jnp.sum(lq2.astype(f32) * lk2.astype(f32))) + lambda_init)
    k_all = jnp.concatenate([k, kc], axis=1)
    v_all = jnp.concatenate([v, vc], axis=1)

    def diff_map(qi, keys, vals):
        nq, nk = qi.shape[1], keys.shape[1]
        p = softmax_f32(jnp.einsum('bqhd,bkhd->bhqk', qi, keys)).reshape(B, H, 2, nq, nk)
        a = (p[:, :, 0] - lam * p[:, :, 1]).astype(vals.dtype)
        return jnp.einsum('bhqk,bkhd->bqhd', a, vals)

    def finish(o, n):
        return (rmsnorm(o, subln, DIFF_EPS) * (1.0 - lambda_init)).reshape(B, n, D) @ w_o

    n_blk = S // Q_BLOCK
    qb = q.reshape(B, n_blk, Q_BLOCK, 2 * H, dh).transpose(1, 0, 2, 3, 4)
    o = lax.map(lambda qi: diff_map(qi, k_all, v_all), qb)
    out = finish(o.transpose(1, 0, 2, 3, 4).reshape(B, S, H, 2 * dh), S)
    oc = None
    if need_ctx_out:
        qc = (hc @ w_qkv[:, :D]).reshape(B, L, 2 * H, dh) * scale
        oc = finish(diff_map(qc, kc, vc), L)
    return out, oc


def moe_ffn(h, w_router, b_router, w_gate_up, b_gate_up, w_down, b_down):
    N, D = h.shape
    logits = (h @ w_router).astype(jnp.float32) + b_router.astype(jnp.float32)
    top_val, top_idx = lax.top_k(logits, TOP_K)
    gates = jax.nn.softmax(top_val, axis=-1).astype(h.dtype)
    A = N * TOP_K
    e_flat = top_idx.reshape(A).astype(jnp.int32)
    order = jnp.argsort(e_flat)
    pos_sorted = jnp.zeros((A,), jnp.int32).at[order].set(jnp.arange(A, dtype=jnp.int32))
    counts = jnp.bincount(e_flat, length=N_EXPERTS).astype(jnp.int32)
    starts = jnp.cumsum(counts) - counts
    padded = (counts + EXPERT_BLOCK - 1) // EXPERT_BLOCK * EXPERT_BLOCK
    pad_ends = jnp.cumsum(padded)
    pad_starts = pad_ends - padded
    dest = pad_starts[e_flat] + pos_sorted - starts[e_flat]
    n_blocks = (A + N_EXPERTS * (EXPERT_BLOCK - 1) + EXPERT_BLOCK - 1) // EXPERT_BLOCK
    tok = jnp.arange(A, dtype=jnp.int32) // TOP_K
    buf = jnp.zeros((n_blocks * EXPERT_BLOCK, D), h.dtype).at[dest].set(h[tok])
    block_start = jnp.arange(n_blocks, dtype=jnp.int32) * EXPERT_BLOCK
    block_expert = jnp.minimum(jnp.searchsorted(pad_ends, block_start, side='right'), N_EXPERTS - 1)

    def expert_block(args):
        xb, e = args
        gu = xb @ w_gate_up[e] + b_gate_up[e]
        g, u = gu[:, :D_FF], gu[:, D_FF:]
        g = jnp.minimum(g, SWIGLU_LIMIT)
        u = jnp.clip(u, -SWIGLU_LIMIT, SWIGLU_LIMIT)
        act = (u + 1) * (g * jax.nn.sigmoid(SWIGLU_ALPHA * g))
        return act @ w_down[e] + b_down[e]

    out = lax.map(expert_block, (buf.reshape(n_blocks, EXPERT_BLOCK, D), block_expert)).reshape(-1, D)
    y = out[dest].reshape(N, TOP_K, D)
    return jnp.einsum('nk,nkd->nd', gates, y)


def setup_inputs(seed: int = 0) -> dict:
    key = jax.random.key(seed)
    ks = jax.random.split(key, 26)
    f32 = jnp.float32
    D, E, F = D_MODEL, N_EXPERTS, D_FF

    def nrm(k, shape, s):
        return jax.random.normal(k, shape, f32) * s

    def gain(k, shape):
        return 1.0 + nrm(k, shape, 0.01)

    return {
        'x': nrm(ks[0], (BATCH, SEQ, D), 1.0),
        'c': nrm(ks[1], (BATCH, D), 1.0),
        'ctx': nrm(ks[2], (BATCH, CTX_LEN, D), 1.0),
        'c_ctx': nrm(ks[3], (D,), 1.0),
        'w_mod': nrm(ks[4], (DEPTH, D, 6 * D), 0.5 * D ** -0.5),
        'b_mod': nrm(ks[5], (DEPTH, 6 * D), 0.01),
        'norm_mix_pre': gain(ks[6], (DEPTH, D)),
        'norm_mix_post': gain(ks[7], (DEPTH, D)),
        'norm_ffn_pre': gain(ks[8], (DEPTH, D)),
        'norm_ffn_post': gain(ks[9], (DEPTH, D)),
        'na_w_qkv': nrm(ks[10], (N_NA_LAYERS, D, 3 * D), D ** -0.5),
        'na_w_o': nrm(ks[11], (N_NA_LAYERS, D, D), D ** -0.5),
        'na_rpb': nrm(ks[12], (N_NA_LAYERS, NA_HEADS, 2 * NA_WIN_ROWS - 1, 2 * NA_WIN_COLS - 1), 0.1),
        'diff_w_qkv': nrm(ks[13], (N_DIFF_LAYERS, D, 3 * D), D ** -0.5),
        'diff_w_o': nrm(ks[14], (N_DIFF_LAYERS, D, D), D ** -0.5),
        'diff_lambda_q1': nrm(ks[15], (N_DIFF_LAYERS, DIFF_HEAD_DIM), 0.1),
        'diff_lambda_k1': nrm(ks[16], (N_DIFF_LAYERS, DIFF_HEAD_DIM), 0.1),
        'diff_lambda_q2': nrm(ks[17], (N_DIFF_LAYERS, DIFF_HEAD_DIM), 0.1),
        'diff_lambda_k2': nrm(ks[18], (N_DIFF_LAYERS, DIFF_HEAD_DIM), 0.1),
        'diff_subln': gain(ks[19], (N_DIFF_LAYERS, 2 * DIFF_HEAD_DIM)),
        'moe_w_router': nrm(ks[20], (DEPTH, D, E), D ** -0.5),
        'moe_b_router': nrm(ks[21], (DEPTH, E), 0.01),
        'moe_w_gate_up': nrm(ks[22], (DEPTH, E, D, 2 * F), D ** -0.5),
        'moe_b_gate_up': nrm(ks[23], (DEPTH, E, 2 * F), 0.01),
        'moe_w_down': nrm(ks[24], (DEPTH, E, F, D), F ** -0.5),
        'moe_b_down': nrm(ks[25], (DEPTH, E, D), 0.01),
    }


def reference(x, c, ctx, c_ctx, w_mod, b_mod, norm_mix_pre, norm_mix_post, norm_ffn_pre, norm_ffn_post,
              na_w_qkv, na_w_o, na_rpb,
              diff_w_qkv, diff_w_o, diff_lambda_q1, diff_lambda_k1, diff_lambda_q2, diff_lambda_k2, diff_subln,
              moe_w_router, moe_b_router, moe_w_gate_up, moe_b_gate_up, moe_w_down, moe_b_down):
    B, S, D = x.shape
    L = ctx.shape[1]
    xc = ctx
    silu_c = jax.nn.silu(c)
    silu_cc = jax.nn.silu(c_ctx)
    for i in range(DEPTH):
        last = i == DEPTH - 1
        mod = silu_c @ w_mod[i] + b_mod[i]
        mod_c = silu_cc @ w_mod[i] + b_mod[i]
        sh_a, sc_a, g_a, sh_f, sc_f, g_f = jnp.split(mod[:, None, :], 6, axis=-1)
        shc_a, scc_a, gc_a, shc_f, scc_f, gc_f = jnp.split(mod_c, 6, axis=-1)
        h = modulate(rmsnorm(x, norm_mix_pre[i]), sh_a, sc_a)
        hc = modulate(rmsnorm(xc, norm_mix_pre[i]), shc_a, scc_a)
        j = i // N_MIXERS
        if i % N_MIXERS == 0:
            o, oc = neighbourhood_attention(h, hc, na_w_qkv[j], na_w_o[j], na_rpb[j], not last)
        else:
            lambda_init = 0.8 - 0.6 * math.exp(-0.3 * i)
            o, oc = diff_attention(h, hc, diff_w_qkv[j], diff_w_o[j], diff_lambda_q1[j], diff_lambda_k1[j],
                                   diff_lambda_q2[j], diff_lambda_k2[j], diff_subln[j], lambda_init, not last)
        x = x + g_a * rmsnorm(o, norm_mix_post[i])
        h = modulate(rmsnorm(x, norm_ffn_pre[i]), sh_f, sc_f).reshape(B * S, D)
        moe_args = (moe_w_router[i], moe_b_router[i], moe_w_gate_up[i], moe_b_gate_up[i],
                    moe_w_down[i], moe_b_down[i])
        if last:
            y = moe_ffn(h, *moe_args).reshape(B, S, D)
        else:
            xc = xc + gc_a * rmsnorm(oc, norm_mix_post[i])
            hc = modulate(rmsnorm(xc, norm_ffn_pre[i]), shc_f, scc_f).reshape(B * L, D)
            y_all = moe_ffn(jnp.concatenate([h, hc], axis=0), *moe_args)
            y = y_all[:B * S].reshape(B, S, D)
            xc = xc + gc_f * rmsnorm(y_all[B * S:].reshape(B, L, D), norm_ffn_post[i])
        x = x + g_f * rmsnorm(y, norm_ffn_post[i])
    return x
```

```python
import functools
import math

import jax
import jax.numpy as jnp
import numpy as np
from jax import lax
from jax.experimental import pallas as pl
from jax.experimental.pallas import tpu as pltpu

F32 = jnp.float32
BF16 = jnp.bfloat16
HIGHEST = lax.Precision.HIGHEST

GRID_W = 64
NA_HEADS = 16
NA_WIN_ROWS = 8
NA_WIN_COLS = 16
DIFF_HEADS = 8
DIFF_EPS = 1e-5
ROPE_BASE = 10000.0
N_EXPERTS = 32
TOP_K = 4
SWIGLU_LIMIT = 7.0
SWIGLU_ALPHA = 1.702
NORM_EPS = 1e-6
NEG_BIG = -1e30

LANES = 128
SUBLANES = 8
VMEM_LIMIT = 56 * 1024 * 1024

TM = 256
NA_Q_ROWS = 4
NA_K_BLOCKS = 3
DIFF_TQ = 512
DIFF_TK = 512
EXPERT_ROWS = 256


def _params(semantics, **kw):
    return pltpu.CompilerParams(dimension_semantics=semantics, vmem_limit_bytes=VMEM_LIMIT, **kw)


def _rms(x, w, eps):
    return x * lax.rsqrt(jnp.mean(x * x, axis=-1, keepdims=True) + eps) * w


def _dot(a, b):
    return jnp.dot(a, b, preferred_element_type=F32)


def _dot_nt(a, b):
    return lax.dot_general(a, b, (((1,), (1,)), ((), ())), preferred_element_type=F32)


def _mod_kernel(c_ref, w_ref, b_ref, o_ref):
    cv = c_ref[...]
    act = cv * jax.nn.sigmoid(cv)
    o_ref[0] = jnp.dot(act, w_ref[0], precision=HIGHEST, preferred_element_type=F32) + b_ref[0]


def _adaln_mod(c, c_ctx, w_mod, b_mod):
    depth, d, d6 = w_mod.shape
    batch = c.shape[0]
    assert batch + 1 <= SUBLANES
    cc = jnp.zeros((SUBLANES, d), F32).at[:batch].set(c).at[batch].set(c_ctx)
    tn = d6 // 4
    return pl.pallas_call(
        _mod_kernel,
        grid=(depth, d6 // tn),
        in_specs=[
            pl.BlockSpec((SUBLANES, d), lambda l, j: (0, 0)),
            pl.BlockSpec((1, d, tn), lambda l, j: (l, 0, j)),
            pl.BlockSpec((1, 1, tn), lambda l, j: (l, 0, j)),
        ],
        out_specs=pl.BlockSpec((1, SUBLANES, tn), lambda l, j: (l, 0, j)),
        out_shape=jax.ShapeDtypeStruct((depth, SUBLANES, d6), F32),
        compiler_params=_params(("arbitrary", "arbitrary")),
        name="adaln_mod",
    )(cc, w_mod, b_mod.reshape(depth, 1, d6))


def _mod_spec(d, chunk, tiles_per_batch, batch):
    return pl.BlockSpec((1, 1, d), lambda t: (jnp.minimum(t // tiles_per_batch, batch), 0, chunk))


def _norm_qkv_kernel(x_ref, nw_ref, sh_ref, sc_ref, w_ref, *rest, rope, q_scale):
    if rope:
        cos_ref, s1_ref, s2_ref, q_ref, k_ref, v_ref = rest
    else:
        q_ref, k_ref, v_ref = rest
    d = x_ref.shape[1]
    h = (_rms(x_ref[...], nw_ref[...], NORM_EPS) * (1.0 + sc_ref[0]) + sh_ref[0]).astype(BF16)
    q = _dot(h, w_ref[:, :d])
    k = _dot(h, w_ref[:, d:2 * d])
    v_ref[...] = _dot(h, w_ref[:, 2 * d:]).astype(v_ref.dtype)
    if rope:
        cos, s1, s2 = cos_ref[...], s1_ref[...], s2_ref[...]
        for j in range(d // LANES):
            sl = slice(j * LANES, (j + 1) * LANES)
            for src, dst, scale in ((q, q_ref, q_scale), (k, k_ref, 1.0)):
                xs = src[:, sl]
                rot = xs * cos + pltpu.roll(xs, LANES - 32, 1) * s1 + pltpu.roll(xs, 32, 1) * s2
                dst[:, sl] = (rot * scale).astype(dst.dtype)
    else:
        q_ref[...] = (q * q_scale).astype(q_ref.dtype)
        k_ref[...] = k.astype(k_ref.dtype)


def _norm_qkv(xs, norm_w, mod3, w_qkv, tiles_per_batch, batch, q_scale, rope_tables=None):
    n, d = xs.shape
    rope = rope_tables is not None
    row = lambda t: (t, 0)
    in_specs = [
        pl.BlockSpec((TM, d), row),
        pl.BlockSpec((1, d), lambda t: (0, 0)),
        _mod_spec(d, 0, tiles_per_batch, batch),
        _mod_spec(d, 1, tiles_per_batch, batch),
        pl.BlockSpec((d, 3 * d), lambda t: (0, 0)),
    ]
    args = [xs, norm_w.reshape(1, d), mod3, mod3, w_qkv]
    if rope:
        in_specs += [pl.BlockSpec((TM, LANES), row)] * 3
        args += list(rope_tables)
    out = jax.ShapeDtypeStruct((n, d), BF16)
    return pl.pallas_call(
        functools.partial(_norm_qkv_kernel, rope=rope, q_scale=q_scale),
        grid=(n // TM,),
        in_specs=in_specs,
        out_specs=[pl.BlockSpec((TM, d), row)] * 3,
        out_shape=[out, out, out],
        compiler_params=_params(("parallel",)),
        name="norm_qkv_rope" if rope else "norm_qkv",
    )(*args)


def _rope_tables(seq, batch, n_ctx_rows, head_dim):
    t = jnp.arange(seq, dtype=jnp.int32)
    row = (t // GRID_W).astype(F32)
    col = (t % GRID_W).astype(F32)
    axis_dim = head_dim // 2
    inv_freq = ROPE_BASE ** (-jnp.arange(0, axis_dim, 2, dtype=F32) / axis_dim)
    ang = jnp.concatenate([row[:, None] * inv_freq, col[:, None] * inv_freq], axis=-1)
    cos, sin = jnp.cos(ang), jnp.sin(ang)
    zero = jnp.zeros_like(sin)
    reps = LANES // head_dim

    def lay(first, second, ctx_value):
        tab = jnp.tile(jnp.concatenate([first, second], axis=-1), (batch, reps))
        return jnp.concatenate([tab, jnp.full((n_ctx_rows, LANES), ctx_value, F32)], axis=0)

    return lay(cos, cos, 1.0), lay(-sin, zero, 0.0), lay(zero, sin, 0.0)


def _pair_split_columns(n_heads, head_dim):
    p = np.arange(head_dim)
    old = np.where(p < head_dim // 2, 2 * p, 2 * (p - head_dim // 2) + 1)
    return (np.arange(n_heads)[:, None] * head_dim + old[None, :]).reshape(-1)


def _head_masks():
    lane = lax.broadcasted_iota(jnp.int32, (1, LANES), 1)
    first = lane < (LANES // 2)
    return first, jnp.logical_not(first)


def _na_kernel(q_ref, k0_ref, k1_ref, k2_ref, v0_ref, v1_ref, v2_ref, kc_ref, vc_ref, bias_ref, o_ref):
    q2 = q_ref[...]
    zero = jnp.zeros_like(q2)
    k_refs = (k0_ref, k1_ref, k2_ref)
    v_refs = (v0_ref, v1_ref, v2_ref)
    kb = k0_ref.shape[0]
    outs = []
    for t, sel in enumerate(_head_masks()):
        qm = jnp.where(sel, q2, zero)
        s = [_dot_nt(qm, kr[...]) + bias_ref[0, t, :, j * kb:(j + 1) * kb] for j, kr in enumerate(k_refs)]
        s.append(_dot_nt(qm, kc_ref[...]))
        m = functools.reduce(jnp.maximum, [jnp.max(x, axis=-1, keepdims=True) for x in s])
        p = [jnp.exp(x - m) for x in s]
        l = functools.reduce(jnp.add, [jnp.sum(x, axis=-1, keepdims=True) for x in p])
        acc = _dot(p[-1].astype(BF16), vc_ref[...])
        for pj, vr in zip(p[:-1], v_refs):
            acc = acc + _dot(pj.astype(BF16), vr[...])
        outs.append(acc / l)
    first, _ = _head_masks()
    o_ref[...] = jnp.where(first, outs[0], outs[1]).astype(o_ref.dtype)


def _na_bias_tables(rpb):
    qn, kn = NA_Q_ROWS * GRID_W, NA_K_BLOCKS * NA_Q_ROWS * GRID_W
    qr, qc = np.arange(qn) // GRID_W, np.arange(qn) % GRID_W
    kr, kc = np.arange(kn) // GRID_W, np.arange(kn) % GRID_W
    c0 = np.clip(qc - NA_WIN_COLS // 2, 0, GRID_W - NA_WIN_COLS)
    col_ok = (kc[None, :] >= c0[:, None]) & (kc[None, :] < c0[:, None] + NA_WIN_COLS)
    dc = np.clip(kc[None, :] - qc[:, None] + (NA_WIN_COLS - 1), 0, 2 * NA_WIN_COLS - 2)
    last_start = NA_K_BLOCKS * NA_Q_ROWS - NA_WIN_ROWS
    tabs = []
    for delta, w0 in ((0, np.zeros_like(qr)), (NA_Q_ROWS, qr), (last_start + NA_Q_ROWS, np.full_like(qr, last_start))):
        row_ok = (kr[None, :] >= w0[:, None]) & (kr[None, :] < w0[:, None] + NA_WIN_ROWS)
        dr = np.clip(kr[None, :] - qr[:, None] - delta + (NA_WIN_ROWS - 1), 0, 2 * NA_WIN_ROWS - 2)
        tabs.append(jnp.where((row_ok & col_ok)[None], rpb[:, dr, dc], NEG_BIG))
    tabs.append(jnp.full_like(tabs[0], NEG_BIG))
    return jnp.stack(tabs).astype(F32)


def _na_attention(q, k, v, rpb, batch, seq, n_ctx):
    n, d = q.shape
    qb = NA_Q_ROWS * GRID_W
    nblk = seq // qb
    assert seq % qb == 0 and nblk >= NA_K_BLOCKS and n_ctx == qb and (batch * seq) % n_ctx == 0
    assert seq // GRID_W >= NA_WIN_ROWS and NA_K_BLOCKS * NA_Q_ROWS == NA_Q_ROWS + NA_WIN_ROWS
    bias = _na_bias_tables(rpb)
    ctx_blk0 = batch * seq // n_ctx
    n_pairs = d // LANES

    def kv_spec(j):
        return pl.BlockSpec((qb, LANES), lambda b, c, i: (b * nblk + jnp.clip(i - 1, 0, nblk - NA_K_BLOCKS) + j, c))

    ctx_spec = pl.BlockSpec((n_ctx, LANES), lambda b, c, i: (ctx_blk0 + b, c))
    q_spec = pl.BlockSpec((qb, LANES), lambda b, c, i: (jnp.where(i < nblk, b * nblk + i, ctx_blk0 + b), c))
    variant = lambda i: jnp.where(i == 0, 0, jnp.where(i < nblk - 1, 1, jnp.where(i == nblk - 1, 2, 3)))
    return pl.pallas_call(
        _na_kernel,
        grid=(batch, n_pairs, nblk + 1),
        in_specs=[q_spec] + [kv_spec(j) for j in range(NA_K_BLOCKS)] * 2
        + [ctx_spec, ctx_spec,
           pl.BlockSpec((1, 2, qb, NA_K_BLOCKS * qb), lambda b, c, i: (variant(i), c, 0, 0))],
        out_specs=q_spec,
        out_shape=jax.ShapeDtypeStruct((n, d), BF16),
        compiler_params=_params(("parallel", "parallel", "arbitrary")),
        name="na_attention",
    )(q, k, k, k, v, v, v, k, v, bias)


def _diff_kernel(q_ref, k_ref, v_ref, kc_ref, vc_ref, lq1_ref, lk1_ref, lq2_ref, lk2_ref, sub_ref,
                 o_ref, m_sc, l_sc, acc_sc, *, lambda_init):
    j = pl.program_id(3)
    q2 = q_ref[...]
    zero = jnp.zeros_like(q2)
    qms = [jnp.where(sel, q2, zero) for sel in _head_masks()]

    @pl.when(j == 0)
    def _():
        for t, qm in enumerate(qms):
            s = _dot_nt(qm, kc_ref[...])
            m = jnp.max(s, axis=-1, keepdims=True)
            p = jnp.exp(s - m)
            m_sc[t] = m
            l_sc[t] = jnp.sum(p, axis=-1, keepdims=True)
            acc_sc[t] = _dot(p.astype(BF16), vc_ref[...])

    for t, qm in enumerate(qms):
        s = _dot_nt(qm, k_ref[...])
        m_prev = m_sc[t]
        m_new = jnp.maximum(m_prev, jnp.max(s, axis=-1, keepdims=True))
        alpha = jnp.exp(m_prev - m_new)
        p = jnp.exp(s - m_new)
        m_sc[t] = m_new
        l_sc[t] = alpha * l_sc[t] + jnp.sum(p, axis=-1, keepdims=True)
        acc_sc[t] = alpha * acc_sc[t] + _dot(p.astype(BF16), v_ref[...])

    @pl.when(j == pl.num_programs(3) - 1)
    def _():
        lam = (jnp.exp(jnp.sum(lq1_ref[...] * lk1_ref[...], axis=-1, keepdims=True))
               - jnp.exp(jnp.sum(lq2_ref[...] * lk2_ref[...], axis=-1, keepdims=True)) + lambda_init)
        o = acc_sc[0] / l_sc[0] - lam * (acc_sc[1] / l_sc[1])
        o_ref[...] = (_rms(o, sub_ref[...], DIFF_EPS) * (1.0 - lambda_init)).astype(o_ref.dtype)


def _diff_attention(q, k, v, lq1, lk1, lq2, lk2, subln, lambda_init, batch, seq, n_ctx):
    n, d = q.shape
    n_heads = d // LANES
    tq, tk = min(DIFF_TQ, seq), min(DIFF_TK, seq)
    assert seq % tq == 0 and seq % tk == 0 and (batch * seq) % n_ctx == 0 and subln.shape[-1] == LANES
    nq, nk = seq // tq, seq // tk
    ctx_blk0 = batch * seq // n_ctx
    ctx_spec = pl.BlockSpec((n_ctx, LANES), lambda b, h, i, j: (ctx_blk0 + b, h))
    kv_spec = pl.BlockSpec((tk, LANES), lambda b, h, i, j: (b * nk + j, h))
    q_spec = pl.BlockSpec((tq, LANES), lambda b, h, i, j: (b * nq + i, h))
    vec = lambda a: a.reshape(1, -1).astype(F32)
    vec_spec = lambda a: pl.BlockSpec((1, a.shape[-1]), lambda b, h, i, j: (0, 0))
    return pl.pallas_call(
        functools.partial(_diff_kernel, lambda_init=lambda_init),
        grid=(batch, n_heads, nq, nk),
        in_specs=[q_spec, kv_spec, kv_spec, ctx_spec, ctx_spec,
                  vec_spec(lq1), vec_spec(lk1), vec_spec(lq2), vec_spec(lk2), vec_spec(subln)],
        out_specs=q_spec,
        out_shape=jax.ShapeDtypeStruct((batch * seq, d), BF16),
        scratch_shapes=[pltpu.VMEM((2, tq, 1), F32), pltpu.VMEM((2, tq, 1), F32), pltpu.VMEM((2, tq, LANES), F32)],
        compiler_params=_params(("parallel", "parallel", "parallel", "arbitrary")),
        name="diff_attention",
    )(q, k, v, k, v, vec(lq1), vec(lk1), vec(lq2), vec(lk2), vec(subln))


def _lane_pack(cols, shape, dtype):
    lane = lax.broadcasted_iota(jnp.int32, shape, 1)
    out = jnp.zeros(shape, dtype)
    for kk, col in enumerate(cols):
        out = jnp.where(lane == kk, col.astype(dtype), out)
    return out


def _proj_post_kernel(o_ref, wo_ref, x_ref, npost_ref, g_ref, npre_ref, sh_ref, sc_ref, wr_ref, br_ref,
                      xo_ref, h_ref, idx_ref, gate_ref, rank_ref, cnt_ref, cnt_sc):
    @pl.when(pl.program_id(0) == 0)
    def _():
        cnt_sc[...] = jnp.zeros_like(cnt_sc)

    a = _dot(o_ref[...], wo_ref[...])
    xn = x_ref[...] + g_ref[0] * _rms(a, npost_ref[...], NORM_EPS)
    xo_ref[...] = xn
    h = _rms(xn, npre_ref[...], NORM_EPS) * (1.0 + sc_ref[0]) + sh_ref[0]
    h_ref[...] = h
    logits = jnp.dot(h, wr_ref[...], precision=HIGHEST, preferred_element_type=F32) + br_ref[...]

    tm = logits.shape[0]
    lane = lax.broadcasted_iota(jnp.int32, (tm, LANES), 1).astype(F32)
    vals, idxs, hits = [], [], []
    work = logits
    for _ in range(TOP_K):
        m = jnp.max(work, axis=-1, keepdims=True)
        idx = jnp.min(jnp.where(work == m, lane, float(LANES)), axis=-1, keepdims=True)
        hit = lane == idx
        work = jnp.where(hit, -jnp.inf, work)
        vals.append(m)
        idxs.append(idx)
        hits.append(hit)
    ex = [jnp.exp(vv - vals[0]) for vv in vals]
    den = functools.reduce(jnp.add, ex)
    gates = [e / den for e in ex]

    member = functools.reduce(jnp.logical_or, hits)
    row = lax.broadcasted_iota(jnp.int32, (tm, tm), 0)
    colm = lax.broadcasted_iota(jnp.int32, (tm, tm), 1)
    earlier = (colm < row).astype(BF16)
    before = _dot(earlier, member.astype(BF16)) + cnt_sc[...]
    ranks = [jnp.sum(jnp.where(hit, before, 0.0), axis=-1, keepdims=True) for hit in hits]
    cnt_sc[...] = cnt_sc[...] + jnp.sum(member.astype(F32), axis=0, keepdims=True)

    idx_ref[...] = _lane_pack(idxs, (tm, LANES), jnp.int32)
    gate_ref[...] = _lane_pack(gates, (tm, LANES), F32)
    rank_ref[...] = _lane_pack(ranks, (tm, LANES), jnp.int32)
    cnt_ref[...] = jnp.broadcast_to(cnt_sc[...], cnt_ref.shape)


def _proj_post(o, w_o, xs, norm_post, norm_pre, mod3, w_router, b_router, n_rows, tiles_per_batch, batch):
    d = xs.shape[1]
    n_exp = w_router.shape[1]
    wr = jnp.zeros((d, LANES), F32).at[:, :n_exp].set(w_router)
    br = jnp.full((1, LANES), NEG_BIG, F32).at[0, :n_exp].set(b_router)
    row = lambda t: (t, 0)
    const = lambda t: (0, 0)
    mspec = lambda chunk: _mod_spec(d, chunk, tiles_per_batch, batch)
    wide = lambda dt: jax.ShapeDtypeStruct((n_rows, LANES), dt)
    return pl.pallas_call(
        _proj_post_kernel,
        grid=(n_rows // TM,),
        in_specs=[
            pl.BlockSpec((TM, d), row), pl.BlockSpec((d, d), const), pl.BlockSpec((TM, d), row),
            pl.BlockSpec((1, d), const), mspec(2), pl.BlockSpec((1, d), const), mspec(3), mspec(4),
            pl.BlockSpec((d, LANES), const), pl.BlockSpec((1, LANES), const),
        ],
        out_specs=[pl.BlockSpec((TM, d), row), pl.BlockSpec((TM, d), row),
                   pl.BlockSpec((TM, LANES), row), pl.BlockSpec((TM, LANES), row), pl.BlockSpec((TM, LANES), row),
                   pl.BlockSpec((SUBLANES, LANES), const)],
        out_shape=[jax.ShapeDtypeStruct((n_rows, d), F32), jax.ShapeDtypeStruct((n_rows, d), F32),
                   wide(jnp.int32), wide(F32), wide(jnp.int32), jax.ShapeDtypeStruct((SUBLANES, LANES), F32)],
        scratch_shapes=[pltpu.VMEM((1, LANES), F32)],
        compiler_params=_params(("arbitrary",)),
        name="proj_post_router",
    )(o, w_o, xs, norm_post.reshape(1, d), mod3, norm_pre.reshape(1, d), mod3, mod3, wr, br)


def _row_copy_wait(src_ref, dst_ref, sem, count):
    def body(i, carry):
        pltpu.make_async_copy(src_ref.at[pl.ds(0, 1), :], dst_ref.at[pl.ds(0, 1), :], sem).wait()
        return carry
    lax.fori_loop(0, count, body, 0)


def _dispatch_kernel(dest_ref, h_ref, buf_in_ref, buf_ref, sem):
    del buf_in_ref
    tm = h_ref.shape[0]

    def issue(i, carry):
        for kk in range(TOP_K):
            row = dest_ref[i * TOP_K + kk]
            pltpu.make_async_copy(h_ref.at[pl.ds(i, 1), :], buf_ref.at[pl.ds(row, 1), :], sem).start()
        return carry

    lax.fori_loop(0, tm, issue, 0)
    _row_copy_wait(h_ref, buf_ref, sem, tm * TOP_K)


def _dispatch(h, dest_flat, n_buf_rows):
    n, d = h.shape
    buf0 = jnp.zeros((n_buf_rows, d), h.dtype)
    return pl.pallas_call(
        _dispatch_kernel,
        grid=(n // TM,),
        in_specs=[pl.BlockSpec((TM * TOP_K,), lambda t: (t,), memory_space=pltpu.SMEM),
                  pl.BlockSpec((TM, d), lambda t: (t, 0)),
                  pl.BlockSpec(memory_space=pl.ANY)],
        out_specs=pl.BlockSpec(memory_space=pl.ANY),
        out_shape=jax.ShapeDtypeStruct((n_buf_rows, d), h.dtype),
        scratch_shapes=[pltpu.SemaphoreType.DMA],
        input_output_aliases={2: 0},
        compiler_params=_params(("arbitrary",), has_side_effects=True),
        name="moe_dispatch",
    )(dest_flat, h, buf0)


def _expert_kernel(be_ref, nu_ref, x_ref, wgu_ref, bgu_ref, wd_ref, bd_ref, o_ref, wgu_bf, wd_bf):
    b = pl.program_id(0)
    changed = jnp.logical_or(b == 0, be_ref[b] != be_ref[jnp.maximum(b - 1, 0)])

    @pl.when(changed)
    def _():
        wgu_bf[...] = wgu_ref[0].astype(BF16)
        wd_bf[...] = wd_ref[0].astype(BF16)

    @pl.when(b < nu_ref[0])
    def _():
        f = wd_bf.shape[0]
        gu = _dot(x_ref[...].astype(BF16), wgu_bf[...]) + bgu_ref[0]
        g = jnp.minimum(gu[:, :f], SWIGLU_LIMIT)
        u = jnp.clip(gu[:, f:], -SWIGLU_LIMIT, SWIGLU_LIMIT)
        act = (u + 1.0) * (g * jax.nn.sigmoid(SWIGLU_ALPHA * g))
        o_ref[...] = _dot(act.astype(BF16), wd_bf[...]) + bd_ref[0]

    @pl.when(b >= nu_ref[0])
    def _():
        o_ref[...] = jnp.zeros_like(o_ref)


def _experts(buf, block_expert, n_used, w_gate_up, b_gate_up, w_down, b_down):
    n_rows, d = buf.shape
    n_exp, _, f2 = w_gate_up.shape
    f = w_down.shape[1]
    n_blocks = n_rows // EXPERT_ROWS
    grid_spec = pltpu.PrefetchScalarGridSpec(
        num_scalar_prefetch=2,
        grid=(n_blocks,),
        in_specs=[
            pl.BlockSpec((EXPERT_ROWS, d), lambda b, be, nu: (b, 0)),
            pl.BlockSpec((1, d, f2), lambda b, be, nu: (be[b], 0, 0)),
            pl.BlockSpec((1, 1, f2), lambda b, be, nu: (be[b], 0, 0)),
            pl.BlockSpec((1, f, d), lambda b, be, nu: (be[b], 0, 0)),
            pl.BlockSpec((1, 1, d), lambda b, be, nu: (be[b], 0, 0)),
        ],
        out_specs=pl.BlockSpec((EXPERT_ROWS, d), lambda b, be, nu: (b, 0)),
        scratch_shapes=[pltpu.VMEM((d, f2), BF16), pltpu.VMEM((f, d), BF16)],
    )
    return pl.pallas_call(
        _expert_kernel,
        grid_spec=grid_spec,
        out_shape=jax.ShapeDtypeStruct((n_rows, d), F32),
        compiler_params=_params(("arbitrary",)),
        name="moe_experts",
    )(block_expert, n_used, buf, w_gate_up, b_gate_up.reshape(n_exp, 1, f2), w_down, b_down.reshape(n_exp, 1, d))


def _combine_kernel(dest_ref, gate_ref, x_ref, nw_ref, g_ref, y_ref, xo_ref, rows_sc, sem):
    tm = x_ref.shape[0]

    def issue(i, carry):
        for kk in range(TOP_K):
            row = dest_ref[i * TOP_K + kk]
            pltpu.make_async_copy(y_ref.at[pl.ds(row, 1), :], rows_sc.at[kk, pl.ds(i, 1), :], sem).start()
        return carry

    lax.fori_loop(0, tm, issue, 0)
    _row_copy_wait(y_ref, rows_sc.at[0], sem, tm * TOP_K)
    gates = gate_ref[...]
    y = gates[:, 0:1] * rows_sc[0]
    for kk in range(1, TOP_K):
        y = y + gates[:, kk:kk + 1] * rows_sc[kk]
    xo_ref[...] = x_ref[...] + g_ref[0] * _rms(y, nw_ref[...], NORM_EPS)


def _combine(y_grouped, dest_flat, gates, xs, norm_w, mod3, n_rows, tiles_per_batch, batch):
    d = xs.shape[1]
    row = lambda t: (t, 0)
    return pl.pallas_call(
        _combine_kernel,
        grid=(n_rows // TM,),
        in_specs=[pl.BlockSpec((TM * TOP_K,), lambda t: (t,), memory_space=pltpu.SMEM),
                  pl.BlockSpec((TM, LANES), row), pl.BlockSpec((TM, d), row),
                  pl.BlockSpec((1, d), lambda t: (0, 0)), _mod_spec(d, 5, tiles_per_batch, batch),
                  pl.BlockSpec(memory_space=pl.ANY)],
        out_specs=pl.BlockSpec((TM, d), row),
        out_shape=jax.ShapeDtypeStruct((n_rows, d), F32),
        scratch_shapes=[pltpu.VMEM((TOP_K, TM, d), F32), pltpu.SemaphoreType.DMA],
        compiler_params=_params(("arbitrary",)),
        name="moe_combine",
    )(dest_flat, gates, xs, norm_w.reshape(1, d), mod3, y_grouped)


def _moe_layer(h, idx, gates, rank, cnt, xs, norm_post, mod3, w_gate_up, b_gate_up, w_down, b_down,
               n_rows, tiles_per_batch, batch):
    n_exp = w_gate_up.shape[0]
    blk = EXPERT_ROWS
    counts = cnt[0, :n_exp].astype(jnp.int32)
    padded = (counts + blk - 1) // blk * blk
    pad_ends = jnp.cumsum(padded)
    pad_starts = pad_ends - padded
    dest = (pad_starts[idx[:, :TOP_K]] + rank[:, :TOP_K]).reshape(-1)
    n_blocks = (n_rows * TOP_K + n_exp * (blk - 1) + blk - 1) // blk
    block_start = jnp.arange(n_blocks, dtype=jnp.int32) * blk
    block_expert = jnp.minimum(jnp.searchsorted(pad_ends, block_start, side="right"), n_exp - 1).astype(jnp.int32)
    n_used = (pad_ends[-1:] // blk).astype(jnp.int32)
    buf = _dispatch(h, dest, n_blocks * blk)
    y_grouped = _experts(buf, block_expert, n_used, w_gate_up, b_gate_up, w_down, b_down)
    return _combine(y_grouped, dest, gates, xs, norm_post, mod3, n_rows, tiles_per_batch, batch)


def kernel(x, c, ctx, c_ctx, w_mod, b_mod, norm_mix_pre, norm_mix_post, norm_ffn_pre, norm_ffn_post,
           na_w_qkv, na_w_o, na_rpb,
           diff_w_qkv, diff_w_o, diff_lambda_q1, diff_lambda_k1, diff_lambda_q2, diff_lambda_k2, diff_subln,
           moe_w_router, moe_b_router, moe_w_gate_up, moe_b_gate_up, moe_w_down, moe_b_down):
    batch, seq, d = x.shape
    n_ctx = ctx.shape[1]
    depth = w_mod.shape[0]
    assert depth == 2 and seq % TM == 0 and n_ctx % TM == 0
    n_lat = batch * seq
    n_all = n_lat + batch * n_ctx
    tiles_per_batch = seq // TM
    xs = jnp.concatenate([x.reshape(n_lat, d), ctx.reshape(batch * n_ctx, d)], axis=0)
    mod = _adaln_mod(c, c_ctx, w_mod, b_mod).reshape(depth, SUBLANES, 1, 6 * d)

    na_dh = d // NA_HEADS
    q, k, v = _norm_qkv(xs, norm_mix_pre[0], mod[0], na_w_qkv[0].astype(BF16), tiles_per_batch, batch,
                        q_scale=na_dh ** -0.5)
    o = _na_attention(q, k, v, na_rpb[0], batch, seq, n_ctx)
    xs, h, idx, gates, rank, cnt = _proj_post(o, na_w_o[0].astype(BF16), xs, norm_mix_post[0], norm_ffn_pre[0],
                                              mod[0], moe_w_router[0], moe_b_router[0], n_all, tiles_per_batch, batch)
    xs = _moe_layer(h, idx, gates, rank, cnt, xs, norm_ffn_post[0], mod[0], moe_w_gate_up[0], moe_b_gate_up[0],
                    moe_w_down[0], moe_b_down[0], n_all, tiles_per_batch, batch)

    diff_dh = d // DIFF_HEADS // 2
    lambda_init = 0.8 - 0.6 * math.exp(-0.3 * 1)
    cols = _pair_split_columns(2 * DIFF_HEADS, diff_dh)
    w1 = diff_w_qkv[0]
    w1 = jnp.concatenate([w1[:, :d][:, cols], w1[:, d:2 * d][:, cols], w1[:, 2 * d:]], axis=1).astype(BF16)
    tables = _rope_tables(seq, batch, batch * n_ctx, diff_dh)
    q, k, v = _norm_qkv(xs, norm_mix_pre[1], mod[1], w1, tiles_per_batch, batch,
                        q_scale=diff_dh ** -0.5, rope_tables=tables)
    o = _diff_attention(q, k, v, diff_lambda_q1[0], diff_lambda_k1[0], diff_lambda_q2[0], diff_lambda_k2[0],
                        diff_subln[0], lambda_init, batch, seq, n_ctx)
    xs, h, idx, gates, rank, cnt = _proj_post(o, diff_w_o[0].astype(BF16), xs, norm_mix_post[1], norm_ffn_pre[1],
                                              mod[1], moe_w_router[1], moe_b_router[1], n_lat, tiles_per_batch, batch)
    xs = _moe_layer(h, idx, gates, rank, cnt, xs, norm_ffn_post[1], mod[1], moe_w_gate_up[1], moe_b_gate_up[1],
                    moe_w_down[1], moe_b_down[1], n_lat, tiles_per_batch, batch)
    return xs.reshape(batch, seq, d)
```

```python
import functools
import math

import jax
import jax.numpy as jnp
import numpy as np
from jax import lax
from jax.experimental import pallas as pl
from jax.experimental.pallas import tpu as pltpu

F32 = jnp.float32
BF16 = jnp.bfloat16
HIGHEST = lax.Precision.HIGHEST

GRID_W = 64
NA_HEADS = 16
NA_WIN_ROWS = 8
NA_WIN_COLS = 16
DIFF_HEADS = 8
DIFF_EPS = 1e-5
ROPE_BASE = 10000.0
N_EXPERTS = 32
TOP_K = 4
SWIGLU_LIMIT = 7.0
SWIGLU_ALPHA = 1.702
NORM_EPS = 1e-6
NEG_BIG = -1e30

LANES = 128
SUBLANES = 8
VMEM_LIMIT = 56 * 1024 * 1024

TM = 256
NA_Q_ROWS = 4
NA_K_BLOCKS = 3
DIFF_TQ = 1024
DIFF_TK = 1024
DIFF_STRIP = 256
DIFF_SUM_ROW = LANES
DIFF_VT_ROWS = LANES + 16
EXPERT_ROWS = 256


def _params(semantics, **kw):
    return pltpu.CompilerParams(dimension_semantics=semantics, vmem_limit_bytes=VMEM_LIMIT, **kw)


def _rms(x, w, eps):
    return x * lax.rsqrt(jnp.mean(x * x, axis=-1, keepdims=True) + eps) * w


def _dot(a, b):
    return jnp.dot(a, b, preferred_element_type=F32)


def _dot_nt(a, b):
    return lax.dot_general(a, b, (((1,), (1,)), ((), ())), preferred_element_type=F32)


def _mod_kernel(c_ref, w_ref, b_ref, o_ref):
    cv = c_ref[...]
    act = cv * jax.nn.sigmoid(cv)
    o_ref[0] = jnp.dot(act, w_ref[0], precision=HIGHEST, preferred_element_type=F32) + b_ref[0]


def _adaln_mod(c, c_ctx, w_mod, b_mod):
    depth, d, d6 = w_mod.shape
    batch = c.shape[0]
    assert batch + 1 <= SUBLANES
    cc = jnp.zeros((SUBLANES, d), F32).at[:batch].set(c).at[batch].set(c_ctx)
    tn = d6 // 4
    return pl.pallas_call(
        _mod_kernel,
        grid=(depth, d6 // tn),
        in_specs=[
            pl.BlockSpec((SUBLANES, d), lambda l, j: (0, 0)),
            pl.BlockSpec((1, d, tn), lambda l, j: (l, 0, j)),
            pl.BlockSpec((1, 1, tn), lambda l, j: (l, 0, j)),
        ],
        out_specs=pl.BlockSpec((1, SUBLANES, tn), lambda l, j: (l, 0, j)),
        out_shape=jax.ShapeDtypeStruct((depth, SUBLANES, d6), F32),
        compiler_params=_params(("arbitrary", "arbitrary")),
        name="adaln_mod",
    )(cc, w_mod, b_mod.reshape(depth, 1, d6))


def _mod_spec(d, chunk, tiles_per_batch, batch):
    return pl.BlockSpec((1, 1, d), lambda t: (jnp.minimum(t // tiles_per_batch, batch), 0, chunk))


def _norm_qkv_kernel(x_ref, nw_ref, sh_ref, sc_ref, w_ref, *rest, rope, q_scale):
    if rope:
        wvt_ref, cos_ref, s1_ref, s2_ref, q_ref, k_ref, vt_ref = rest
    else:
        q_ref, k_ref, v_ref = rest
    d = x_ref.shape[1]
    h = (_rms(x_ref[...], nw_ref[...], NORM_EPS) * (1.0 + sc_ref[0]) + sh_ref[0]).astype(BF16)
    q = _dot(h, w_ref[:, :d])
    k = _dot(h, w_ref[:, d:2 * d])
    if rope:
        vt = _dot_nt(wvt_ref[...], h)
        for hh in range(vt_ref.shape[0]):
            vt_ref[hh, :DIFF_SUM_ROW, :] = vt[hh * LANES:(hh + 1) * LANES, :].astype(vt_ref.dtype)
            vt_ref[hh, DIFF_SUM_ROW:, :] = jnp.ones((vt_ref.shape[1] - DIFF_SUM_ROW, vt_ref.shape[2]), vt_ref.dtype)
        cos, s1, s2 = cos_ref[...], s1_ref[...], s2_ref[...]
        for j in range(d // LANES):
            sl = slice(j * LANES, (j + 1) * LANES)
            for src, dst, scale in ((q, q_ref, q_scale), (k, k_ref, 1.0)):
                xs = src[:, sl]
                rot = xs * cos + pltpu.roll(xs, LANES - 32, 1) * s1 + pltpu.roll(xs, 32, 1) * s2
                dst[:, sl] = (rot * scale).astype(dst.dtype)
    else:
        q_ref[...] = (q * q_scale).astype(q_ref.dtype)
        k_ref[...] = k.astype(k_ref.dtype)
        v_ref[...] = _dot(h, w_ref[:, 2 * d:]).astype(v_ref.dtype)


def _norm_qkv(xs, norm_w, mod3, w_qkv, tiles_per_batch, batch, q_scale, rope_tables=None):
    n, d = xs.shape
    rope = rope_tables is not None
    row = lambda t: (t, 0)
    const = lambda t: (0, 0)
    in_specs = [
        pl.BlockSpec((TM, d), row),
        pl.BlockSpec((1, d), const),
        _mod_spec(d, 0, tiles_per_batch, batch),
        _mod_spec(d, 1, tiles_per_batch, batch),
    ]
    args = [xs, norm_w.reshape(1, d), mod3, mod3]
    out = jax.ShapeDtypeStruct((n, d), BF16)
    out_specs = [pl.BlockSpec((TM, d), row)] * 3
    out_shape = [out, out, out]
    if rope:
        in_specs += [pl.BlockSpec((d, 2 * d), const), pl.BlockSpec((d, d), const)] + [pl.BlockSpec((TM, LANES), row)] * 3
        args += [w_qkv[:, :2 * d], w_qkv[:, 2 * d:].T] + list(rope_tables)
        out_specs[2] = pl.BlockSpec((d // LANES, DIFF_VT_ROWS, TM), lambda t: (0, 0, t))
        out_shape[2] = jax.ShapeDtypeStruct((d // LANES, DIFF_VT_ROWS, n), BF16)
    else:
        in_specs.append(pl.BlockSpec((d, 3 * d), const))
        args.append(w_qkv)
    return pl.pallas_call(
        functools.partial(_norm_qkv_kernel, rope=rope, q_scale=q_scale),
        grid=(n // TM,),
        in_specs=in_specs,
        out_specs=out_specs,
        out_shape=out_shape,
        compiler_params=_params(("arbitrary",)),
        name="norm_qkv_rope" if rope else "norm_qkv",
    )(*args)


def _rope_tables(seq, batch, n_ctx_rows, head_dim):
    t = jnp.arange(seq, dtype=jnp.int32)
    row = (t // GRID_W).astype(F32)
    col = (t % GRID_W).astype(F32)
    axis_dim = head_dim // 2
    inv_freq = ROPE_BASE ** (-jnp.arange(0, axis_dim, 2, dtype=F32) / axis_dim)
    ang = jnp.concatenate([row[:, None] * inv_freq, col[:, None] * inv_freq], axis=-1)
    cos, sin = jnp.cos(ang), jnp.sin(ang)
    zero = jnp.zeros_like(sin)
    reps = LANES // head_dim

    def lay(first, second, ctx_value):
        tab = jnp.tile(jnp.concatenate([first, second], axis=-1), (batch, reps))
        return jnp.concatenate([tab, jnp.full((n_ctx_rows, LANES), ctx_value, F32)], axis=0)

    return lay(cos, cos, 1.0), lay(-sin, zero, 0.0), lay(zero, sin, 0.0)


def _pair_split_columns(n_heads, head_dim):
    p = np.arange(head_dim)
    old = np.where(p < head_dim // 2, 2 * p, 2 * (p - head_dim // 2) + 1)
    return (np.arange(n_heads)[:, None] * head_dim + old[None, :]).reshape(-1)


def _head_masks():
    lane = lax.broadcasted_iota(jnp.int32, (1, LANES), 1)
    first = lane < (LANES // 2)
    return first, jnp.logical_not(first)


def _na_kernel(q_ref, k0_ref, k1_ref, k2_ref, v0_ref, v1_ref, v2_ref, kc_ref, vc_ref, bias_ref, o_ref):
    q2 = q_ref[...]
    zero = jnp.zeros_like(q2)
    k_refs = (k0_ref, k1_ref, k2_ref)
    v_refs = (v0_ref, v1_ref, v2_ref)
    kb = k0_ref.shape[0]
    outs = []
    for t, sel in enumerate(_head_masks()):
        qm = jnp.where(sel, q2, zero)
        s = [_dot_nt(qm, kr[...]) + bias_ref[0, t, :, j * kb:(j + 1) * kb] for j, kr in enumerate(k_refs)]
        s.append(_dot_nt(qm, kc_ref[...]))
        m = functools.reduce(jnp.maximum, [jnp.max(x, axis=-1, keepdims=True) for x in s])
        p = [jnp.exp(x - m) for x in s]
        l = functools.reduce(jnp.add, [jnp.sum(x, axis=-1, keepdims=True) for x in p])
        acc = _dot(p[-1].astype(BF16), vc_ref[...])
        for pj, vr in zip(p[:-1], v_refs):
            acc = acc + _dot(pj.astype(BF16), vr[...])
        outs.append(acc / l)
    first, _ = _head_masks()
    o_ref[...] = jnp.where(first, outs[0], outs[1]).astype(o_ref.dtype)


def _na_bias_tables(rpb):
    qn, kn = NA_Q_ROWS * GRID_W, NA_K_BLOCKS * NA_Q_ROWS * GRID_W
    qr, qc = np.arange(qn) // GRID_W, np.arange(qn) % GRID_W
    kr, kc = np.arange(kn) // GRID_W, np.arange(kn) % GRID_W
    c0 = np.clip(qc - NA_WIN_COLS // 2, 0, GRID_W - NA_WIN_COLS)
    col_ok = (kc[None, :] >= c0[:, None]) & (kc[None, :] < c0[:, None] + NA_WIN_COLS)
    n_dr, n_dc = 2 * NA_WIN_ROWS - 1, 2 * NA_WIN_COLS - 1
    last_start = NA_K_BLOCKS * NA_Q_ROWS - NA_WIN_ROWS
    q_rows, k_rows = np.arange(NA_Q_ROWS), np.arange(NA_K_BLOCKS * NA_Q_ROWS)
    sel_dr, oks = [], []
    for delta, w0 in ((0, np.zeros_like(qr)), (NA_Q_ROWS, qr), (last_start + NA_Q_ROWS, np.full_like(qr, last_start))):
        row_ok = (kr[None, :] >= w0[:, None]) & (kr[None, :] < w0[:, None] + NA_WIN_ROWS)
        dr = np.clip(k_rows[None, :] - q_rows[:, None] - delta + (NA_WIN_ROWS - 1), 0, n_dr - 1)
        sel_dr.append(dr[:, :, None] == np.arange(n_dr))
        oks.append(row_ok & col_ok)
    cols = np.arange(GRID_W)
    dc = np.clip(cols[None, :] - cols[:, None] + (NA_WIN_COLS - 1), 0, n_dc - 1)
    sel_dr = jnp.asarray(np.stack(sel_dr), F32)
    sel_dc = jnp.asarray(dc[:, :, None] == np.arange(n_dc), F32)
    by_row = jnp.einsum("vabr,hrc->vhabc", sel_dr, rpb, precision=HIGHEST)
    tabs = jnp.einsum("vhabc,xyc->vhaxby", by_row, sel_dc, precision=HIGHEST)
    tabs = tabs.reshape(len(oks), rpb.shape[0], qn, kn)
    tabs = jnp.where(jnp.asarray(np.stack(oks))[:, None], tabs, NEG_BIG)
    return jnp.concatenate([tabs, jnp.full_like(tabs[:1], NEG_BIG)], axis=0).astype(F32)


def _na_attention(q, k, v, rpb, batch, seq, n_ctx):
    n, d = q.shape
    qb = NA_Q_ROWS * GRID_W
    nblk = seq // qb
    assert seq % qb == 0 and nblk >= NA_K_BLOCKS and n_ctx == qb and (batch * seq) % n_ctx == 0
    assert seq // GRID_W >= NA_WIN_ROWS and NA_K_BLOCKS * NA_Q_ROWS == NA_Q_ROWS + NA_WIN_ROWS
    bias = _na_bias_tables(rpb)
    ctx_blk0 = batch * seq // n_ctx
    n_pairs = d // LANES

    def kv_spec(j):
        return pl.BlockSpec((qb, LANES), lambda b, c, i: (b * nblk + jnp.clip(i - 1, 0, nblk - NA_K_BLOCKS) + j, c))

    ctx_spec = pl.BlockSpec((n_ctx, LANES), lambda b, c, i: (ctx_blk0 + b, c))
    q_spec = pl.BlockSpec((qb, LANES), lambda b, c, i: (jnp.where(i < nblk, b * nblk + i, ctx_blk0 + b), c))
    variant = lambda i: jnp.where(i == 0, 0, jnp.where(i < nblk - 1, 1, jnp.where(i == nblk - 1, 2, 3)))
    return pl.pallas_call(
        _na_kernel,
        grid=(batch, n_pairs, nblk + 1),
        in_specs=[q_spec] + [kv_spec(j) for j in range(NA_K_BLOCKS)] * 2
        + [ctx_spec, ctx_spec,
           pl.BlockSpec((1, 2, qb, NA_K_BLOCKS * qb), lambda b, c, i: (variant(i), c, 0, 0))],
        out_specs=q_spec,
        out_shape=jax.ShapeDtypeStruct((n, d), BF16),
        compiler_params=_params(("arbitrary", "arbitrary", "arbitrary")),
        name="na_attention",
    )(q, k, k, k, v, v, v, k, v, bias)


def _diff_kernel(q_ref, k_ref, vt_ref, kc_ref, vct_ref, lq1_ref, lk1_ref, lq2_ref, lk2_ref, sub_ref,
                 o_ref, m_sc, acc_sc, *, lambda_init):
    j = pl.program_id(3)
    q2 = q_ref[...]
    zero = jnp.zeros_like(q2)
    qms = [jnp.where(sel, q2, zero) for sel in _head_masks()]

    @pl.when(j == 0)
    def _():
        for t, qm in enumerate(qms):
            s = _dot_nt(kc_ref[...], qm)
            m = jnp.max(s, axis=0, keepdims=True)
            m_sc[t] = m
            acc_sc[t] = _dot(vct_ref[0], jnp.exp(s - m).astype(BF16))

    tq = q2.shape[0]
    units = [(t, c) for t in range(2) for c in range(0, tq, DIFF_STRIP)]
    scores = lambda t, c: _dot_nt(k_ref[...], qms[t][c:c + DIFF_STRIP, :])
    s_next = scores(*units[0])
    for u, (t, c) in enumerate(units):
        s = s_next
        if u + 1 < len(units):
            s_next = scores(*units[u + 1])
        cols = slice(c, c + DIFF_STRIP)
        m_prev = m_sc[t, :, cols]
        m_new = jnp.maximum(m_prev, jnp.max(s, axis=0, keepdims=True))
        m_sc[t, :, cols] = m_new
        acc_sc[t, :, cols] = (jnp.exp(m_prev - m_new) * acc_sc[t, :, cols]
                              + _dot(vt_ref[0], jnp.exp(s - m_new).astype(BF16)))

    @pl.when(j == pl.num_programs(3) - 1)
    def _():
        lam = (jnp.exp(jnp.sum(lq1_ref[...] * lk1_ref[...], axis=-1, keepdims=True))
               - jnp.exp(jnp.sum(lq2_ref[...] * lk2_ref[...], axis=-1, keepdims=True)) + lambda_init)
        a1, a2 = acc_sc[0], acc_sc[1]
        sums = slice(DIFF_SUM_ROW, DIFF_SUM_ROW + 1)
        o = a1[:DIFF_SUM_ROW] / a1[sums] - lam * (a2[:DIFF_SUM_ROW] / a2[sums])
        y = o * lax.rsqrt(jnp.mean(o * o, axis=0, keepdims=True) + DIFF_EPS) * sub_ref[...]
        o_ref[...] = (y * (1.0 - lambda_init)).T.astype(o_ref.dtype)


def _diff_attention(q, k, vt, lq1, lk1, lq2, lk2, subln, lambda_init, batch, seq, n_ctx):
    n, d = q.shape
    n_heads, vrows = vt.shape[0], vt.shape[1]
    tq, tk = min(DIFF_TQ, seq), min(DIFF_TK, seq)
    assert seq % tq == 0 and seq % tk == 0 and (batch * seq) % n_ctx == 0 and subln.shape[-1] == LANES
    nq, nk = seq // tq, seq // tk
    ctx_blk0 = batch * seq // n_ctx
    q_spec = pl.BlockSpec((tq, LANES), lambda b, h, i, j: (b * nq + i, h))
    k_spec = pl.BlockSpec((tk, LANES), lambda b, h, i, j: (b * nk + j, h))
    vt_spec = pl.BlockSpec((1, vrows, tk), lambda b, h, i, j: (h, 0, b * nk + j))
    kc_spec = pl.BlockSpec((n_ctx, LANES), lambda b, h, i, j: (ctx_blk0 + b, h))
    vct_spec = pl.BlockSpec((1, vrows, n_ctx), lambda b, h, i, j: (h, 0, ctx_blk0 + b))
    vec = lambda a: a.reshape(1, -1).astype(F32)
    vec_spec = lambda a: pl.BlockSpec((1, a.shape[-1]), lambda b, h, i, j: (0, 0))
    return pl.pallas_call(
        functools.partial(_diff_kernel, lambda_init=lambda_init),
        grid=(batch, n_heads, nq, nk),
        in_specs=[q_spec, k_spec, vt_spec, kc_spec, vct_spec,
                  vec_spec(lq1), vec_spec(lk1), vec_spec(lq2), vec_spec(lk2),
                  pl.BlockSpec((LANES, 1), lambda b, h, i, j: (0, 0))],
        out_specs=q_spec,
        out_shape=jax.ShapeDtypeStruct((batch * seq, d), BF16),
        scratch_shapes=[pltpu.VMEM((2, 1, tq), F32), pltpu.VMEM((2, vrows, tq), F32)],
        compiler_params=_params(("arbitrary", "arbitrary", "arbitrary", "arbitrary")),
        name="diff_attention",
    )(q, k, vt, k, vt, vec(lq1), vec(lk1), vec(lq2), vec(lk2), subln.reshape(LANES, 1).astype(F32))


def _lane_pack(cols, shape, dtype):
    lane = lax.broadcasted_iota(jnp.int32, shape, 1)
    out = jnp.zeros(shape, dtype)
    for kk, col in enumerate(cols):
        out = jnp.where(lane == kk, col.astype(dtype), out)
    return out


def _proj_post_kernel(o_ref, wo_ref, x_ref, npost_ref, g_ref, npre_ref, sh_ref, sc_ref, wr_ref, br_ref,
                      xo_ref, h_ref, idx_ref, gate_ref, rank_ref, cnt_ref, cnt_sc):
    @pl.when(pl.program_id(0) == 0)
    def _():
        cnt_sc[...] = jnp.zeros_like(cnt_sc)

    a = _dot(o_ref[...], wo_ref[...])
    xn = x_ref[...] + g_ref[0] * _rms(a, npost_ref[...], NORM_EPS)
    xo_ref[...] = xn
    h = _rms(xn, npre_ref[...], NORM_EPS) * (1.0 + sc_ref[0]) + sh_ref[0]
    h_ref[...] = h
    logits = jnp.dot(h, wr_ref[...], precision=HIGHEST, preferred_element_type=F32) + br_ref[...]

    tm = logits.shape[0]
    lane = lax.broadcasted_iota(jnp.int32, (tm, LANES), 1).astype(F32)
    vals, idxs, hits = [], [], []
    work = logits
    for _ in range(TOP_K):
        m = jnp.max(work, axis=-1, keepdims=True)
        idx = jnp.min(jnp.where(work == m, lane, float(LANES)), axis=-1, keepdims=True)
        hit = lane == idx
        work = jnp.where(hit, -jnp.inf, work)
        vals.append(m)
        idxs.append(idx)
        hits.append(hit)
    ex = [jnp.exp(vv - vals[0]) for vv in vals]
    den = functools.reduce(jnp.add, ex)
    gates = [e / den for e in ex]

    member = functools.reduce(jnp.logical_or, hits)
    row = lax.broadcasted_iota(jnp.int32, (tm, tm), 0)
    colm = lax.broadcasted_iota(jnp.int32, (tm, tm), 1)
    earlier = (colm < row).astype(BF16)
    before = _dot(earlier, member.astype(BF16)) + cnt_sc[...]
    ranks = [jnp.sum(jnp.where(hit, before, 0.0), axis=-1, keepdims=True) for hit in hits]
    cnt_sc[...] = cnt_sc[...] + jnp.sum(member.astype(F32), axis=0, keepdims=True)

    idx_ref[...] = _lane_pack(idxs, (tm, LANES), jnp.int32)
    gate_ref[...] = _lane_pack(gates, (tm, LANES), F32)
    rank_ref[...] = _lane_pack(ranks, (tm, LANES), jnp.int32)
    cnt_ref[...] = jnp.broadcast_to(cnt_sc[...], cnt_ref.shape)


def _proj_post(o, w_o, xs, norm_post, norm_pre, mod3, w_router, b_router, n_rows, tiles_per_batch, batch):
    d = xs.shape[1]
    n_exp = w_router.shape[1]
    wr = jnp.zeros((d, LANES), F32).at[:, :n_exp].set(w_router)
    br = jnp.full((1, LANES), NEG_BIG, F32).at[0, :n_exp].set(b_router)
    row = lambda t: (t, 0)
    const = lambda t: (0, 0)
    mspec = lambda chunk: _mod_spec(d, chunk, tiles_per_batch, batch)
    wide = lambda dt: jax.ShapeDtypeStruct((n_rows, LANES), dt)
    return pl.pallas_call(
        _proj_post_kernel,
        grid=(n_rows // TM,),
        in_specs=[
            pl.BlockSpec((TM, d), row), pl.BlockSpec((d, d), const), pl.BlockSpec((TM, d), row),
            pl.BlockSpec((1, d), const), mspec(2), pl.BlockSpec((1, d), const), mspec(3), mspec(4),
            pl.BlockSpec((d, LANES), const), pl.BlockSpec((1, LANES), const),
        ],
        out_specs=[pl.BlockSpec((TM, d), row), pl.BlockSpec((TM, d), row),
                   pl.BlockSpec((TM, LANES), row), pl.BlockSpec((TM, LANES), row), pl.BlockSpec((TM, LANES), row),
                   pl.BlockSpec((SUBLANES, LANES), const)],
        out_shape=[jax.ShapeDtypeStruct((n_rows, d), F32), jax.ShapeDtypeStruct((n_rows, d), F32),
                   wide(jnp.int32), wide(F32), wide(jnp.int32), jax.ShapeDtypeStruct((SUBLANES, LANES), F32)],
        scratch_shapes=[pltpu.VMEM((1, LANES), F32)],
        compiler_params=_params(("arbitrary",)),
        name="proj_post_router",
    )(o, w_o, xs, norm_post.reshape(1, d), mod3, norm_pre.reshape(1, d), mod3, mod3, wr, br)


def _for_each_assignment(n_tokens, fn):
    def body(i, carry):
        for kk in range(TOP_K):
            fn(i, kk)
        return carry
    lax.fori_loop(0, n_tokens, body, 0)


def _dispatch_kernel(dest_ref, h_ref, buf_in_ref, buf_ref, sem):
    del buf_in_ref

    def row_copy(i, kk):
        row = dest_ref[i * TOP_K + kk]
        return pltpu.make_async_copy(h_ref.at[pl.ds(i, 1), :], buf_ref.at[pl.ds(row, 1), :], sem)

    _for_each_assignment(h_ref.shape[0], lambda i, kk: row_copy(i, kk).start())
    _for_each_assignment(h_ref.shape[0], lambda i, kk: row_copy(i, kk).wait())


def _dispatch(h, dest_flat, n_buf_rows):
    n, d = h.shape
    buf0 = jnp.zeros((n_buf_rows, d), h.dtype)
    return pl.pallas_call(
        _dispatch_kernel,
        grid=(n // TM,),
        in_specs=[pl.BlockSpec((TM * TOP_K,), lambda t: (t,), memory_space=pltpu.SMEM),
                  pl.BlockSpec((TM, d), lambda t: (t, 0)),
                  pl.BlockSpec(memory_space=pl.ANY)],
        out_specs=pl.BlockSpec(memory_space=pl.ANY),
        out_shape=jax.ShapeDtypeStruct((n_buf_rows, d), h.dtype),
        scratch_shapes=[pltpu.SemaphoreType.DMA],
        input_output_aliases={2: 0},
        compiler_params=_params(("arbitrary",), has_side_effects=True),
        name="moe_dispatch",
    )(dest_flat, h, buf0)


def _expert_kernel(be_ref, nu_ref, x_ref, wgu_ref, bgu_ref, wd_ref, bd_ref, o_ref, wgu_bf, wd_bf):
    b = pl.program_id(0)
    changed = jnp.logical_or(b == 0, be_ref[b] != be_ref[jnp.maximum(b - 1, 0)])

    @pl.when(changed)
    def _():
        wgu_bf[...] = wgu_ref[0].astype(BF16)
        wd_bf[...] = wd_ref[0].astype(BF16)

    @pl.when(b < nu_ref[0])
    def _():
        f = wd_bf.shape[0]
        gu = _dot(x_ref[...].astype(BF16), wgu_bf[...]) + bgu_ref[0]
        g = jnp.minimum(gu[:, :f], SWIGLU_LIMIT)
        u = jnp.clip(gu[:, f:], -SWIGLU_LIMIT, SWIGLU_LIMIT)
        act = (u + 1.0) * (g * jax.nn.sigmoid(SWIGLU_ALPHA * g))
        o_ref[...] = _dot(act.astype(BF16), wd_bf[...]) + bd_ref[0]

    @pl.when(b >= nu_ref[0])
    def _():
        o_ref[...] = jnp.zeros_like(o_ref)


def _experts(buf, block_expert, n_used, w_gate_up, b_gate_up, w_down, b_down):
    n_rows, d = buf.shape
    n_exp, _, f2 = w_gate_up.shape
    f = w_down.shape[1]
    n_blocks = n_rows // EXPERT_ROWS
    grid_spec = pltpu.PrefetchScalarGridSpec(
        num_scalar_prefetch=2,
        grid=(n_blocks,),
        in_specs=[
            pl.BlockSpec((EXPERT_ROWS, d), lambda b, be, nu: (b, 0)),
            pl.BlockSpec((1, d, f2), lambda b, be, nu: (be[b], 0, 0)),
            pl.BlockSpec((1, 1, f2), lambda b, be, nu: (be[b], 0, 0)),
            pl.BlockSpec((1, f, d), lambda b, be, nu: (be[b], 0, 0)),
            pl.BlockSpec((1, 1, d), lambda b, be, nu: (be[b], 0, 0)),
        ],
        out_specs=pl.BlockSpec((EXPERT_ROWS, d), lambda b, be, nu: (b, 0)),
        scratch_shapes=[pltpu.VMEM((d, f2), BF16), pltpu.VMEM((f, d), BF16)],
    )
    return pl.pallas_call(
        _expert_kernel,
        grid_spec=grid_spec,
        out_shape=jax.ShapeDtypeStruct((n_rows, d), F32),
        compiler_params=_params(("arbitrary",)),
        name="moe_experts",
    )(block_expert, n_used, buf, w_gate_up, b_gate_up.reshape(n_exp, 1, f2), w_down, b_down.reshape(n_exp, 1, d))


def _combine_kernel(dest_ref, gate_ref, x_ref, nw_ref, g_ref, y_ref, xo_ref, rows_sc, sem):
    def row_copy(i, kk):
        row = dest_ref[i * TOP_K + kk]
        return pltpu.make_async_copy(y_ref.at[pl.ds(row, 1), :], rows_sc.at[kk, pl.ds(i, 1), :], sem)

    _for_each_assignment(x_ref.shape[0], lambda i, kk: row_copy(i, kk).start())
    _for_each_assignment(x_ref.shape[0], lambda i, kk: row_copy(i, kk).wait())
    gates = gate_ref[...]
    y = gates[:, 0:1] * rows_sc[0]
    for kk in range(1, TOP_K):
        y = y + gates[:, kk:kk + 1] * rows_sc[kk]
    xo_ref[...] = x_ref[...] + g_ref[0] * _rms(y, nw_ref[...], NORM_EPS)


def _combine(y_grouped, dest_flat, gates, xs, norm_w, mod3, n_rows, tiles_per_batch, batch):
    d = xs.shape[1]
    row = lambda t: (t, 0)
    return pl.pallas_call(
        _combine_kernel,
        grid=(n_rows // TM,),
        in_specs=[pl.BlockSpec((TM * TOP_K,), lambda t: (t,), memory_space=pltpu.SMEM),
                  pl.BlockSpec((TM, LANES), row), pl.BlockSpec((TM, d), row),
                  pl.BlockSpec((1, d), lambda t: (0, 0)), _mod_spec(d, 5, tiles_per_batch, batch),
                  pl.BlockSpec(memory_space=pl.ANY)],
        out_specs=pl.BlockSpec((TM, d), row),
        out_shape=jax.ShapeDtypeStruct((n_rows, d), F32),
        scratch_shapes=[pltpu.VMEM((TOP_K, TM, d), F32), pltpu.SemaphoreType.DMA],
        compiler_params=_params(("arbitrary",)),
        name="moe_combine",
    )(dest_flat, gates, xs, norm_w.reshape(1, d), mod3, y_grouped)


def _moe_layer(h, idx, gates, rank, cnt, xs, norm_post, mod3, w_gate_up, b_gate_up, w_down, b_down,
               n_rows, tiles_per_batch, batch):
    n_exp = w_gate_up.shape[0]
    blk = EXPERT_ROWS
    counts = cnt[0, :n_exp].astype(jnp.int32)
    padded = (counts + blk - 1) // blk * blk
    pad_ends = jnp.cumsum(padded)
    pad_starts = pad_ends - padded
    dest = (pad_starts[idx[:, :TOP_K]] + rank[:, :TOP_K]).reshape(-1)
    n_blocks = (n_rows * TOP_K + n_exp * (blk - 1) + blk - 1) // blk
    block_start = jnp.arange(n_blocks, dtype=jnp.int32) * blk
    block_expert = jnp.sum(block_start[:, None] >= pad_ends[None, :], axis=1)
    block_expert = jnp.minimum(block_expert, n_exp - 1).astype(jnp.int32)
    n_used = (pad_ends[-1:] // blk).astype(jnp.int32)
    buf = _dispatch(h, dest, n_blocks * blk)
    y_grouped = _experts(buf, block_expert, n_used, w_gate_up, b_gate_up, w_down, b_down)
    return _combine(y_grouped, dest, gates, xs, norm_post, mod3, n_rows, tiles_per_batch, batch)


def kernel(x, c, ctx, c_ctx, w_mod, b_mod, norm_mix_pre, norm_mix_post, norm_ffn_pre, norm_ffn_post,
           na_w_qkv, na_w_o, na_rpb,
           diff_w_qkv, diff_w_o, diff_lambda_q1, diff_lambda_k1, diff_lambda_q2, diff_lambda_k2, diff_subln,
           moe_w_router, moe_b_router, moe_w_gate_up, moe_b_gate_up, moe_w_down, moe_b_down):
    batch, seq, d = x.shape
    n_ctx = ctx.shape[1]
    depth = w_mod.shape[0]
    assert depth == 2 and seq % TM == 0 and n_ctx % TM == 0
    n_lat = batch * seq
    n_all = n_lat + batch * n_ctx
    tiles_per_batch = seq // TM
    xs = jnp.concatenate([x.reshape(n_lat, d), ctx.reshape(batch * n_ctx, d)], axis=0)
    mod = _adaln_mod(c, c_ctx, w_mod, b_mod).reshape(depth, SUBLANES, 1, 6 * d)

    na_dh = d // NA_HEADS
    q, k, v = _norm_qkv(xs, norm_mix_pre[0], mod[0], na_w_qkv[0].astype(BF16), tiles_per_batch, batch,
                        q_scale=na_dh ** -0.5)
    o = _na_attention(q, k, v, na_rpb[0], batch, seq, n_ctx)
    xs, h, idx, gates, rank, cnt = _proj_post(o, na_w_o[0].astype(BF16), xs, norm_mix_post[0], norm_ffn_pre[0],
                                              mod[0], moe_w_router[0], moe_b_router[0], n_all, tiles_per_batch, batch)
    xs = _moe_layer(h, idx, gates, rank, cnt, xs, norm_ffn_post[0], mod[0], moe_w_gate_up[0], moe_b_gate_up[0],
                    moe_w_down[0], moe_b_down[0], n_all, tiles_per_batch, batch)

    diff_dh = d // DIFF_HEADS // 2
    lambda_init = 0.8 - 0.6 * math.exp(-0.3 * 1)
    cols = _pair_split_columns(2 * DIFF_HEADS, diff_dh)
    w1 = diff_w_qkv[0]
    w1 = jnp.concatenate([w1[:, :d][:, cols], w1[:, d:2 * d][:, cols], w1[:, 2 * d:]], axis=1).astype(BF16)
    tables = _rope_tables(seq, batch, batch * n_ctx, diff_dh)
    q, k, v = _norm_qkv(xs, norm_mix_pre[1], mod[1], w1, tiles_per_batch, batch,
                        q_scale=diff_dh ** -0.5, rope_tables=tables)
    o = _diff_attention(q, k, v, diff_lambda_q1[0], diff_lambda_k1[0], diff_lambda_q2[0], diff_lambda_k2[0],
                        diff_subln[0], lambda_init, batch, seq, n_ctx)
    xs, h, idx, gates, rank, cnt = _proj_post(o, diff_w_o[0].astype(BF16), xs, norm_mix_post[1], norm_ffn_pre[1],
                                              mod[1], moe_w_router[1], moe_b_router[1], n_lat, tiles_per_batch, batch)
    xs = _moe_layer(h, idx, gates, rank, cnt, xs, norm_ffn_post[1], mod[1], moe_w_gate_up[1], moe_b_gate_up[1],
                    moe_w_down[1], moe_b_down[1], n_lat, tiles_per_batch, batch)
    return xs.reshape(batch, seq, d)
```

```python
import functools
import math

import jax
import jax.numpy as jnp
import numpy as np
from jax import lax
from jax.experimental import pallas as pl
from jax.experimental.pallas import tpu as pltpu

F32 = jnp.float32
BF16 = jnp.bfloat16
HIGHEST = lax.Precision.HIGHEST

GRID_W = 64
NA_HEADS = 16
NA_WIN_ROWS = 8
NA_WIN_COLS = 16
DIFF_HEADS = 8
DIFF_EPS = 1e-5
ROPE_BASE = 10000.0
N_EXPERTS = 32
TOP_K = 4
SWIGLU_LIMIT = 7.0
SWIGLU_ALPHA = 1.702
NORM_EPS = 1e-6
NEG_BIG = -1e30
LOG2_E = math.log2(math.e)

LANES = 128
SUBLANES = 8
VMEM_LIMIT = 56 * 1024 * 1024

TM = 256
NA_Q_ROWS = 4
NA_K_BLOCKS = 3
DIFF_TQ = 2048
DIFF_TK = 1024
DIFF_STRIP = 256
DIFF_SUM_ROW = LANES
DIFF_VT_ROWS = LANES + 16
EXPERT_ROWS = 256


def _params(semantics, **kw):
    return pltpu.CompilerParams(dimension_semantics=semantics, vmem_limit_bytes=VMEM_LIMIT, **kw)


def _rms(x, w, eps):
    return x * lax.rsqrt(jnp.mean(x * x, axis=-1, keepdims=True) + eps) * w


def _dot(a, b):
    return jnp.dot(a, b, preferred_element_type=F32)


def _dot_nt(a, b):
    return lax.dot_general(a, b, (((1,), (1,)), ((), ())), preferred_element_type=F32)


def _mod_kernel(c_ref, w_ref, b_ref, o_ref):
    cv = c_ref[...]
    act = cv * jax.nn.sigmoid(cv)
    o_ref[0] = jnp.dot(act, w_ref[0], precision=HIGHEST, preferred_element_type=F32) + b_ref[0]


def _adaln_mod(c, c_ctx, w_mod, b_mod):
    depth, d, d6 = w_mod.shape
    batch = c.shape[0]
    assert batch + 1 <= SUBLANES
    cc = jnp.zeros((SUBLANES, d), F32).at[:batch].set(c).at[batch].set(c_ctx)
    tn = d6 // 4
    return pl.pallas_call(
        _mod_kernel,
        grid=(depth, d6 // tn),
        in_specs=[
            pl.BlockSpec((SUBLANES, d), lambda l, j: (0, 0)),
            pl.BlockSpec((1, d, tn), lambda l, j: (l, 0, j)),
            pl.BlockSpec((1, 1, tn), lambda l, j: (l, 0, j)),
        ],
        out_specs=pl.BlockSpec((1, SUBLANES, tn), lambda l, j: (l, 0, j)),
        out_shape=jax.ShapeDtypeStruct((depth, SUBLANES, d6), F32),
        compiler_params=_params(("arbitrary", "arbitrary")),
        name="adaln_mod",
    )(cc, w_mod, b_mod.reshape(depth, 1, d6))


def _mod_spec(d, chunk, tiles_per_batch, batch):
    return pl.BlockSpec((1, 1, d), lambda t: (jnp.minimum(t // tiles_per_batch, batch), 0, chunk))


def _norm_qkv_kernel(x_ref, nw_ref, sh_ref, sc_ref, w_ref, *rest, rope, q_scale):
    if rope:
        wvt_ref, cos_ref, s1_ref, s2_ref, q_ref, k_ref, vt_ref = rest
    else:
        q_ref, k_ref, v_ref = rest
    d = x_ref.shape[1]
    h = (_rms(x_ref[...], nw_ref[...], NORM_EPS) * (1.0 + sc_ref[0]) + sh_ref[0]).astype(BF16)
    q = _dot(h, w_ref[:, :d])
    k = _dot(h, w_ref[:, d:2 * d])
    if rope:
        vt = _dot_nt(wvt_ref[...], h)
        for hh in range(vt_ref.shape[0]):
            vt_ref[hh, :DIFF_SUM_ROW, :] = vt[hh * LANES:(hh + 1) * LANES, :].astype(vt_ref.dtype)
            vt_ref[hh, DIFF_SUM_ROW:, :] = jnp.ones((vt_ref.shape[1] - DIFF_SUM_ROW, vt_ref.shape[2]), vt_ref.dtype)
        cos, s1, s2 = cos_ref[...], s1_ref[...], s2_ref[...]
        for j in range(d // LANES):
            sl = slice(j * LANES, (j + 1) * LANES)
            for src, dst, scale in ((q, q_ref, q_scale), (k, k_ref, 1.0)):
                xs = src[:, sl]
                rot = xs * cos + pltpu.roll(xs, LANES - 32, 1) * s1 + pltpu.roll(xs, 32, 1) * s2
                dst[j] = (rot * scale).astype(dst.dtype)
    else:
        q_ref[...] = (q * q_scale).astype(q_ref.dtype)
        k_ref[...] = k.astype(k_ref.dtype)
        v_ref[...] = _dot(h, w_ref[:, 2 * d:]).astype(v_ref.dtype)


def _norm_qkv(xs, norm_w, mod3, w_qkv, tiles_per_batch, batch, q_scale, rope_tables=None):
    n, d = xs.shape
    rope = rope_tables is not None
    row = lambda t: (t, 0)
    const = lambda t: (0, 0)
    in_specs = [
        pl.BlockSpec((TM, d), row),
        pl.BlockSpec((1, d), const),
        _mod_spec(d, 0, tiles_per_batch, batch),
        _mod_spec(d, 1, tiles_per_batch, batch),
    ]
    args = [xs, norm_w.reshape(1, d), mod3, mod3]
    out = jax.ShapeDtypeStruct((n, d), BF16)
    out_specs = [pl.BlockSpec((TM, d), row)] * 3
    out_shape = [out, out, out]
    if rope:
        in_specs += [pl.BlockSpec((d, 2 * d), const), pl.BlockSpec((d, d), const)] + [pl.BlockSpec((TM, LANES), row)] * 3
        args += [w_qkv[:, :2 * d], w_qkv[:, 2 * d:].T] + list(rope_tables)
        out_specs[:2] = [pl.BlockSpec((d // LANES, TM, LANES), lambda t: (0, t, 0))] * 2
        out_shape[:2] = [jax.ShapeDtypeStruct((d // LANES, n, LANES), BF16)] * 2
        out_specs[2] = pl.BlockSpec((d // LANES, DIFF_VT_ROWS, TM), lambda t: (0, 0, t))
        out_shape[2] = jax.ShapeDtypeStruct((d // LANES, DIFF_VT_ROWS, n), BF16)
    else:
        in_specs.append(pl.BlockSpec((d, 3 * d), const))
        args.append(w_qkv)
    return pl.pallas_call(
        functools.partial(_norm_qkv_kernel, rope=rope, q_scale=q_scale),
        grid=(n // TM,),
        in_specs=in_specs,
        out_specs=out_specs,
        out_shape=out_shape,
        compiler_params=_params(("arbitrary",)),
        name="norm_qkv_rope" if rope else "norm_qkv",
    )(*args)


def _rope_tables(seq, batch, n_ctx_rows, head_dim):
    t = jnp.arange(seq, dtype=jnp.int32)
    row = (t // GRID_W).astype(F32)
    col = (t % GRID_W).astype(F32)
    axis_dim = head_dim // 2
    inv_freq = ROPE_BASE ** (-jnp.arange(0, axis_dim, 2, dtype=F32) / axis_dim)
    ang = jnp.concatenate([row[:, None] * inv_freq, col[:, None] * inv_freq], axis=-1)
    cos, sin = jnp.cos(ang), jnp.sin(ang)
    zero = jnp.zeros_like(sin)
    reps = LANES // head_dim

    def lay(first, second, ctx_value):
        tab = jnp.tile(jnp.concatenate([first, second], axis=-1), (batch, reps))
        return jnp.concatenate([tab, jnp.full((n_ctx_rows, LANES), ctx_value, F32)], axis=0)

    return lay(cos, cos, 1.0), lay(-sin, zero, 0.0), lay(zero, sin, 0.0)


def _pair_split_columns(n_heads, head_dim):
    p = np.arange(head_dim)
    old = np.where(p < head_dim // 2, 2 * p, 2 * (p - head_dim // 2) + 1)
    return (np.arange(n_heads)[:, None] * head_dim + old[None, :]).reshape(-1)


def _head_masks():
    lane = lax.broadcasted_iota(jnp.int32, (1, LANES), 1)
    first = lane < (LANES // 2)
    return first, jnp.logical_not(first)


def _na_kernel(q_ref, k0_ref, k1_ref, k2_ref, v0_ref, v1_ref, v2_ref, kc_ref, vc_ref, bias_ref, o_ref):
    q2 = q_ref[...]
    zero = jnp.zeros_like(q2)
    k_refs = (k0_ref, k1_ref, k2_ref)
    v_refs = (v0_ref, v1_ref, v2_ref)
    kb = k0_ref.shape[0]
    outs = []
    for t, sel in enumerate(_head_masks()):
        qm = jnp.where(sel, q2, zero)
        s = [_dot_nt(qm, kr[...]) + bias_ref[0, t, :, j * kb:(j + 1) * kb] for j, kr in enumerate(k_refs)]
        s.append(_dot_nt(qm, kc_ref[...]))
        m = functools.reduce(jnp.maximum, [jnp.max(x, axis=-1, keepdims=True) for x in s])
        p = [jnp.exp(x - m) for x in s]
        l = functools.reduce(jnp.add, [jnp.sum(x, axis=-1, keepdims=True) for x in p])
        acc = _dot(p[-1].astype(BF16), vc_ref[...])
        for pj, vr in zip(p[:-1], v_refs):
            acc = acc + _dot(pj.astype(BF16), vr[...])
        outs.append(acc / l)
    first, _ = _head_masks()
    o_ref[...] = jnp.where(first, outs[0], outs[1]).astype(o_ref.dtype)


def _na_bias_tables(rpb):
    qn, kn = NA_Q_ROWS * GRID_W, NA_K_BLOCKS * NA_Q_ROWS * GRID_W
    qr, qc = np.arange(qn) // GRID_W, np.arange(qn) % GRID_W
    kr, kc = np.arange(kn) // GRID_W, np.arange(kn) % GRID_W
    c0 = np.clip(qc - NA_WIN_COLS // 2, 0, GRID_W - NA_WIN_COLS)
    col_ok = (kc[None, :] >= c0[:, None]) & (kc[None, :] < c0[:, None] + NA_WIN_COLS)
    n_dr, n_dc = 2 * NA_WIN_ROWS - 1, 2 * NA_WIN_COLS - 1
    last_start = NA_K_BLOCKS * NA_Q_ROWS - NA_WIN_ROWS
    q_rows, k_rows = np.arange(NA_Q_ROWS), np.arange(NA_K_BLOCKS * NA_Q_ROWS)
    sel_dr, oks = [], []
    for delta, w0 in ((0, np.zeros_like(qr)), (NA_Q_ROWS, qr), (last_start + NA_Q_ROWS, np.full_like(qr, last_start))):
        row_ok = (kr[None, :] >= w0[:, None]) & (kr[None, :] < w0[:, None] + NA_WIN_ROWS)
        dr = np.clip(k_rows[None, :] - q_rows[:, None] - delta + (NA_WIN_ROWS - 1), 0, n_dr - 1)
        sel_dr.append(dr[:, :, None] == np.arange(n_dr))
        oks.append(row_ok & col_ok)
    cols = np.arange(GRID_W)
    dc = np.clip(cols[None, :] - cols[:, None] + (NA_WIN_COLS - 1), 0, n_dc - 1)
    sel_dr = jnp.asarray(np.stack(sel_dr), F32)
    sel_dc = jnp.asarray(dc[:, :, None] == np.arange(n_dc), F32)
    by_row = jnp.einsum("vabr,hrc->vhabc", sel_dr, rpb, precision=HIGHEST)
    tabs = jnp.einsum("vhabc,xyc->vhaxby", by_row, sel_dc, precision=HIGHEST)
    tabs = tabs.reshape(len(oks), rpb.shape[0], qn, kn)
    tabs = jnp.where(jnp.asarray(np.stack(oks))[:, None], tabs, NEG_BIG)
    return jnp.concatenate([tabs, jnp.full_like(tabs[:1], NEG_BIG)], axis=0).astype(F32)


def _na_attention(q, k, v, rpb, batch, seq, n_ctx):
    n, d = q.shape
    qb = NA_Q_ROWS * GRID_W
    nblk = seq // qb
    assert seq % qb == 0 and nblk >= NA_K_BLOCKS and n_ctx == qb and (batch * seq) % n_ctx == 0
    assert seq // GRID_W >= NA_WIN_ROWS and NA_K_BLOCKS * NA_Q_ROWS == NA_Q_ROWS + NA_WIN_ROWS
    bias = _na_bias_tables(rpb)
    ctx_blk0 = batch * seq // n_ctx
    n_pairs = d // LANES

    def kv_spec(j):
        return pl.BlockSpec((qb, LANES), lambda b, c, i: (b * nblk + jnp.clip(i - 1, 0, nblk - NA_K_BLOCKS) + j, c))

    ctx_spec = pl.BlockSpec((n_ctx, LANES), lambda b, c, i: (ctx_blk0 + b, c))
    q_spec = pl.BlockSpec((qb, LANES), lambda b, c, i: (jnp.where(i < nblk, b * nblk + i, ctx_blk0 + b), c))
    variant = lambda i: jnp.where(i == 0, 0, jnp.where(i < nblk - 1, 1, jnp.where(i == nblk - 1, 2, 3)))
    return pl.pallas_call(
        _na_kernel,
        grid=(batch, n_pairs, nblk + 1),
        in_specs=[q_spec] + [kv_spec(j) for j in range(NA_K_BLOCKS)] * 2
        + [ctx_spec, ctx_spec,
           pl.BlockSpec((1, 2, qb, NA_K_BLOCKS * qb), lambda b, c, i: (variant(i), c, 0, 0))],
        out_specs=q_spec,
        out_shape=jax.ShapeDtypeStruct((n, d), BF16),
        compiler_params=_params(("arbitrary", "arbitrary", "arbitrary")),
        name="na_attention",
    )(q, k, k, k, v, v, v, k, v, bias)


def _diff_kernel(q_ref, k_ref, vt_ref, kc_ref, vct_ref, lq1_ref, lk1_ref, lq2_ref, lk2_ref, sub_ref,
                 o_ref, m_sc, acc_sc, *, lambda_init):
    j = pl.program_id(3)
    q2 = q_ref[0]
    zero = jnp.zeros_like(q2)
    qms = [jnp.where(sel, q2, zero) for sel in _head_masks()]

    @pl.when(j == 0)
    def _():
        for t, qm in enumerate(qms):
            s = _dot_nt(kc_ref[0], qm)
            m = jnp.max(s, axis=0, keepdims=True)
            m_sc[t] = m
            acc_sc[t] = _dot(vct_ref[0], jnp.exp2(s - m).astype(BF16))

    tq = q2.shape[0]
    units = [(t, c) for t in range(2) for c in range(0, tq, DIFF_STRIP)]
    scores = lambda t, c: _dot_nt(k_ref[0], qms[t][c:c + DIFF_STRIP, :])
    s_next = scores(*units[0])
    for u, (t, c) in enumerate(units):
        s = s_next
        if u + 1 < len(units):
            s_next = scores(*units[u + 1])
        cols = slice(c, c + DIFF_STRIP)
        m_prev = m_sc[t, :, cols]
        m_new = jnp.maximum(m_prev, jnp.max(s, axis=0, keepdims=True))
        m_sc[t, :, cols] = m_new
        acc_sc[t, :, cols] = (jnp.exp2(m_prev - m_new) * acc_sc[t, :, cols]
                              + _dot(vt_ref[0], jnp.exp2(s - m_new).astype(BF16)))

    @pl.when(j == pl.num_programs(3) - 1)
    def _():
        lam = (jnp.exp(jnp.sum(lq1_ref[...] * lk1_ref[...], axis=-1, keepdims=True))
               - jnp.exp(jnp.sum(lq2_ref[...] * lk2_ref[...], axis=-1, keepdims=True)) + lambda_init)
        a1, a2 = acc_sc[0], acc_sc[1]
        sums = slice(DIFF_SUM_ROW, DIFF_SUM_ROW + 1)
        o = a1[:DIFF_SUM_ROW] / a1[sums] - lam * (a2[:DIFF_SUM_ROW] / a2[sums])
        y = o * lax.rsqrt(jnp.mean(o * o, axis=0, keepdims=True) + DIFF_EPS) * sub_ref[...]
        o_ref[...] = (y * (1.0 - lambda_init)).T.astype(o_ref.dtype)


def _diff_attention(q, k, vt, lq1, lk1, lq2, lk2, subln, lambda_init, batch, seq, n_ctx):
    n_heads, vrows = vt.shape[0], vt.shape[1]
    d = n_heads * LANES
    tq, tk = min(DIFF_TQ, seq), min(DIFF_TK, seq)
    assert seq % tq == 0 and seq % tk == 0 and (batch * seq) % n_ctx == 0 and subln.shape[-1] == LANES
    nq, nk = seq // tq, seq // tk
    ctx_blk0 = batch * seq // n_ctx
    q_spec = pl.BlockSpec((1, tq, LANES), lambda b, h, i, j: (h, b * nq + i, 0))
    k_spec = pl.BlockSpec((1, tk, LANES), lambda b, h, i, j: (h, b * nk + j, 0))
    vt_spec = pl.BlockSpec((1, vrows, tk), lambda b, h, i, j: (h, 0, b * nk + j))
    kc_spec = pl.BlockSpec((1, n_ctx, LANES), lambda b, h, i, j: (h, ctx_blk0 + b, 0))
    vct_spec = pl.BlockSpec((1, vrows, n_ctx), lambda b, h, i, j: (h, 0, ctx_blk0 + b))
    vec = lambda a: a.reshape(1, -1).astype(F32)
    vec_spec = lambda a: pl.BlockSpec((1, a.shape[-1]), lambda b, h, i, j: (0, 0))
    return pl.pallas_call(
        functools.partial(_diff_kernel, lambda_init=lambda_init),
        grid=(batch, n_heads, nq, nk),
        in_specs=[q_spec, k_spec, vt_spec, kc_spec, vct_spec,
                  vec_spec(lq1), vec_spec(lk1), vec_spec(lq2), vec_spec(lk2),
                  pl.BlockSpec((LANES, 1), lambda b, h, i, j: (0, 0))],
        out_specs=pl.BlockSpec((tq, LANES), lambda b, h, i, j: (b * nq + i, h)),
        out_shape=jax.ShapeDtypeStruct((batch * seq, d), BF16),
        scratch_shapes=[pltpu.VMEM((2, 1, tq), F32), pltpu.VMEM((2, vrows, tq), F32)],
        compiler_params=_params(("arbitrary", "arbitrary", "arbitrary", "arbitrary")),
        name="diff_attention",
    )(q, k, vt, k, vt, vec(lq1), vec(lk1), vec(lq2), vec(lk2), subln.reshape(LANES, 1).astype(F32))


def _lane_pack(cols, shape, dtype):
    lane = lax.broadcasted_iota(jnp.int32, shape, 1)
    out = jnp.zeros(shape, dtype)
    for kk, col in enumerate(cols):
        out = jnp.where(lane == kk, col.astype(dtype), out)
    return out


def _proj_post_kernel(o_ref, wo_ref, x_ref, npost_ref, g_ref, npre_ref, sh_ref, sc_ref, wr_ref, br_ref,
                      xo_ref, h_ref, idx_ref, gate_ref, rank_ref, cnt_ref, cnt_sc):
    @pl.when(pl.program_id(0) == 0)
    def _():
        cnt_sc[...] = jnp.zeros_like(cnt_sc)

    a = _dot(o_ref[...], wo_ref[...])
    xn = x_ref[...] + g_ref[0] * _rms(a, npost_ref[...], NORM_EPS)
    xo_ref[...] = xn
    h = _rms(xn, npre_ref[...], NORM_EPS) * (1.0 + sc_ref[0]) + sh_ref[0]
    h_ref[...] = h
    logits = jnp.dot(h, wr_ref[...], precision=HIGHEST, preferred_element_type=F32) + br_ref[...]

    tm = logits.shape[0]
    lane = lax.broadcasted_iota(jnp.int32, (tm, LANES), 1).astype(F32)
    vals, idxs, hits = [], [], []
    work = logits
    for _ in range(TOP_K):
        m = jnp.max(work, axis=-1, keepdims=True)
        idx = jnp.min(jnp.where(work == m, lane, float(LANES)), axis=-1, keepdims=True)
        hit = lane == idx
        work = jnp.where(hit, -jnp.inf, work)
        vals.append(m)
        idxs.append(idx)
        hits.append(hit)
    ex = [jnp.exp(vv - vals[0]) for vv in vals]
    den = functools.reduce(jnp.add, ex)
    gates = [e / den for e in ex]

    member = functools.reduce(jnp.logical_or, hits)
    row = lax.broadcasted_iota(jnp.int32, (tm, tm), 0)
    colm = lax.broadcasted_iota(jnp.int32, (tm, tm), 1)
    earlier = (colm < row).astype(BF16)
    before = _dot(earlier, member.astype(BF16)) + cnt_sc[...]
    ranks = [jnp.sum(jnp.where(hit, before, 0.0), axis=-1, keepdims=True) for hit in hits]
    cnt_sc[...] = cnt_sc[...] + jnp.sum(member.astype(F32), axis=0, keepdims=True)

    idx_ref[...] = _lane_pack(idxs, (tm, LANES), jnp.int32)
    gate_ref[...] = _lane_pack(gates, (tm, LANES), F32)
    rank_ref[...] = _lane_pack(ranks, (tm, LANES), jnp.int32)
    cnt_ref[...] = jnp.broadcast_to(cnt_sc[...], cnt_ref.shape)


def _proj_post(o, w_o, xs, norm_post, norm_pre, mod3, w_router, b_router, n_rows, tiles_per_batch, batch):
    d = xs.shape[1]
    n_exp = w_router.shape[1]
    wr = jnp.zeros((d, LANES), F32).at[:, :n_exp].set(w_router)
    br = jnp.full((1, LANES), NEG_BIG, F32).at[0, :n_exp].set(b_router)
    row = lambda t: (t, 0)
    const = lambda t: (0, 0)
    mspec = lambda chunk: _mod_spec(d, chunk, tiles_per_batch, batch)
    wide = lambda dt: jax.ShapeDtypeStruct((n_rows, LANES), dt)
    return pl.pallas_call(
        _proj_post_kernel,
        grid=(n_rows // TM,),
        in_specs=[
            pl.BlockSpec((TM, d), row), pl.BlockSpec((d, d), const), pl.BlockSpec((TM, d), row),
            pl.BlockSpec((1, d), const), mspec(2), pl.BlockSpec((1, d), const), mspec(3), mspec(4),
            pl.BlockSpec((d, LANES), const), pl.BlockSpec((1, LANES), const),
        ],
        out_specs=[pl.BlockSpec((TM, d), row), pl.BlockSpec((TM, d), row),
                   pl.BlockSpec((TM, LANES), row), pl.BlockSpec((TM, LANES), row), pl.BlockSpec((TM, LANES), row),
                   pl.BlockSpec((SUBLANES, LANES), const)],
        out_shape=[jax.ShapeDtypeStruct((n_rows, d), F32), jax.ShapeDtypeStruct((n_rows, d), F32),
                   wide(jnp.int32), wide(F32), wide(jnp.int32), jax.ShapeDtypeStruct((SUBLANES, LANES), F32)],
        scratch_shapes=[pltpu.VMEM((1, LANES), F32)],
        compiler_params=_params(("arbitrary",)),
        name="proj_post_router",
    )(o, w_o, xs, norm_post.reshape(1, d), mod3, norm_pre.reshape(1, d), mod3, mod3, wr, br)


def _for_each_assignment(n_tokens, fn):
    def body(i, carry):
        for kk in range(TOP_K):
            fn(i, kk)
        return carry
    lax.fori_loop(0, n_tokens, body, 0)


def _dispatch_kernel(dest_ref, h_ref, buf_in_ref, buf_ref, sem):
    del buf_in_ref

    def row_copy(i, kk):
        row = dest_ref[i * TOP_K + kk]
        return pltpu.make_async_copy(h_ref.at[pl.ds(i, 1), :], buf_ref.at[pl.ds(row, 1), :], sem)

    tm = h_ref.shape[0]
    _for_each_assignment(tm, lambda i, kk: row_copy(i, kk).start())
    for _ in range(TOP_K):
        pltpu.make_async_copy(h_ref, buf_ref.at[pl.ds(0, tm), :], sem).wait()


def _dispatch(h, dest_flat, n_buf_rows):
    n, d = h.shape
    buf0 = jnp.zeros((n_buf_rows, d), h.dtype)
    return pl.pallas_call(
        _dispatch_kernel,
        grid=(n // TM,),
        in_specs=[pl.BlockSpec((TM * TOP_K,), lambda t: (t,), memory_space=pltpu.SMEM),
                  pl.BlockSpec((TM, d), lambda t: (t, 0)),
                  pl.BlockSpec(memory_space=pl.ANY)],
        out_specs=pl.BlockSpec(memory_space=pl.ANY),
        out_shape=jax.ShapeDtypeStruct((n_buf_rows, d), h.dtype),
        scratch_shapes=[pltpu.SemaphoreType.DMA],
        input_output_aliases={2: 0},
        compiler_params=_params(("arbitrary",), has_side_effects=True),
        name="moe_dispatch",
    )(dest_flat, h, buf0)


def _expert_kernel(be_ref, nu_ref, x_ref, wgu_ref, bgu_ref, wd_ref, bd_ref, o_ref, wgu_bf, wd_bf):
    b = pl.program_id(0)
    changed = jnp.logical_or(b == 0, be_ref[b] != be_ref[jnp.maximum(b - 1, 0)])

    @pl.when(changed)
    def _():
        wgu_bf[...] = wgu_ref[0].astype(BF16)
        wd_bf[...] = wd_ref[0].astype(BF16)

    @pl.when(b < nu_ref[0])
    def _():
        f = wd_bf.shape[0]
        gu = _dot(x_ref[...].astype(BF16), wgu_bf[...]) + bgu_ref[0]
        g = jnp.minimum(gu[:, :f], SWIGLU_LIMIT)
        u = jnp.clip(gu[:, f:], -SWIGLU_LIMIT, SWIGLU_LIMIT)
        act = (u + 1.0) * (g * jax.nn.sigmoid(SWIGLU_ALPHA * g))
        o_ref[...] = _dot(act.astype(BF16), wd_bf[...]) + bd_ref[0]

    @pl.when(b >= nu_ref[0])
    def _():
        o_ref[...] = jnp.zeros_like(o_ref)


def _experts(buf, block_expert, n_used, w_gate_up, b_gate_up, w_down, b_down):
    n_rows, d = buf.shape
    n_exp, _, f2 = w_gate_up.shape
    f = w_down.shape[1]
    n_blocks = n_rows // EXPERT_ROWS
    grid_spec = pltpu.PrefetchScalarGridSpec(
        num_scalar_prefetch=2,
        grid=(n_blocks,),
        in_specs=[
            pl.BlockSpec((EXPERT_ROWS, d), lambda b, be, nu: (b, 0)),
            pl.BlockSpec((1, d, f2), lambda b, be, nu: (be[b], 0, 0)),
            pl.BlockSpec((1, 1, f2), lambda b, be, nu: (be[b], 0, 0)),
            pl.BlockSpec((1, f, d), lambda b, be, nu: (be[b], 0, 0)),
            pl.BlockSpec((1, 1, d), lambda b, be, nu: (be[b], 0, 0)),
        ],
        out_specs=pl.BlockSpec((EXPERT_ROWS, d), lambda b, be, nu: (b, 0)),
        scratch_shapes=[pltpu.VMEM((d, f2), BF16), pltpu.VMEM((f, d), BF16)],
    )
    return pl.pallas_call(
        _expert_kernel,
        grid_spec=grid_spec,
        out_shape=jax.ShapeDtypeStruct((n_rows, d), F32),
        compiler_params=_params(("arbitrary",)),
        name="moe_experts",
    )(block_expert, n_used, buf, w_gate_up, b_gate_up.reshape(n_exp, 1, f2), w_down, b_down.reshape(n_exp, 1, d))


def _combine_kernel(dest_ref, gate_ref, x_ref, nw_ref, g_ref, y_ref, xo_ref, rows_sc, sem):
    def row_copy(i, kk):
        row = dest_ref[i * TOP_K + kk]
        return pltpu.make_async_copy(y_ref.at[pl.ds(row, 1), :], rows_sc.at[kk, pl.ds(i, 1), :], sem)

    tm = x_ref.shape[0]
    _for_each_assignment(tm, lambda i, kk: row_copy(i, kk).start())
    for kk in range(TOP_K):
        pltpu.make_async_copy(y_ref.at[pl.ds(0, tm), :], rows_sc.at[kk], sem).wait()
    gates = gate_ref[...]
    y = gates[:, 0:1] * rows_sc[0]
    for kk in range(1, TOP_K):
        y = y + gates[:, kk:kk + 1] * rows_sc[kk]
    xo_ref[...] = x_ref[...] + g_ref[0] * _rms(y, nw_ref[...], NORM_EPS)


def _combine(y_grouped, dest_flat, gates, xs, norm_w, mod3, n_rows, tiles_per_batch, batch):
    d = xs.shape[1]
    row = lambda t: (t, 0)
    return pl.pallas_call(
        _combine_kernel,
        grid=(n_rows // TM,),
        in_specs=[pl.BlockSpec((TM * TOP_K,), lambda t: (t,), memory_space=pltpu.SMEM),
                  pl.BlockSpec((TM, LANES), row), pl.BlockSpec((TM, d), row),
                  pl.BlockSpec((1, d), lambda t: (0, 0)), _mod_spec(d, 5, tiles_per_batch, batch),
                  pl.BlockSpec(memory_space=pl.ANY)],
        out_specs=pl.BlockSpec((TM, d), row),
        out_shape=jax.ShapeDtypeStruct((n_rows, d), F32),
        scratch_shapes=[pltpu.VMEM((TOP_K, TM, d), F32), pltpu.SemaphoreType.DMA],
        compiler_params=_params(("arbitrary",)),
        name="moe_combine",
    )(dest_flat, gates, xs, norm_w.reshape(1, d), mod3, y_grouped)


def _moe_layer(h, idx, gates, rank, cnt, xs, norm_post, mod3, w_gate_up, b_gate_up, w_down, b_down,
               n_rows, tiles_per_batch, batch):
    n_exp = w_gate_up.shape[0]
    blk = EXPERT_ROWS
    counts = cnt[0, :n_exp].astype(jnp.int32)
    padded = (counts + blk - 1) // blk * blk
    pad_ends = jnp.cumsum(padded)
    pad_starts = pad_ends - padded
    dest = (pad_starts[idx[:, :TOP_K]] + rank[:, :TOP_K]).reshape(-1)
    n_blocks = (n_rows * TOP_K + n_exp * (blk - 1) + blk - 1) // blk
    block_start = jnp.arange(n_blocks, dtype=jnp.int32) * blk
    block_expert = jnp.sum(block_start[:, None] >= pad_ends[None, :], axis=1)
    block_expert = jnp.minimum(block_expert, n_exp - 1).astype(jnp.int32)
    n_used = (pad_ends[-1:] // blk).astype(jnp.int32)
    buf = _dispatch(h, dest, n_blocks * blk)
    y_grouped = _experts(buf, block_expert, n_used, w_gate_up, b_gate_up, w_down, b_down)
    return _combine(y_grouped, dest, gates, xs, norm_post, mod3, n_rows, tiles_per_batch, batch)


def kernel(x, c, ctx, c_ctx, w_mod, b_mod, norm_mix_pre, norm_mix_post, norm_ffn_pre, norm_ffn_post,
           na_w_qkv, na_w_o, na_rpb,
           diff_w_qkv, diff_w_o, diff_lambda_q1, diff_lambda_k1, diff_lambda_q2, diff_lambda_k2, diff_subln,
           moe_w_router, moe_b_router, moe_w_gate_up, moe_b_gate_up, moe_w_down, moe_b_down):
    batch, seq, d = x.shape
    n_ctx = ctx.shape[1]
    depth = w_mod.shape[0]
    assert depth == 2 and seq % TM == 0 and n_ctx % TM == 0
    n_lat = batch * seq
    n_all = n_lat + batch * n_ctx
    tiles_per_batch = seq // TM
    xs = jnp.concatenate([x.reshape(n_lat, d), ctx.reshape(batch * n_ctx, d)], axis=0)
    mod = _adaln_mod(c, c_ctx, w_mod, b_mod).reshape(depth, SUBLANES, 1, 6 * d)

    na_dh = d // NA_HEADS
    q, k, v = _norm_qkv(xs, norm_mix_pre[0], mod[0], na_w_qkv[0].astype(BF16), tiles_per_batch, batch,
                        q_scale=na_dh ** -0.5)
    o = _na_attention(q, k, v, na_rpb[0], batch, seq, n_ctx)
    xs, h, idx, gates, rank, cnt = _proj_post(o, na_w_o[0].astype(BF16), xs, norm_mix_post[0], norm_ffn_pre[0],
                                              mod[0], moe_w_router[0], moe_b_router[0], n_all, tiles_per_batch, batch)
    xs = _moe_layer(h, idx, gates, rank, cnt, xs, norm_ffn_post[0], mod[0], moe_w_gate_up[0], moe_b_gate_up[0],
                    moe_w_down[0], moe_b_down[0], n_all, tiles_per_batch, batch)

    diff_dh = d // DIFF_HEADS // 2
    lambda_init = 0.8 - 0.6 * math.exp(-0.3 * 1)
    cols = _pair_split_columns(2 * DIFF_HEADS, diff_dh)
    w1 = diff_w_qkv[0]
    w1 = jnp.concatenate([w1[:, :d][:, cols], w1[:, d:2 * d][:, cols], w1[:, 2 * d:]], axis=1).astype(BF16)
    tables = _rope_tables(seq, batch, batch * n_ctx, diff_dh)
    q, k, v = _norm_qkv(xs, norm_mix_pre[1], mod[1], w1, tiles_per_batch, batch,
                        q_scale=diff_dh ** -0.5 * LOG2_E, rope_tables=tables)
    o = _diff_attention(q, k, v, diff_lambda_q1[0], diff_lambda_k1[0], diff_lambda_q2[0], diff_lambda_k2[0],
                        diff_subln[0], lambda_init, batch, seq, n_ctx)
    xs, h, idx, gates, rank, cnt = _proj_post(o, diff_w_o[0].astype(BF16), xs, norm_mix_post[1], norm_ffn_pre[1],
                                              mod[1], moe_w_router[1], moe_b_router[1], n_lat, tiles_per_batch, batch)
    xs = _moe_layer(h, idx, gates, rank, cnt, xs, norm_ffn_post[1], mod[1], moe_w_gate_up[1], moe_b_gate_up[1],
                    moe_w_down[1], moe_b_down[1], n_lat, tiles_per_batch, batch)
    return xs.reshape(batch, seq, d)
```

```python
import functools
import math

import jax
import jax.numpy as jnp
import numpy as np
from jax import lax
from jax.experimental import pallas as pl
from jax.experimental.pallas import tpu as pltpu

F32 = jnp.float32
BF16 = jnp.bfloat16
HIGHEST = lax.Precision.HIGHEST

GRID_W = 64
NA_HEADS = 16
NA_WIN_ROWS = 8
NA_WIN_COLS = 16
DIFF_HEADS = 8
DIFF_EPS = 1e-5
ROPE_BASE = 10000.0
N_EXPERTS = 32
TOP_K = 4
SWIGLU_LIMIT = 7.0
SWIGLU_ALPHA = 1.702
NORM_EPS = 1e-6
NEG_BIG = -1e30
LOG2_E = math.log2(math.e)

LANES = 128
SUBLANES = 8
VMEM_LIMIT = 56 * 1024 * 1024

TM = 256
NA_Q_ROWS = 4
NA_K_BLOCKS = 3
DIFF_TQ = 2048
DIFF_TK = 1024
DIFF_STRIP = 1024
EXPERT_ROWS = 256


def _params(semantics, **kw):
    return pltpu.CompilerParams(dimension_semantics=semantics, vmem_limit_bytes=VMEM_LIMIT, **kw)


def _rms(x, w, eps):
    return x * lax.rsqrt(jnp.mean(x * x, axis=-1, keepdims=True) + eps) * w


def _dot(a, b):
    return jnp.dot(a, b, preferred_element_type=F32)


def _dot_nt(a, b):
    return lax.dot_general(a, b, (((1,), (1,)), ((), ())), preferred_element_type=F32)


def _mod_kernel(c_ref, w_ref, b_ref, o_ref):
    cv = c_ref[...]
    act = cv * jax.nn.sigmoid(cv)
    o_ref[0] = jnp.dot(act, w_ref[0], precision=HIGHEST, preferred_element_type=F32) + b_ref[0]


def _adaln_mod(c, c_ctx, w_mod, b_mod):
    depth, d, d6 = w_mod.shape
    batch = c.shape[0]
    assert batch + 1 <= SUBLANES
    cc = jnp.zeros((SUBLANES, d), F32).at[:batch].set(c).at[batch].set(c_ctx)
    tn = d6 // 4
    return pl.pallas_call(
        _mod_kernel,
        grid=(depth, d6 // tn),
        in_specs=[
            pl.BlockSpec((SUBLANES, d), lambda l, j: (0, 0)),
            pl.BlockSpec((1, d, tn), lambda l, j: (l, 0, j)),
            pl.BlockSpec((1, 1, tn), lambda l, j: (l, 0, j)),
        ],
        out_specs=pl.BlockSpec((1, SUBLANES, tn), lambda l, j: (l, 0, j)),
        out_shape=jax.ShapeDtypeStruct((depth, SUBLANES, d6), F32),
        compiler_params=_params(("arbitrary", "arbitrary")),
        name="adaln_mod",
    )(cc, w_mod, b_mod.reshape(depth, 1, d6))


def _mod_spec(d, chunk, tiles_per_batch, batch):
    return pl.BlockSpec((1, 1, d), lambda t: (jnp.minimum(t // tiles_per_batch, batch), 0, chunk))


def _norm_qkv_kernel(x_ref, nw_ref, sh_ref, sc_ref, w_ref, *rest, rope, q_scale):
    if rope:
        cos_ref, s1_ref, s2_ref, q_ref, k_ref, v_ref = rest
    else:
        q_ref, k_ref, v_ref = rest
    d = x_ref.shape[1]
    h = (_rms(x_ref[...], nw_ref[...], NORM_EPS) * (1.0 + sc_ref[0]) + sh_ref[0]).astype(BF16)
    q = _dot(h, w_ref[:, :d])
    k = _dot(h, w_ref[:, d:2 * d])
    v = _dot(h, w_ref[:, 2 * d:])
    if rope:
        for hh in range(v_ref.shape[0]):
            v_ref[hh, :, :LANES] = v[:, hh * LANES:(hh + 1) * LANES].astype(v_ref.dtype)
            v_ref[hh, :, LANES:] = jnp.ones((v_ref.shape[1], LANES), v_ref.dtype)
        cos, s1, s2 = cos_ref[...], s1_ref[...], s2_ref[...]
        for j in range(d // LANES):
            sl = slice(j * LANES, (j + 1) * LANES)
            for src, dst, scale in ((q, q_ref, q_scale), (k, k_ref, 1.0)):
                xs = src[:, sl]
                rot = xs * cos + pltpu.roll(xs, LANES - 32, 1) * s1 + pltpu.roll(xs, 32, 1) * s2
                dst[j] = (rot * scale).astype(dst.dtype)
    else:
        q_ref[...] = (q * q_scale).astype(q_ref.dtype)
        k_ref[...] = k.astype(k_ref.dtype)
        v_ref[...] = v.astype(v_ref.dtype)


def _norm_qkv(xs, norm_w, mod3, w_qkv, tiles_per_batch, batch, q_scale, rope_tables=None):
    n, d = xs.shape
    rope = rope_tables is not None
    row = lambda t: (t, 0)
    const = lambda t: (0, 0)
    in_specs = [
        pl.BlockSpec((TM, d), row),
        pl.BlockSpec((1, d), const),
        _mod_spec(d, 0, tiles_per_batch, batch),
        _mod_spec(d, 1, tiles_per_batch, batch),
        pl.BlockSpec((d, 3 * d), const),
    ]
    args = [xs, norm_w.reshape(1, d), mod3, mod3, w_qkv]
    out = jax.ShapeDtypeStruct((n, d), BF16)
    out_specs = [pl.BlockSpec((TM, d), row)] * 3
    out_shape = [out, out, out]
    if rope:
        in_specs += [pl.BlockSpec((TM, LANES), row)] * 3
        args += list(rope_tables)
        heads = d // LANES
        out_specs = [pl.BlockSpec((heads, TM, w), lambda t: (0, t, 0)) for w in (LANES, LANES, 2 * LANES)]
        out_shape = [jax.ShapeDtypeStruct((heads, n, w), BF16) for w in (LANES, LANES, 2 * LANES)]
    return pl.pallas_call(
        functools.partial(_norm_qkv_kernel, rope=rope, q_scale=q_scale),
        grid=(n // TM,),
        in_specs=in_specs,
        out_specs=out_specs,
        out_shape=out_shape,
        compiler_params=_params(("arbitrary",)),
        name="norm_qkv_rope" if rope else "norm_qkv",
    )(*args)


def _rope_tables(seq, batch, n_ctx_rows, head_dim):
    t = jnp.arange(seq, dtype=jnp.int32)
    row = (t // GRID_W).astype(F32)
    col = (t % GRID_W).astype(F32)
    axis_dim = head_dim // 2
    inv_freq = ROPE_BASE ** (-jnp.arange(0, axis_dim, 2, dtype=F32) / axis_dim)
    ang = jnp.concatenate([row[:, None] * inv_freq, col[:, None] * inv_freq], axis=-1)
    cos, sin = jnp.cos(ang), jnp.sin(ang)
    zero = jnp.zeros_like(sin)
    reps = LANES // head_dim

    def lay(first, second, ctx_value):
        tab = jnp.tile(jnp.concatenate([first, second], axis=-1), (batch, reps))
        return jnp.concatenate([tab, jnp.full((n_ctx_rows, LANES), ctx_value, F32)], axis=0)

    return lay(cos, cos, 1.0), lay(-sin, zero, 0.0), lay(zero, sin, 0.0)


def _pair_split_columns(n_heads, head_dim):
    p = np.arange(head_dim)
    old = np.where(p < head_dim // 2, 2 * p, 2 * (p - head_dim // 2) + 1)
    return (np.arange(n_heads)[:, None] * head_dim + old[None, :]).reshape(-1)


def _head_masks():
    lane = lax.broadcasted_iota(jnp.int32, (1, LANES), 1)
    first = lane < (LANES // 2)
    return first, jnp.logical_not(first)


def _na_kernel(q_ref, k0_ref, k1_ref, k2_ref, v0_ref, v1_ref, v2_ref, kc_ref, vc_ref, bias_ref, o_ref):
    q2 = q_ref[...]
    zero = jnp.zeros_like(q2)
    k_refs = (k0_ref, k1_ref, k2_ref)
    v_refs = (v0_ref, v1_ref, v2_ref)
    kb = k0_ref.shape[0]
    outs = []
    for t, sel in enumerate(_head_masks()):
        qm = jnp.where(sel, q2, zero)
        s = [_dot_nt(qm, kr[...]) + bias_ref[0, t, :, j * kb:(j + 1) * kb] for j, kr in enumerate(k_refs)]
        s.append(_dot_nt(qm, kc_ref[...]))
        m = functools.reduce(jnp.maximum, [jnp.max(x, axis=-1, keepdims=True) for x in s])
        p = [jnp.exp(x - m) for x in s]
        l = functools.reduce(jnp.add, [jnp.sum(x, axis=-1, keepdims=True) for x in p])
        acc = _dot(p[-1].astype(BF16), vc_ref[...])
        for pj, vr in zip(p[:-1], v_refs):
            acc = acc + _dot(pj.astype(BF16), vr[...])
        outs.append(acc / l)
    first, _ = _head_masks()
    o_ref[...] = jnp.where(first, outs[0], outs[1]).astype(o_ref.dtype)


def _na_bias_tables(rpb):
    qn, kn = NA_Q_ROWS * GRID_W, NA_K_BLOCKS * NA_Q_ROWS * GRID_W
    qr, qc = np.arange(qn) // GRID_W, np.arange(qn) % GRID_W
    kr, kc = np.arange(kn) // GRID_W, np.arange(kn) % GRID_W
    c0 = np.clip(qc - NA_WIN_COLS // 2, 0, GRID_W - NA_WIN_COLS)
    col_ok = (kc[None, :] >= c0[:, None]) & (kc[None, :] < c0[:, None] + NA_WIN_COLS)
    n_dr, n_dc = 2 * NA_WIN_ROWS - 1, 2 * NA_WIN_COLS - 1
    last_start = NA_K_BLOCKS * NA_Q_ROWS - NA_WIN_ROWS
    q_rows, k_rows = np.arange(NA_Q_ROWS), np.arange(NA_K_BLOCKS * NA_Q_ROWS)
    sel_dr, oks = [], []
    for delta, w0 in ((0, np.zeros_like(qr)), (NA_Q_ROWS, qr), (last_start + NA_Q_ROWS, np.full_like(qr, last_start))):
        row_ok = (kr[None, :] >= w0[:, None]) & (kr[None, :] < w0[:, None] + NA_WIN_ROWS)
        dr = np.clip(k_rows[None, :] - q_rows[:, None] - delta + (NA_WIN_ROWS - 1), 0, n_dr - 1)
        sel_dr.append(dr[:, :, None] == np.arange(n_dr))
        oks.append(row_ok & col_ok)
    cols = np.arange(GRID_W)
    dc = np.clip(cols[None, :] - cols[:, None] + (NA_WIN_COLS - 1), 0, n_dc - 1)
    sel_dr = jnp.asarray(np.stack(sel_dr), F32)
    sel_dc = jnp.asarray(dc[:, :, None] == np.arange(n_dc), F32)
    by_row = jnp.einsum("vabr,hrc->vhabc", sel_dr, rpb, precision=HIGHEST)
    tabs = jnp.einsum("vhabc,xyc->vhaxby", by_row, sel_dc, precision=HIGHEST)
    tabs = tabs.reshape(len(oks), rpb.shape[0], qn, kn)
    tabs = jnp.where(jnp.asarray(np.stack(oks))[:, None], tabs, NEG_BIG)
    return jnp.concatenate([tabs, jnp.full_like(tabs[:1], NEG_BIG)], axis=0).astype(F32)


def _na_attention(q, k, v, rpb, batch, seq, n_ctx):
    n, d = q.shape
    qb = NA_Q_ROWS * GRID_W
    nblk = seq // qb
    assert seq % qb == 0 and nblk >= NA_K_BLOCKS and n_ctx == qb and (batch * seq) % n_ctx == 0
    assert seq // GRID_W >= NA_WIN_ROWS and NA_K_BLOCKS * NA_Q_ROWS == NA_Q_ROWS + NA_WIN_ROWS
    bias = _na_bias_tables(rpb)
    ctx_blk0 = batch * seq // n_ctx
    n_pairs = d // LANES

    def kv_spec(j):
        return pl.BlockSpec((qb, LANES), lambda b, c, i: (b * nblk + jnp.clip(i - 1, 0, nblk - NA_K_BLOCKS) + j, c))

    ctx_spec = pl.BlockSpec((n_ctx, LANES), lambda b, c, i: (ctx_blk0 + b, c))
    q_spec = pl.BlockSpec((qb, LANES), lambda b, c, i: (jnp.where(i < nblk, b * nblk + i, ctx_blk0 + b), c))
    variant = lambda i: jnp.where(i == 0, 0, jnp.where(i < nblk - 1, 1, jnp.where(i == nblk - 1, 2, 3)))
    return pl.pallas_call(
        _na_kernel,
        grid=(batch, n_pairs, nblk + 1),
        in_specs=[q_spec] + [kv_spec(j) for j in range(NA_K_BLOCKS)] * 2
        + [ctx_spec, ctx_spec,
           pl.BlockSpec((1, 2, qb, NA_K_BLOCKS * qb), lambda b, c, i: (variant(i), c, 0, 0))],
        out_specs=q_spec,
        out_shape=jax.ShapeDtypeStruct((n, d), BF16),
        compiler_params=_params(("arbitrary", "arbitrary", "arbitrary")),
        name="na_attention",
    )(q, k, k, k, v, v, v, k, v, bias)


def _diff_update(qm, k, v_ref, m_prev, acc_prev):
    s = _dot_nt(qm, k)
    m_cur = jnp.max(s, axis=1, keepdims=True)
    m_new = jnp.broadcast_to(m_cur, (s.shape[0], LANES)) if m_prev is None else jnp.maximum(m_prev, m_cur)
    m_wide = jnp.concatenate([m_new, m_new], axis=1)
    acc = None
    for c in range(0, s.shape[1], 2 * LANES):
        p = jnp.exp2(s[:, c:c + 2 * LANES] - m_wide).astype(BF16)
        pv = _dot(p, v_ref[0, c:c + 2 * LANES, :])
        acc = pv if acc is None else acc + pv
    if m_prev is not None:
        alpha = jnp.exp2(m_prev - m_new)
        acc = jnp.concatenate([alpha, alpha], axis=1) * acc_prev + acc
    return m_new, acc


def _diff_kernel(q_ref, k_ref, v_ref, kc_ref, vc_ref, lq1_ref, lk1_ref, lq2_ref, lk2_ref, sub_ref,
                 o_ref, m_sc, acc_sc, *, lambda_init):
    j = pl.program_id(3)
    tq = q_ref.shape[1]
    units = [(t, r) for r in range(0, tq, DIFF_STRIP) for t in range(2)]
    masks = _head_masks()

    def q_strip(t, r):
        qs = q_ref[0, r:r + DIFF_STRIP, :]
        return jnp.where(masks[t], qs, jnp.zeros_like(qs))

    @pl.when(j == 0)
    def _():
        for t, r in units:
            m, acc = _diff_update(q_strip(t, r), kc_ref[0], vc_ref, None, None)
            m_sc[t, r:r + DIFF_STRIP, :] = m
            acc_sc[t, r:r + DIFF_STRIP, :] = acc

    for t, r in units:
        rows = slice(r, r + DIFF_STRIP)
        m, acc = _diff_update(q_strip(t, r), k_ref[0], v_ref, m_sc[t, rows, :], acc_sc[t, rows, :])
        m_sc[t, rows, :] = m
        acc_sc[t, rows, :] = acc

    @pl.when(j == pl.num_programs(3) - 1)
    def _():
        lam = (jnp.exp(jnp.sum(lq1_ref[...] * lk1_ref[...], axis=-1, keepdims=True))
               - jnp.exp(jnp.sum(lq2_ref[...] * lk2_ref[...], axis=-1, keepdims=True)) + lambda_init)
        a1, a2 = acc_sc[0], acc_sc[1]
        o = a1[:, :LANES] / a1[:, LANES:] - lam * (a2[:, :LANES] / a2[:, LANES:])
        o_ref[...] = (_rms(o, sub_ref[...], DIFF_EPS) * (1.0 - lambda_init)).astype(o_ref.dtype)


def _diff_attention(q, k, v, lq1, lk1, lq2, lk2, subln, lambda_init, batch, seq, n_ctx):
    n_heads, vw = v.shape[0], v.shape[2]
    d = n_heads * LANES
    tq, tk = min(DIFF_TQ, seq), min(DIFF_TK, seq)
    assert seq % tq == 0 and seq % tk == 0 and tq % DIFF_STRIP == 0 and (batch * seq) % n_ctx == 0
    assert subln.shape[-1] == LANES and vw == 2 * LANES
    nq, nk = seq // tq, seq // tk
    ctx_blk0 = batch * seq // n_ctx
    q_spec = pl.BlockSpec((1, tq, LANES), lambda b, h, i, j: (h, b * nq + i, 0))
    k_spec = pl.BlockSpec((1, tk, LANES), lambda b, h, i, j: (h, b * nk + j, 0))
    v_spec = pl.BlockSpec((1, tk, vw), lambda b, h, i, j: (h, b * nk + j, 0))
    kc_spec = pl.BlockSpec((1, n_ctx, LANES), lambda b, h, i, j: (h, ctx_blk0 + b, 0))
    vc_spec = pl.BlockSpec((1, n_ctx, vw), lambda b, h, i, j: (h, ctx_blk0 + b, 0))
    vec = lambda a: a.reshape(1, -1).astype(F32)
    vec_spec = lambda a: pl.BlockSpec((1, a.shape[-1]), lambda b, h, i, j: (0, 0))
    return pl.pallas_call(
        functools.partial(_diff_kernel, lambda_init=lambda_init),
        grid=(batch, n_heads, nq, nk),
        in_specs=[q_spec, k_spec, v_spec, kc_spec, vc_spec,
                  vec_spec(lq1), vec_spec(lk1), vec_spec(lq2), vec_spec(lk2), vec_spec(subln)],
        out_specs=pl.BlockSpec((tq, LANES), lambda b, h, i, j: (b * nq + i, h)),
        out_shape=jax.ShapeDtypeStruct((batch * seq, d), BF16),
        scratch_shapes=[pltpu.VMEM((2, tq, LANES), F32), pltpu.VMEM((2, tq, vw), F32)],
        compiler_params=_params(("arbitrary", "arbitrary", "arbitrary", "arbitrary")),
        name="diff_attention",
    )(q, k, v, k, v, vec(lq1), vec(lk1), vec(lq2), vec(lk2), vec(subln))


def _lane_pack(cols, shape, dtype):
    lane = lax.broadcasted_iota(jnp.int32, shape, 1)
    out = jnp.zeros(shape, dtype)
    for kk, col in enumerate(cols):
        out = jnp.where(lane == kk, col.astype(dtype), out)
    return out


def _proj_post_kernel(o_ref, wo_ref, x_ref, npost_ref, g_ref, npre_ref, sh_ref, sc_ref, wr_ref, br_ref,
                      xo_ref, h_ref, idx_ref, gate_ref, rank_ref, cnt_ref, cnt_sc):
    @pl.when(pl.program_id(0) == 0)
    def _():
        cnt_sc[...] = jnp.zeros_like(cnt_sc)

    a = _dot(o_ref[...], wo_ref[...])
    xn = x_ref[...] + g_ref[0] * _rms(a, npost_ref[...], NORM_EPS)
    xo_ref[...] = xn
    h = _rms(xn, npre_ref[...], NORM_EPS) * (1.0 + sc_ref[0]) + sh_ref[0]
    h_ref[...] = h
    logits = jnp.dot(h, wr_ref[...], precision=HIGHEST, preferred_element_type=F32) + br_ref[...]

    tm = logits.shape[0]
    lane = lax.broadcasted_iota(jnp.int32, (tm, LANES), 1).astype(F32)
    vals, idxs, hits = [], [], []
    work = logits
    for _ in range(TOP_K):
        m = jnp.max(work, axis=-1, keepdims=True)
        idx = jnp.min(jnp.where(work == m, lane, float(LANES)), axis=-1, keepdims=True)
        hit = lane == idx
        work = jnp.where(hit, -jnp.inf, work)
        vals.append(m)
        idxs.append(idx)
        hits.append(hit)
    ex = [jnp.exp(vv - vals[0]) for vv in vals]
    den = functools.reduce(jnp.add, ex)
    gates = [e / den for e in ex]

    member = functools.reduce(jnp.logical_or, hits)
    row = lax.broadcasted_iota(jnp.int32, (tm, tm), 0)
    colm = lax.broadcasted_iota(jnp.int32, (tm, tm), 1)
    earlier = (colm < row).astype(BF16)
    before = _dot(earlier, member.astype(BF16)) + cnt_sc[...]
    ranks = [jnp.sum(jnp.where(hit, before, 0.0), axis=-1, keepdims=True) for hit in hits]
    cnt_sc[...] = cnt_sc[...] + jnp.sum(member.astype(F32), axis=0, keepdims=True)

    idx_ref[...] = _lane_pack(idxs, (tm, LANES), jnp.int32)
    gate_ref[...] = _lane_pack(gates, (tm, LANES), F32)
    rank_ref[...] = _lane_pack(ranks, (tm, LANES), jnp.int32)
    cnt_ref[...] = jnp.broadcast_to(cnt_sc[...], cnt_ref.shape)


def _proj_post(o, w_o, xs, norm_post, norm_pre, mod3, w_router, b_router, n_rows, tiles_per_batch, batch):
    d = xs.shape[1]
    n_exp = w_router.shape[1]
    wr = jnp.zeros((d, LANES), F32).at[:, :n_exp].set(w_router)
    br = jnp.full((1, LANES), NEG_BIG, F32).at[0, :n_exp].set(b_router)
    row = lambda t: (t, 0)
    const = lambda t: (0, 0)
    mspec = lambda chunk: _mod_spec(d, chunk, tiles_per_batch, batch)
    wide = lambda dt: jax.ShapeDtypeStruct((n_rows, LANES), dt)
    return pl.pallas_call(
        _proj_post_kernel,
        grid=(n_rows // TM,),
        in_specs=[
            pl.BlockSpec((TM, d), row), pl.BlockSpec((d, d), const), pl.BlockSpec((TM, d), row),
            pl.BlockSpec((1, d), const), mspec(2), pl.BlockSpec((1, d), const), mspec(3), mspec(4),
            pl.BlockSpec((d, LANES), const), pl.BlockSpec((1, LANES), const),
        ],
        out_specs=[pl.BlockSpec((TM, d), row), pl.BlockSpec((TM, d), row),
                   pl.BlockSpec((TM, LANES), row), pl.BlockSpec((TM, LANES), row), pl.BlockSpec((TM, LANES), row),
                   pl.BlockSpec((SUBLANES, LANES), const)],
        out_shape=[jax.ShapeDtypeStruct((n_rows, d), F32), jax.ShapeDtypeStruct((n_rows, d), F32),
                   wide(jnp.int32), wide(F32), wide(jnp.int32), jax.ShapeDtypeStruct((SUBLANES, LANES), F32)],
        scratch_shapes=[pltpu.VMEM((1, LANES), F32)],
        compiler_params=_params(("arbitrary",)),
        name="proj_post_router",
    )(o, w_o, xs, norm_post.reshape(1, d), mod3, norm_pre.reshape(1, d), mod3, mod3, wr, br)


def _for_each_assignment(n_tokens, fn):
    def body(i, carry):
        for kk in range(TOP_K):
            fn(i, kk)
        return carry
    lax.fori_loop(0, n_tokens, body, 0)


def _dispatch_kernel(dest_ref, h_ref, buf_in_ref, buf_ref, sem):
    del buf_in_ref

    def row_copy(i, kk):
        row = dest_ref[i * TOP_K + kk]
        return pltpu.make_async_copy(h_ref.at[pl.ds(i, 1), :], buf_ref.at[pl.ds(row, 1), :], sem)

    tm = h_ref.shape[0]
    _for_each_assignment(tm, lambda i, kk: row_copy(i, kk).start())
    for _ in range(TOP_K):
        pltpu.make_async_copy(h_ref, buf_ref.at[pl.ds(0, tm), :], sem).wait()


def _dispatch(h, dest_flat, n_buf_rows):
    n, d = h.shape
    buf0 = jnp.zeros((n_buf_rows, d), h.dtype)
    return pl.pallas_call(
        _dispatch_kernel,
        grid=(n // TM,),
        in_specs=[pl.BlockSpec((TM * TOP_K,), lambda t: (t,), memory_space=pltpu.SMEM),
                  pl.BlockSpec((TM, d), lambda t: (t, 0)),
                  pl.BlockSpec(memory_space=pl.ANY)],
        out_specs=pl.BlockSpec(memory_space=pl.ANY),
        out_shape=jax.ShapeDtypeStruct((n_buf_rows, d), h.dtype),
        scratch_shapes=[pltpu.SemaphoreType.DMA],
        input_output_aliases={2: 0},
        compiler_params=_params(("arbitrary",), has_side_effects=True),
        name="moe_dispatch",
    )(dest_flat, h, buf0)


def _expert_kernel(be_ref, nu_ref, x_ref, wgu_ref, bgu_ref, wd_ref, bd_ref, o_ref, wgu_bf, wd_bf):
    b = pl.program_id(0)
    changed = jnp.logical_or(b == 0, be_ref[b] != be_ref[jnp.maximum(b - 1, 0)])

    @pl.when(changed)
    def _():
        wgu_bf[...] = wgu_ref[0].astype(BF16)
        wd_bf[...] = wd_ref[0].astype(BF16)

    @pl.when(b < nu_ref[0])
    def _():
        f = wd_bf.shape[0]
        gu = _dot(x_ref[...].astype(BF16), wgu_bf[...]) + bgu_ref[0]
        g = jnp.minimum(gu[:, :f], SWIGLU_LIMIT)
        u = jnp.clip(gu[:, f:], -SWIGLU_LIMIT, SWIGLU_LIMIT)
        act = (u + 1.0) * (g * jax.nn.sigmoid(SWIGLU_ALPHA * g))
        o_ref[...] = _dot(act.astype(BF16), wd_bf[...]) + bd_ref[0]

    @pl.when(b >= nu_ref[0])
    def _():
        o_ref[...] = jnp.zeros_like(o_ref)


def _experts(buf, block_expert, n_used, w_gate_up, b_gate_up, w_down, b_down):
    n_rows, d = buf.shape
    n_exp, _, f2 = w_gate_up.shape
    f = w_down.shape[1]
    n_blocks = n_rows // EXPERT_ROWS
    grid_spec = pltpu.PrefetchScalarGridSpec(
        num_scalar_prefetch=2,
        grid=(n_blocks,),
        in_specs=[
            pl.BlockSpec((EXPERT_ROWS, d), lambda b, be, nu: (b, 0)),
            pl.BlockSpec((1, d, f2), lambda b, be, nu: (be[b], 0, 0)),
            pl.BlockSpec((1, 1, f2), lambda b, be, nu: (be[b], 0, 0)),
            pl.BlockSpec((1, f, d), lambda b, be, nu: (be[b], 0, 0)),
            pl.BlockSpec((1, 1, d), lambda b, be, nu: (be[b], 0, 0)),
        ],
        out_specs=pl.BlockSpec((EXPERT_ROWS, d), lambda b, be, nu: (b, 0)),
        scratch_shapes=[pltpu.VMEM((d, f2), BF16), pltpu.VMEM((f, d), BF16)],
    )
    return pl.pallas_call(
        _expert_kernel,
        grid_spec=grid_spec,
        out_shape=jax.ShapeDtypeStruct((n_rows, d), F32),
        compiler_params=_params(("arbitrary",)),
        name="moe_experts",
    )(block_expert, n_used, buf, w_gate_up, b_gate_up.reshape(n_exp, 1, f2), w_down, b_down.reshape(n_exp, 1, d))


def _combine_kernel(dest_ref, gate_ref, x_ref, nw_ref, g_ref, y_ref, xo_ref, rows_sc, sem):
    def row_copy(i, kk):
        row = dest_ref[i * TOP_K + kk]
        return pltpu.make_async_copy(y_ref.at[pl.ds(row, 1), :], rows_sc.at[kk, pl.ds(i, 1), :], sem)

    tm = x_ref.shape[0]
    _for_each_assignment(tm, lambda i, kk: row_copy(i, kk).start())
    for kk in range(TOP_K):
        pltpu.make_async_copy(y_ref.at[pl.ds(0, tm), :], rows_sc.at[kk], sem).wait()
    gates = gate_ref[...]
    y = gates[:, 0:1] * rows_sc[0]
    for kk in range(1, TOP_K):
        y = y + gates[:, kk:kk + 1] * rows_sc[kk]
    xo_ref[...] = x_ref[...] + g_ref[0] * _rms(y, nw_ref[...], NORM_EPS)


def _combine(y_grouped, dest_flat, gates, xs, norm_w, mod3, n_rows, tiles_per_batch, batch):
    d = xs.shape[1]
    row = lambda t: (t, 0)
    return pl.pallas_call(
        _combine_kernel,
        grid=(n_rows // TM,),
        in_specs=[pl.BlockSpec((TM * TOP_K,), lambda t: (t,), memory_space=pltpu.SMEM),
                  pl.BlockSpec((TM, LANES), row), pl.BlockSpec((TM, d), row),
                  pl.BlockSpec((1, d), lambda t: (0, 0)), _mod_spec(d, 5, tiles_per_batch, batch),
                  pl.BlockSpec(memory_space=pl.ANY)],
        out_specs=pl.BlockSpec((TM, d), row),
        out_shape=jax.ShapeDtypeStruct((n_rows, d), F32),
        scratch_shapes=[pltpu.VMEM((TOP_K, TM, d), F32), pltpu.SemaphoreType.DMA],
        compiler_params=_params(("arbitrary",)),
        name="moe_combine",
    )(dest_flat, gates, xs, norm_w.reshape(1, d), mod3, y_grouped)


def _moe_layer(h, idx, gates, rank, cnt, xs, norm_post, mod3, w_gate_up, b_gate_up, w_down, b_down,
               n_rows, tiles_per_batch, batch):
    n_exp = w_gate_up.shape[0]
    blk = EXPERT_ROWS
    counts = cnt[0, :n_exp].astype(jnp.int32)
    padded = (counts + blk - 1) // blk * blk
    pad_ends = jnp.cumsum(padded)
    pad_starts = pad_ends - padded
    dest = (pad_starts[idx[:, :TOP_K]] + rank[:, :TOP_K]).reshape(-1)
    n_blocks = (n_rows * TOP_K + n_exp * (blk - 1) + blk - 1) // blk
    block_start = jnp.arange(n_blocks, dtype=jnp.int32) * blk
    block_expert = jnp.sum(block_start[:, None] >= pad_ends[None, :], axis=1)
    block_expert = jnp.minimum(block_expert, n_exp - 1).astype(jnp.int32)
    n_used = (pad_ends[-1:] // blk).astype(jnp.int32)
    buf = _dispatch(h, dest, n_blocks * blk)
    y_grouped = _experts(buf, block_expert, n_used, w_gate_up, b_gate_up, w_down, b_down)
    return _combine(y_grouped, dest, gates, xs, norm_post, mod3, n_rows, tiles_per_batch, batch)


def kernel(x, c, ctx, c_ctx, w_mod, b_mod, norm_mix_pre, norm_mix_post, norm_ffn_pre, norm_ffn_post,
           na_w_qkv, na_w_o, na_rpb,
           diff_w_qkv, diff_w_o, diff_lambda_q1, diff_lambda_k1, diff_lambda_q2, diff_lambda_k2, diff_subln,
           moe_w_router, moe_b_router, moe_w_gate_up, moe_b_gate_up, moe_w_down, moe_b_down):
    batch, seq, d = x.shape
    n_ctx = ctx.shape[1]
    depth = w_mod.shape[0]
    assert depth == 2 and seq % TM == 0 and n_ctx % TM == 0
    n_lat = batch * seq
    n_all = n_lat + batch * n_ctx
    tiles_per_batch = seq // TM
    xs = jnp.concatenate([x.reshape(n_lat, d), ctx.reshape(batch * n_ctx, d)], axis=0)
    mod = _adaln_mod(c, c_ctx, w_mod, b_mod).reshape(depth, SUBLANES, 1, 6 * d)

    na_dh = d // NA_HEADS
    q, k, v = _norm_qkv(xs, norm_mix_pre[0], mod[0], na_w_qkv[0].astype(BF16), tiles_per_batch, batch,
                        q_scale=na_dh ** -0.5)
    o = _na_attention(q, k, v, na_rpb[0], batch, seq, n_ctx)
    xs, h, idx, gates, rank, cnt = _proj_post(o, na_w_o[0].astype(BF16), xs, norm_mix_post[0], norm_ffn_pre[0],
                                              mod[0], moe_w_router[0], moe_b_router[0], n_all, tiles_per_batch, batch)
    xs = _moe_layer(h, idx, gates, rank, cnt, xs, norm_ffn_post[0], mod[0], moe_w_gate_up[0], moe_b_gate_up[0],
                    moe_w_down[0], moe_b_down[0], n_all, tiles_per_batch, batch)

    diff_dh = d // DIFF_HEADS // 2
    lambda_init = 0.8 - 0.6 * math.exp(-0.3 * 1)
    cols = _pair_split_columns(2 * DIFF_HEADS, diff_dh)
    w1 = diff_w_qkv[0]
    w1 = jnp.concatenate([w1[:, :d][:, cols], w1[:, d:2 * d][:, cols], w1[:, 2 * d:]], axis=1).astype(BF16)
    tables = _rope_tables(seq, batch, batch * n_ctx, diff_dh)
    q, k, v = _norm_qkv(xs, norm_mix_pre[1], mod[1], w1, tiles_per_batch, batch,
                        q_scale=diff_dh ** -0.5 * LOG2_E, rope_tables=tables)
    o = _diff_attention(q, k, v, diff_lambda_q1[0], diff_lambda_k1[0], diff_lambda_q2[0], diff_lambda_k2[0],
                        diff_subln[0], lambda_init, batch, seq, n_ctx)
    xs, h, idx, gates, rank, cnt = _proj_post(o, diff_w_o[0].astype(BF16), xs, norm_mix_post[1], norm_ffn_pre[1],
                                              mod[1], moe_w_router[1], moe_b_router[1], n_lat, tiles_per_batch, batch)
    xs = _moe_layer(h, idx, gates, rank, cnt, xs, norm_ffn_post[1], mod[1], moe_w_gate_up[1], moe_b_gate_up[1],
                    moe_w_down[1], moe_b_down[1], n_lat, tiles_per_batch, batch)
    return xs.reshape(batch, seq, d)
```

```python
import functools
import math

import jax
import jax.numpy as jnp
import numpy as np
from jax import lax
from jax.experimental import pallas as pl
from jax.experimental.pallas import tpu as pltpu

F32 = jnp.float32
BF16 = jnp.bfloat16
HIGHEST = lax.Precision.HIGHEST

GRID_W = 64
NA_HEADS = 16
NA_WIN_ROWS = 8
NA_WIN_COLS = 16
DIFF_HEADS = 8
DIFF_EPS = 1e-5
ROPE_BASE = 10000.0
N_EXPERTS = 32
TOP_K = 4
SWIGLU_LIMIT = 7.0
SWIGLU_ALPHA = 1.702
NORM_EPS = 1e-6
NEG_BIG = -1e30
LOG2_E = math.log2(math.e)

LANES = 128
SUBLANES = 8
VMEM_LIMIT = 56 * 1024 * 1024

TM = 256
NA_Q_ROWS = 4
NA_K_BLOCKS = 3
DIFF_TQ = 2048
DIFF_TK = 1024
DIFF_STRIP = 1024
EXPERT_ROWS = 256
TM_POST = 512
DMA_ISSUE_UNROLL = 8


def _params(semantics, **kw):
    return pltpu.CompilerParams(dimension_semantics=semantics, vmem_limit_bytes=VMEM_LIMIT, **kw)


def _rms(x, w, eps):
    return x * lax.rsqrt(jnp.mean(x * x, axis=-1, keepdims=True) + eps) * w


def _dot(a, b):
    return jnp.dot(a, b, preferred_element_type=F32)


def _dot_nt(a, b):
    return lax.dot_general(a, b, (((1,), (1,)), ((), ())), preferred_element_type=F32)


def _mod_kernel(c_ref, w_ref, b_ref, o_ref):
    cv = c_ref[...]
    act = cv * jax.nn.sigmoid(cv)
    o_ref[0] = jnp.dot(act, w_ref[0], precision=HIGHEST, preferred_element_type=F32) + b_ref[0]


def _adaln_mod(c, c_ctx, w_mod, b_mod):
    depth, d, d6 = w_mod.shape
    batch = c.shape[0]
    assert batch + 1 <= SUBLANES
    cc = jnp.zeros((SUBLANES, d), F32).at[:batch].set(c).at[batch].set(c_ctx)
    tn = d6 // 4
    return pl.pallas_call(
        _mod_kernel,
        grid=(depth, d6 // tn),
        in_specs=[
            pl.BlockSpec((SUBLANES, d), lambda l, j: (0, 0)),
            pl.BlockSpec((1, d, tn), lambda l, j: (l, 0, j)),
            pl.BlockSpec((1, 1, tn), lambda l, j: (l, 0, j)),
        ],
        out_specs=pl.BlockSpec((1, SUBLANES, tn), lambda l, j: (l, 0, j)),
        out_shape=jax.ShapeDtypeStruct((depth, SUBLANES, d6), F32),
        compiler_params=_params(("arbitrary", "arbitrary")),
        name="adaln_mod",
    )(cc, w_mod, b_mod.reshape(depth, 1, d6))


def _mod_spec(d, chunk, tiles_per_batch, batch):
    return pl.BlockSpec((1, 1, d), lambda t: (jnp.minimum(t // tiles_per_batch, batch), 0, chunk))


def _norm_qkv_kernel(x_ref, nw_ref, sh_ref, sc_ref, w_ref, *rest, rope, q_scale):
    if rope:
        cos_ref, s1_ref, s2_ref, q_ref, k_ref, v_ref = rest
    else:
        q_ref, k_ref, v_ref = rest
    d = x_ref.shape[1]
    h = (_rms(x_ref[...], nw_ref[...], NORM_EPS) * (1.0 + sc_ref[0]) + sh_ref[0]).astype(BF16)
    q = _dot(h, w_ref[:, :d])
    k = _dot(h, w_ref[:, d:2 * d])
    v = _dot(h, w_ref[:, 2 * d:])
    if rope:
        for hh in range(v_ref.shape[0]):
            v_ref[hh, :, :LANES] = v[:, hh * LANES:(hh + 1) * LANES].astype(v_ref.dtype)
            v_ref[hh, :, LANES:] = jnp.ones((v_ref.shape[1], LANES), v_ref.dtype)
        cos, s1, s2 = cos_ref[...], s1_ref[...], s2_ref[...]
        for j in range(d // LANES):
            sl = slice(j * LANES, (j + 1) * LANES)
            for src, dst, scale in ((q, q_ref, q_scale), (k, k_ref, 1.0)):
                xs = src[:, sl]
                rot = xs * cos + pltpu.roll(xs, LANES - 32, 1) * s1 + pltpu.roll(xs, 32, 1) * s2
                dst[j] = (rot * scale).astype(dst.dtype)
    else:
        q_ref[...] = (q * q_scale).astype(q_ref.dtype)
        k_ref[...] = k.astype(k_ref.dtype)
        v_ref[...] = v.astype(v_ref.dtype)


def _norm_qkv(xs, norm_w, mod3, w_qkv, tiles_per_batch, batch, q_scale, rope_tables=None):
    n, d = xs.shape
    rope = rope_tables is not None
    row = lambda t: (t, 0)
    const = lambda t: (0, 0)
    in_specs = [
        pl.BlockSpec((TM, d), row),
        pl.BlockSpec((1, d), const),
        _mod_spec(d, 0, tiles_per_batch, batch),
        _mod_spec(d, 1, tiles_per_batch, batch),
        pl.BlockSpec((d, 3 * d), const),
    ]
    args = [xs, norm_w.reshape(1, d), mod3, mod3, w_qkv]
    out = jax.ShapeDtypeStruct((n, d), BF16)
    out_specs = [pl.BlockSpec((TM, d), row)] * 3
    out_shape = [out, out, out]
    if rope:
        in_specs += [pl.BlockSpec((TM, LANES), row)] * 3
        args += list(rope_tables)
        heads = d // LANES
        out_specs = [pl.BlockSpec((heads, TM, w), lambda t: (0, t, 0)) for w in (LANES, LANES, 2 * LANES)]
        out_shape = [jax.ShapeDtypeStruct((heads, n, w), BF16) for w in (LANES, LANES, 2 * LANES)]
    return pl.pallas_call(
        functools.partial(_norm_qkv_kernel, rope=rope, q_scale=q_scale),
        grid=(n // TM,),
        in_specs=in_specs,
        out_specs=out_specs,
        out_shape=out_shape,
        compiler_params=_params(("arbitrary",)),
        name="norm_qkv_rope" if rope else "norm_qkv",
    )(*args)


def _rope_tables(seq, batch, n_ctx_rows, head_dim):
    t = jnp.arange(seq, dtype=jnp.int32)
    row = (t // GRID_W).astype(F32)
    col = (t % GRID_W).astype(F32)
    axis_dim = head_dim // 2
    inv_freq = ROPE_BASE ** (-jnp.arange(0, axis_dim, 2, dtype=F32) / axis_dim)
    ang = jnp.concatenate([row[:, None] * inv_freq, col[:, None] * inv_freq], axis=-1)
    cos, sin = jnp.cos(ang), jnp.sin(ang)
    zero = jnp.zeros_like(sin)
    reps = LANES // head_dim

    def lay(first, second, ctx_value):
        tab = jnp.tile(jnp.concatenate([first, second], axis=-1), (batch, reps))
        return jnp.concatenate([tab, jnp.full((n_ctx_rows, LANES), ctx_value, F32)], axis=0)

    return lay(cos, cos, 1.0), lay(-sin, zero, 0.0), lay(zero, sin, 0.0)


def _pair_split_columns(n_heads, head_dim):
    p = np.arange(head_dim)
    old = np.where(p < head_dim // 2, 2 * p, 2 * (p - head_dim // 2) + 1)
    return (np.arange(n_heads)[:, None] * head_dim + old[None, :]).reshape(-1)


def _head_masks():
    lane = lax.broadcasted_iota(jnp.int32, (1, LANES), 1)
    first = lane < (LANES // 2)
    return first, jnp.logical_not(first)


def _na_kernel(q_ref, k0_ref, k1_ref, k2_ref, v0_ref, v1_ref, v2_ref, kc_ref, vc_ref, bias_ref, o_ref):
    q2 = q_ref[...]
    zero = jnp.zeros_like(q2)
    k_refs = (k0_ref, k1_ref, k2_ref)
    v_refs = (v0_ref, v1_ref, v2_ref)
    kb = k0_ref.shape[0]
    outs = []
    for t, sel in enumerate(_head_masks()):
        qm = jnp.where(sel, q2, zero)
        s = [_dot_nt(qm, kr[...]) + bias_ref[0, t, :, j * kb:(j + 1) * kb] for j, kr in enumerate(k_refs)]
        s.append(_dot_nt(qm, kc_ref[...]))
        m = functools.reduce(jnp.maximum, [jnp.max(x, axis=-1, keepdims=True) for x in s])
        p = [jnp.exp(x - m) for x in s]
        l = functools.reduce(jnp.add, [jnp.sum(x, axis=-1, keepdims=True) for x in p])
        acc = _dot(p[-1].astype(BF16), vc_ref[...])
        for pj, vr in zip(p[:-1], v_refs):
            acc = acc + _dot(pj.astype(BF16), vr[...])
        outs.append(acc / l)
    first, _ = _head_masks()
    o_ref[...] = jnp.where(first, outs[0], outs[1]).astype(o_ref.dtype)


def _na_bias_tables(rpb):
    qn, kn = NA_Q_ROWS * GRID_W, NA_K_BLOCKS * NA_Q_ROWS * GRID_W
    qr, qc = np.arange(qn) // GRID_W, np.arange(qn) % GRID_W
    kr, kc = np.arange(kn) // GRID_W, np.arange(kn) % GRID_W
    c0 = np.clip(qc - NA_WIN_COLS // 2, 0, GRID_W - NA_WIN_COLS)
    col_ok = (kc[None, :] >= c0[:, None]) & (kc[None, :] < c0[:, None] + NA_WIN_COLS)
    n_dr, n_dc = 2 * NA_WIN_ROWS - 1, 2 * NA_WIN_COLS - 1
    last_start = NA_K_BLOCKS * NA_Q_ROWS - NA_WIN_ROWS
    q_rows, k_rows = np.arange(NA_Q_ROWS), np.arange(NA_K_BLOCKS * NA_Q_ROWS)
    sel_dr, oks = [], []
    for delta, w0 in ((0, np.zeros_like(qr)), (NA_Q_ROWS, qr), (last_start + NA_Q_ROWS, np.full_like(qr, last_start))):
        row_ok = (kr[None, :] >= w0[:, None]) & (kr[None, :] < w0[:, None] + NA_WIN_ROWS)
        dr = np.clip(k_rows[None, :] - q_rows[:, None] - delta + (NA_WIN_ROWS - 1), 0, n_dr - 1)
        sel_dr.append(dr[:, :, None] == np.arange(n_dr))
        oks.append(row_ok & col_ok)
    cols = np.arange(GRID_W)
    dc = np.clip(cols[None, :] - cols[:, None] + (NA_WIN_COLS - 1), 0, n_dc - 1)
    sel_dr = jnp.asarray(np.stack(sel_dr), F32)
    sel_dc = jnp.asarray(dc[:, :, None] == np.arange(n_dc), F32)
    by_row = jnp.einsum("vabr,hrc->vhabc", sel_dr, rpb, precision=HIGHEST)
    tabs = jnp.einsum("vhabc,xyc->vhaxby", by_row, sel_dc, precision=HIGHEST)
    tabs = tabs.reshape(len(oks), rpb.shape[0], qn, kn)
    tabs = jnp.where(jnp.asarray(np.stack(oks))[:, None], tabs, NEG_BIG)
    return jnp.concatenate([tabs, jnp.full_like(tabs[:1], NEG_BIG)], axis=0).astype(F32)


def _na_attention(q, k, v, rpb, batch, seq, n_ctx):
    n, d = q.shape
    qb = NA_Q_ROWS * GRID_W
    nblk = seq // qb
    assert seq % qb == 0 and nblk >= NA_K_BLOCKS and n_ctx == qb and (batch * seq) % n_ctx == 0
    assert seq // GRID_W >= NA_WIN_ROWS and NA_K_BLOCKS * NA_Q_ROWS == NA_Q_ROWS + NA_WIN_ROWS
    bias = _na_bias_tables(rpb)
    ctx_blk0 = batch * seq // n_ctx
    n_pairs = d // LANES

    def kv_spec(j):
        return pl.BlockSpec((qb, LANES), lambda b, c, i: (b * nblk + jnp.clip(i - 1, 0, nblk - NA_K_BLOCKS) + j, c))

    ctx_spec = pl.BlockSpec((n_ctx, LANES), lambda b, c, i: (ctx_blk0 + b, c))
    q_spec = pl.BlockSpec((qb, LANES), lambda b, c, i: (jnp.where(i < nblk, b * nblk + i, ctx_blk0 + b), c))
    variant = lambda i: jnp.where(i == 0, 0, jnp.where(i < nblk - 1, 1, jnp.where(i == nblk - 1, 2, 3)))
    return pl.pallas_call(
        _na_kernel,
        grid=(batch, n_pairs, nblk + 1),
        in_specs=[q_spec] + [kv_spec(j) for j in range(NA_K_BLOCKS)] * 2
        + [ctx_spec, ctx_spec,
           pl.BlockSpec((1, 2, qb, NA_K_BLOCKS * qb), lambda b, c, i: (variant(i), c, 0, 0))],
        out_specs=q_spec,
        out_shape=jax.ShapeDtypeStruct((n, d), BF16),
        compiler_params=_params(("arbitrary", "arbitrary", "arbitrary")),
        name="na_attention",
    )(q, k, k, k, v, v, v, k, v, bias)


def _diff_update(qm, k, v_ref, m_prev, acc_prev):
    s = _dot_nt(qm, k)
    m_cur = jnp.max(s, axis=1, keepdims=True)
    m_new = jnp.broadcast_to(m_cur, (s.shape[0], LANES)) if m_prev is None else jnp.maximum(m_prev, m_cur)
    m_wide = jnp.concatenate([m_new, m_new], axis=1)
    acc = None
    for c in range(0, s.shape[1], 2 * LANES):
        p = jnp.exp2(s[:, c:c + 2 * LANES] - m_wide).astype(BF16)
        pv = _dot(p, v_ref[0, c:c + 2 * LANES, :])
        acc = pv if acc is None else acc + pv
    if m_prev is not None:
        alpha = jnp.exp2(m_prev - m_new)
        acc = jnp.concatenate([alpha, alpha], axis=1) * acc_prev + acc
    return m_new, acc


def _diff_kernel(q_ref, k_ref, v_ref, kc_ref, vc_ref, lq1_ref, lk1_ref, lq2_ref, lk2_ref, sub_ref,
                 o_ref, m_sc, acc_sc, *, lambda_init):
    j = pl.program_id(3)
    tq = q_ref.shape[1]
    units = [(t, r) for r in range(0, tq, DIFF_STRIP) for t in range(2)]
    masks = _head_masks()

    def q_strip(t, r):
        qs = q_ref[0, r:r + DIFF_STRIP, :]
        return jnp.where(masks[t], qs, jnp.zeros_like(qs))

    @pl.when(j == 0)
    def _():
        for t, r in units:
            m, acc = _diff_update(q_strip(t, r), kc_ref[0], vc_ref, None, None)
            m_sc[t, r:r + DIFF_STRIP, :] = m
            acc_sc[t, r:r + DIFF_STRIP, :] = acc

    for t, r in units:
        rows = slice(r, r + DIFF_STRIP)
        m, acc = _diff_update(q_strip(t, r), k_ref[0], v_ref, m_sc[t, rows, :], acc_sc[t, rows, :])
        m_sc[t, rows, :] = m
        acc_sc[t, rows, :] = acc

    @pl.when(j == pl.num_programs(3) - 1)
    def _():
        lam = (jnp.exp(jnp.sum(lq1_ref[...] * lk1_ref[...], axis=-1, keepdims=True))
               - jnp.exp(jnp.sum(lq2_ref[...] * lk2_ref[...], axis=-1, keepdims=True)) + lambda_init)
        a1, a2 = acc_sc[0], acc_sc[1]
        o = a1[:, :LANES] / a1[:, LANES:] - lam * (a2[:, :LANES] / a2[:, LANES:])
        o_ref[...] = (_rms(o, sub_ref[...], DIFF_EPS) * (1.0 - lambda_init)).astype(o_ref.dtype)


def _diff_attention(q, k, v, lq1, lk1, lq2, lk2, subln, lambda_init, batch, seq, n_ctx):
    n_heads, vw = v.shape[0], v.shape[2]
    d = n_heads * LANES
    tq, tk = min(DIFF_TQ, seq), min(DIFF_TK, seq)
    assert seq % tq == 0 and seq % tk == 0 and tq % DIFF_STRIP == 0 and (batch * seq) % n_ctx == 0
    assert subln.shape[-1] == LANES and vw == 2 * LANES
    nq, nk = seq // tq, seq // tk
    ctx_blk0 = batch * seq // n_ctx
    q_spec = pl.BlockSpec((1, tq, LANES), lambda b, h, i, j: (h, b * nq + i, 0))
    k_spec = pl.BlockSpec((1, tk, LANES), lambda b, h, i, j: (h, b * nk + j, 0))
    v_spec = pl.BlockSpec((1, tk, vw), lambda b, h, i, j: (h, b * nk + j, 0))
    kc_spec = pl.BlockSpec((1, n_ctx, LANES), lambda b, h, i, j: (h, ctx_blk0 + b, 0))
    vc_spec = pl.BlockSpec((1, n_ctx, vw), lambda b, h, i, j: (h, ctx_blk0 + b, 0))
    vec = lambda a: a.reshape(1, -1).astype(F32)
    vec_spec = lambda a: pl.BlockSpec((1, a.shape[-1]), lambda b, h, i, j: (0, 0))
    return pl.pallas_call(
        functools.partial(_diff_kernel, lambda_init=lambda_init),
        grid=(batch, n_heads, nq, nk),
        in_specs=[q_spec, k_spec, v_spec, kc_spec, vc_spec,
                  vec_spec(lq1), vec_spec(lk1), vec_spec(lq2), vec_spec(lk2), vec_spec(subln)],
        out_specs=pl.BlockSpec((tq, LANES), lambda b, h, i, j: (b * nq + i, h)),
        out_shape=jax.ShapeDtypeStruct((batch * seq, d), BF16),
        scratch_shapes=[pltpu.VMEM((2, tq, LANES), F32), pltpu.VMEM((2, tq, vw), F32)],
        compiler_params=_params(("arbitrary", "arbitrary", "arbitrary", "arbitrary")),
        name="diff_attention",
    )(q, k, v, k, v, vec(lq1), vec(lk1), vec(lq2), vec(lk2), vec(subln))


def _lane_pack(cols, shape, dtype):
    lane = lax.broadcasted_iota(jnp.int32, shape, 1)
    out = jnp.zeros(shape, dtype)
    for kk, col in enumerate(cols):
        out = jnp.where(lane == kk, col.astype(dtype), out)
    return out


def _proj_post_kernel(o_ref, wo_ref, x_ref, npost_ref, g_ref, npre_ref, sh_ref, sc_ref, wr_ref, br_ref,
                      xo_ref, h_ref, idx_ref, gate_ref, rank_ref, cnt_ref, cnt_sc):
    @pl.when(pl.program_id(0) == 0)
    def _():
        cnt_sc[...] = jnp.zeros_like(cnt_sc)

    a = _dot(o_ref[...], wo_ref[...])
    xn = x_ref[...] + g_ref[0] * _rms(a, npost_ref[...], NORM_EPS)
    xo_ref[...] = xn
    h = _rms(xn, npre_ref[...], NORM_EPS) * (1.0 + sc_ref[0]) + sh_ref[0]
    h_ref[...] = h
    logits = jnp.dot(h, wr_ref[...], precision=HIGHEST, preferred_element_type=F32) + br_ref[...]

    tm = logits.shape[0]
    lane = lax.broadcasted_iota(jnp.int32, (tm, LANES), 1).astype(F32)
    vals, idxs, hits = [], [], []
    work = logits
    for _ in range(TOP_K):
        m = jnp.max(work, axis=-1, keepdims=True)
        idx = jnp.min(jnp.where(work == m, lane, float(LANES)), axis=-1, keepdims=True)
        hit = lane == idx
        work = jnp.where(hit, -jnp.inf, work)
        vals.append(m)
        idxs.append(idx)
        hits.append(hit)
    ex = [jnp.exp(vv - vals[0]) for vv in vals]
    den = functools.reduce(jnp.add, ex)
    gates = [e / den for e in ex]

    member = functools.reduce(jnp.logical_or, hits)
    row = lax.broadcasted_iota(jnp.int32, (tm, tm), 0)
    colm = lax.broadcasted_iota(jnp.int32, (tm, tm), 1)
    earlier = (colm < row).astype(BF16)
    before = _dot(earlier, member.astype(BF16)) + cnt_sc[...]
    ranks = [jnp.sum(jnp.where(hit, before, 0.0), axis=-1, keepdims=True) for hit in hits]
    cnt_sc[...] = cnt_sc[...] + jnp.sum(member.astype(F32), axis=0, keepdims=True)

    idx_ref[...] = _lane_pack(idxs, (tm, LANES), jnp.int32)
    gate_ref[...] = _lane_pack(gates, (tm, LANES), F32)
    rank_ref[...] = _lane_pack(ranks, (tm, LANES), jnp.int32)
    cnt_ref[...] = jnp.broadcast_to(cnt_sc[...], cnt_ref.shape)


def _proj_post(o, w_o, xs, norm_post, norm_pre, mod3, w_router, b_router, n_rows, tiles_per_batch, batch):
    d = xs.shape[1]
    n_exp = w_router.shape[1]
    wr = jnp.zeros((d, LANES), F32).at[:, :n_exp].set(w_router)
    br = jnp.full((1, LANES), NEG_BIG, F32).at[0, :n_exp].set(b_router)
    row = lambda t: (t, 0)
    const = lambda t: (0, 0)
    tm = TM_POST
    assert n_rows % tm == 0 and (tiles_per_batch * TM) % tm == 0
    mspec = lambda chunk: _mod_spec(d, chunk, tiles_per_batch * TM // tm, batch)
    wide = lambda dt: jax.ShapeDtypeStruct((n_rows, LANES), dt)
    return pl.pallas_call(
        _proj_post_kernel,
        grid=(n_rows // tm,),
        in_specs=[
            pl.BlockSpec((tm, d), row), pl.BlockSpec((d, d), const), pl.BlockSpec((tm, d), row),
            pl.BlockSpec((1, d), const), mspec(2), pl.BlockSpec((1, d), const), mspec(3), mspec(4),
            pl.BlockSpec((d, LANES), const), pl.BlockSpec((1, LANES), const),
        ],
        out_specs=[pl.BlockSpec((tm, d), row), pl.BlockSpec((tm, d), row),
                   pl.BlockSpec((tm, LANES), row), pl.BlockSpec((tm, LANES), row), pl.BlockSpec((tm, LANES), row),
                   pl.BlockSpec((SUBLANES, LANES), const)],
        out_shape=[jax.ShapeDtypeStruct((n_rows, d), F32), jax.ShapeDtypeStruct((n_rows, d), F32),
                   wide(jnp.int32), wide(F32), wide(jnp.int32), jax.ShapeDtypeStruct((SUBLANES, LANES), F32)],
        scratch_shapes=[pltpu.VMEM((1, LANES), F32)],
        compiler_params=_params(("arbitrary",)),
        name="proj_post_router",
    )(o, w_o, xs, norm_post.reshape(1, d), mod3, norm_pre.reshape(1, d), mod3, mod3, wr, br)


def _for_each_assignment(n_tokens, fn):
    def body(i, carry):
        for kk in range(TOP_K):
            fn(i, kk)
        return carry
    lax.fori_loop(0, n_tokens, body, 0, unroll=DMA_ISSUE_UNROLL)


def _dispatch_kernel(ends_ref, nu_ref, dest_ref, h_ref, buf_ref, zero_sc, sem, zero_sem):
    tm = h_ref.shape[0]
    blk = zero_sc.shape[0]

    @pl.when(pl.program_id(0) == 0)
    def _():
        zero_sc[...] = jnp.zeros_like(zero_sc)
        zero_copy = lambda row: pltpu.make_async_copy(zero_sc, buf_ref.at[pl.ds(row, blk), :], zero_sem)
        n_exp = ends_ref.shape[0]
        non_empty = [ends_ref[e] > (ends_ref[e - 1] if e else 0) for e in range(n_exp)]
        for action in ("start", "wait"):
            for e in range(n_exp):
                @pl.when(non_empty[e])
                def _():
                    getattr(zero_copy(pl.multiple_of(ends_ref[e] - blk, blk)), action)()

            def tail(b, carry):
                getattr(zero_copy(pl.multiple_of(b * blk, blk)), action)()
                return carry
            lax.fori_loop(nu_ref[0], buf_ref.shape[0] // blk, tail, 0)

    def row_copy(i, kk):
        row = dest_ref[i * TOP_K + kk]
        return pltpu.make_async_copy(h_ref.at[pl.ds(i, 1), :], buf_ref.at[pl.ds(row, 1), :], sem)

    _for_each_assignment(tm, lambda i, kk: row_copy(i, kk).start())
    for _ in range(TOP_K):
        pltpu.make_async_copy(h_ref, buf_ref.at[pl.ds(0, tm), :], sem).wait()


def _dispatch(h, dest_flat, pad_ends, n_used, n_buf_rows):
    n, d = h.shape
    grid_spec = pltpu.PrefetchScalarGridSpec(
        num_scalar_prefetch=2,
        grid=(n // TM,),
        in_specs=[pl.BlockSpec((TM * TOP_K,), lambda t, ends, nu: (t,), memory_space=pltpu.SMEM),
                  pl.BlockSpec((TM, d), lambda t, ends, nu: (t, 0))],
        out_specs=pl.BlockSpec(memory_space=pl.ANY),
        scratch_shapes=[pltpu.VMEM((EXPERT_ROWS, d), h.dtype), pltpu.SemaphoreType.DMA, pltpu.SemaphoreType.DMA],
    )
    return pl.pallas_call(
        _dispatch_kernel,
        grid_spec=grid_spec,
        out_shape=jax.ShapeDtypeStruct((n_buf_rows, d), h.dtype),
        compiler_params=_params(("arbitrary",), has_side_effects=True, disable_bounds_checks=True),
        name="moe_dispatch",
    )(pad_ends, n_used, dest_flat, h)


def _expert_kernel(be_ref, nu_ref, x_ref, wgu_ref, bgu_ref, wd_ref, bd_ref, o_ref, wgu_bf, wd_bf):
    b = pl.program_id(0)
    changed = jnp.logical_or(b == 0, be_ref[b] != be_ref[jnp.maximum(b - 1, 0)])

    @pl.when(changed)
    def _():
        wgu_bf[...] = wgu_ref[0].astype(BF16)
        wd_bf[...] = wd_ref[0].astype(BF16)

    @pl.when(b < nu_ref[0])
    def _():
        f = wd_bf.shape[0]
        gu = _dot(x_ref[...].astype(BF16), wgu_bf[...]) + bgu_ref[0]
        g = jnp.minimum(gu[:, :f], SWIGLU_LIMIT)
        u = jnp.clip(gu[:, f:], -SWIGLU_LIMIT, SWIGLU_LIMIT)
        act = (u + 1.0) * (g * jax.nn.sigmoid(SWIGLU_ALPHA * g))
        o_ref[...] = _dot(act.astype(BF16), wd_bf[...]) + bd_ref[0]

    @pl.when(b >= nu_ref[0])
    def _():
        o_ref[...] = jnp.zeros_like(o_ref)


def _experts(buf, block_expert, n_used, layer, w_gate_up, b_gate_up, w_down, b_down):
    n_rows, d = buf.shape
    depth, n_exp, _, f2 = w_gate_up.shape
    f = w_down.shape[2]
    n_blocks = n_rows // EXPERT_ROWS
    block_expert = block_expert + layer * n_exp
    w_gate_up, w_down = w_gate_up.reshape(depth * n_exp, d, f2), w_down.reshape(depth * n_exp, f, d)
    n_exp = depth * n_exp
    grid_spec = pltpu.PrefetchScalarGridSpec(
        num_scalar_prefetch=2,
        grid=(n_blocks,),
        in_specs=[
            pl.BlockSpec((EXPERT_ROWS, d), lambda b, be, nu: (b, 0)),
            pl.BlockSpec((1, d, f2), lambda b, be, nu: (be[b], 0, 0)),
            pl.BlockSpec((1, 1, f2), lambda b, be, nu: (be[b], 0, 0)),
            pl.BlockSpec((1, f, d), lambda b, be, nu: (be[b], 0, 0)),
            pl.BlockSpec((1, 1, d), lambda b, be, nu: (be[b], 0, 0)),
        ],
        out_specs=pl.BlockSpec((EXPERT_ROWS, d), lambda b, be, nu: (b, 0)),
        scratch_shapes=[pltpu.VMEM((d, f2), BF16), pltpu.VMEM((f, d), BF16)],
    )
    return pl.pallas_call(
        _expert_kernel,
        grid_spec=grid_spec,
        out_shape=jax.ShapeDtypeStruct((n_rows, d), F32),
        compiler_params=_params(("arbitrary",)),
        name="moe_experts",
    )(block_expert, n_used, buf, w_gate_up, b_gate_up.reshape(n_exp, 1, f2), w_down, b_down.reshape(n_exp, 1, d))


def _combine_kernel(dest_ref, gate_ref, x_ref, nw_ref, g_ref, y_ref, xo_ref, rows_sc, sem):
    def row_copy(i, kk):
        row = dest_ref[i * TOP_K + kk]
        return pltpu.make_async_copy(y_ref.at[pl.ds(row, 1), :], rows_sc.at[kk, pl.ds(i, 1), :], sem)

    tm = x_ref.shape[0]
    _for_each_assignment(tm, lambda i, kk: row_copy(i, kk).start())
    for kk in range(TOP_K):
        pltpu.make_async_copy(y_ref.at[pl.ds(0, tm), :], rows_sc.at[kk], sem).wait()
    gates = gate_ref[...]
    y = gates[:, 0:1] * rows_sc[0]
    for kk in range(1, TOP_K):
        y = y + gates[:, kk:kk + 1] * rows_sc[kk]
    xo_ref[...] = x_ref[...] + g_ref[0] * _rms(y, nw_ref[...], NORM_EPS)


def _combine(y_grouped, dest_flat, gates, xs, norm_w, mod3, n_rows, tiles_per_batch, batch):
    d = xs.shape[1]
    row = lambda t: (t, 0)
    return pl.pallas_call(
        _combine_kernel,
        grid=(n_rows // TM,),
        in_specs=[pl.BlockSpec((TM * TOP_K,), lambda t: (t,), memory_space=pltpu.SMEM),
                  pl.BlockSpec((TM, LANES), row), pl.BlockSpec((TM, d), row),
                  pl.BlockSpec((1, d), lambda t: (0, 0)), _mod_spec(d, 5, tiles_per_batch, batch),
                  pl.BlockSpec(memory_space=pl.ANY)],
        out_specs=pl.BlockSpec((TM, d), row),
        out_shape=jax.ShapeDtypeStruct((n_rows, d), F32),
        scratch_shapes=[pltpu.VMEM((TOP_K, TM, d), F32), pltpu.SemaphoreType.DMA],
        compiler_params=_params(("arbitrary",), disable_bounds_checks=True),
        name="moe_combine",
    )(dest_flat, gates, xs, norm_w.reshape(1, d), mod3, y_grouped)


def _moe_layer(h, idx, gates, rank, cnt, xs, norm_post, mod3, layer, w_gate_up, b_gate_up, w_down, b_down,
               n_rows, tiles_per_batch, batch):
    n_exp = w_gate_up.shape[1]
    blk = EXPERT_ROWS
    counts = cnt[0, :n_exp].astype(jnp.int32)
    padded = (counts + blk - 1) // blk * blk
    pad_ends = jnp.cumsum(padded)
    pad_starts = pad_ends - padded
    dest = (pad_starts[idx[:, :TOP_K]] + rank[:, :TOP_K]).reshape(-1)
    n_blocks = (n_rows * TOP_K + n_exp * (blk - 1) + blk - 1) // blk
    block_start = jnp.arange(n_blocks, dtype=jnp.int32) * blk
    block_expert = jnp.sum(block_start[:, None] >= pad_ends[None, :], axis=1)
    block_expert = jnp.minimum(block_expert, n_exp - 1).astype(jnp.int32)
    n_used = (pad_ends[-1:] // blk).astype(jnp.int32)
    buf = _dispatch(h, dest, pad_ends.astype(jnp.int32), n_used, n_blocks * blk)
    y_grouped = _experts(buf, block_expert, n_used, layer, w_gate_up, b_gate_up, w_down, b_down)
    return _combine(y_grouped, dest, gates, xs, norm_post, mod3, n_rows, tiles_per_batch, batch)


def kernel(x, c, ctx, c_ctx, w_mod, b_mod, norm_mix_pre, norm_mix_post, norm_ffn_pre, norm_ffn_post,
           na_w_qkv, na_w_o, na_rpb,
           diff_w_qkv, diff_w_o, diff_lambda_q1, diff_lambda_k1, diff_lambda_q2, diff_lambda_k2, diff_subln,
           moe_w_router, moe_b_router, moe_w_gate_up, moe_b_gate_up, moe_w_down, moe_b_down):
    batch, seq, d = x.shape
    n_ctx = ctx.shape[1]
    depth = w_mod.shape[0]
    assert depth == 2 and seq % TM == 0 and n_ctx % TM == 0
    n_lat = batch * seq
    n_all = n_lat + batch * n_ctx
    tiles_per_batch = seq // TM
    xs = jnp.concatenate([x.reshape(n_lat, d), ctx.reshape(batch * n_ctx, d)], axis=0)
    mod = _adaln_mod(c, c_ctx, w_mod, b_mod).reshape(depth, SUBLANES, 1, 6 * d)

    na_dh = d // NA_HEADS
    q, k, v = _norm_qkv(xs, norm_mix_pre[0], mod[0], na_w_qkv[0].astype(BF16), tiles_per_batch, batch,
                        q_scale=na_dh ** -0.5)
    o = _na_attention(q, k, v, na_rpb[0], batch, seq, n_ctx)
    xs, h, idx, gates, rank, cnt = _proj_post(o, na_w_o[0].astype(BF16), xs, norm_mix_post[0], norm_ffn_pre[0],
                                              mod[0], moe_w_router[0], moe_b_router[0], n_all, tiles_per_batch, batch)
    xs = _moe_layer(h, idx, gates, rank, cnt, xs, norm_ffn_post[0], mod[0], 0, moe_w_gate_up, moe_b_gate_up,
                    moe_w_down, moe_b_down, n_all, tiles_per_batch, batch)

    diff_dh = d // DIFF_HEADS // 2
    lambda_init = 0.8 - 0.6 * math.exp(-0.3 * 1)
    cols = _pair_split_columns(2 * DIFF_HEADS, diff_dh)
    w1 = diff_w_qkv[0]
    w1 = jnp.concatenate([w1[:, :d][:, cols], w1[:, d:2 * d][:, cols], w1[:, 2 * d:]], axis=1).astype(BF16)
    tables = _rope_tables(seq, batch, batch * n_ctx, diff_dh)
    q, k, v = _norm_qkv(xs, norm_mix_pre[1], mod[1], w1, tiles_per_batch, batch,
                        q_scale=diff_dh ** -0.5 * LOG2_E, rope_tables=tables)
    o = _diff_attention(q, k, v, diff_lambda_q1[0], diff_lambda_k1[0], diff_lambda_q2[0], diff_lambda_k2[0],
                        diff_subln[0], lambda_init, batch, seq, n_ctx)
    xs, h, idx, gates, rank, cnt = _proj_post(o, diff_w_o[0].astype(BF16), xs, norm_mix_post[1], norm_ffn_pre[1],
                                              mod[1], moe_w_router[1], moe_b_router[1], n_lat, tiles_per_batch, batch)
    xs = _moe_layer(h, idx, gates, rank, cnt, xs, norm_ffn_post[1], mod[1], 1, moe_w_gate_up, moe_b_gate_up,
                    moe_w_down, moe_b_down, n_lat, tiles_per_batch, batch)
    return xs.reshape(batch, seq, d)
```

```python
import functools
import math

import jax
import jax.numpy as jnp
import numpy as np
from jax import lax
from jax.experimental import pallas as pl
from jax.experimental.pallas import tpu as pltpu

F32 = jnp.float32
BF16 = jnp.bfloat16
HIGHEST = lax.Precision.HIGHEST

GRID_W = 64
NA_HEADS = 16
NA_WIN_ROWS = 8
NA_WIN_COLS = 16
DIFF_HEADS = 8
DIFF_EPS = 1e-5
ROPE_BASE = 10000.0
N_EXPERTS = 32
TOP_K = 4
SWIGLU_LIMIT = 7.0
SWIGLU_ALPHA = 1.702
NORM_EPS = 1e-6
NEG_BIG = -1e30
LOG2_E = math.log2(math.e)

LANES = 128
SUBLANES = 8
VMEM_LIMIT = 56 * 1024 * 1024

TM = 256
NA_Q_ROWS = 4
NA_K_BLOCKS = 3
DIFF_TQ = 2048
DIFF_TK = 2048
DIFF_STRIP = 1024
EXPERT_ROWS = 256
TM_POST = 512
DMA_ISSUE_UNROLL = 8


def _params(semantics, **kw):
    return pltpu.CompilerParams(dimension_semantics=semantics, vmem_limit_bytes=VMEM_LIMIT, **kw)


def _rms(x, w, eps):
    return x * lax.rsqrt(jnp.mean(x * x, axis=-1, keepdims=True) + eps) * w


def _dot(a, b):
    return jnp.dot(a, b, preferred_element_type=F32)


def _dot_nt(a, b):
    return lax.dot_general(a, b, (((1,), (1,)), ((), ())), preferred_element_type=F32)


def _mod_kernel(c_ref, w_ref, b_ref, o_ref):
    cv = c_ref[...]
    act = cv * jax.nn.sigmoid(cv)
    o_ref[0] = jnp.dot(act, w_ref[0], precision=HIGHEST, preferred_element_type=F32) + b_ref[0]


def _adaln_mod(c, c_ctx, w_mod, b_mod):
    depth, d, d6 = w_mod.shape
    batch = c.shape[0]
    assert batch + 1 <= SUBLANES
    cc = jnp.zeros((SUBLANES, d), F32).at[:batch].set(c).at[batch].set(c_ctx)
    tn = d6 // 4
    return pl.pallas_call(
        _mod_kernel,
        grid=(depth, d6 // tn),
        in_specs=[
            pl.BlockSpec((SUBLANES, d), lambda l, j: (0, 0)),
            pl.BlockSpec((1, d, tn), lambda l, j: (l, 0, j)),
            pl.BlockSpec((1, 1, tn), lambda l, j: (l, 0, j)),
        ],
        out_specs=pl.BlockSpec((1, SUBLANES, tn), lambda l, j: (l, 0, j)),
        out_shape=jax.ShapeDtypeStruct((depth, SUBLANES, d6), F32),
        compiler_params=_params(("arbitrary", "arbitrary")),
        name="adaln_mod",
    )(cc, w_mod, b_mod.reshape(depth, 1, d6))


def _mod_spec(d, chunk, tiles_per_batch, batch):
    return pl.BlockSpec((1, 1, d), lambda t: (jnp.minimum(t // tiles_per_batch, batch), 0, chunk))


def _norm_qkv_kernel(x_ref, nw_ref, sh_ref, sc_ref, w_ref, *rest, rope, q_scale):
    if rope:
        cos_ref, s1_ref, s2_ref, q_ref, k_ref, v_ref = rest
    else:
        q_ref, k_ref, v_ref = rest
    d = x_ref.shape[1]
    h = (_rms(x_ref[...], nw_ref[...], NORM_EPS) * (1.0 + sc_ref[0]) + sh_ref[0]).astype(BF16)
    q = _dot(h, w_ref[:, :d])
    k = _dot(h, w_ref[:, d:2 * d])
    v = _dot(h, w_ref[:, 2 * d:])
    if rope:
        for hh in range(v_ref.shape[0]):
            v_ref[hh, :, :LANES] = v[:, hh * LANES:(hh + 1) * LANES].astype(v_ref.dtype)
            v_ref[hh, :, LANES:] = jnp.ones((v_ref.shape[1], LANES), v_ref.dtype)
        cos, s1, s2 = cos_ref[...], s1_ref[...], s2_ref[...]
        for j in range(d // LANES):
            sl = slice(j * LANES, (j + 1) * LANES)
            for src, dst, scale in ((q, q_ref, q_scale), (k, k_ref, 1.0)):
                xs = src[:, sl]
                rot = xs * cos + pltpu.roll(xs, LANES - 32, 1) * s1 + pltpu.roll(xs, 32, 1) * s2
                dst[j] = (rot * scale).astype(dst.dtype)
    else:
        q_ref[...] = (q * q_scale).astype(q_ref.dtype)
        k_ref[...] = k.astype(k_ref.dtype)
        v_ref[...] = v.astype(v_ref.dtype)


def _norm_qkv(xs, norm_w, mod3, w_qkv, tiles_per_batch, batch, q_scale, rope_tables=None):
    n, d = xs.shape
    rope = rope_tables is not None
    row = lambda t: (t, 0)
    const = lambda t: (0, 0)
    in_specs = [
        pl.BlockSpec((TM, d), row),
        pl.BlockSpec((1, d), const),
        _mod_spec(d, 0, tiles_per_batch, batch),
        _mod_spec(d, 1, tiles_per_batch, batch),
        pl.BlockSpec((d, 3 * d), const),
    ]
    args = [xs, norm_w.reshape(1, d), mod3, mod3, w_qkv]
    out = jax.ShapeDtypeStruct((n, d), BF16)
    out_specs = [pl.BlockSpec((TM, d), row)] * 3
    out_shape = [out, out, out]
    if rope:
        in_specs += [pl.BlockSpec((TM, LANES), row)] * 3
        args += list(rope_tables)
        heads = d // LANES
        out_specs = [pl.BlockSpec((heads, TM, w), lambda t: (0, t, 0)) for w in (LANES, LANES, 2 * LANES)]
        out_shape = [jax.ShapeDtypeStruct((heads, n, w), BF16) for w in (LANES, LANES, 2 * LANES)]
    return pl.pallas_call(
        functools.partial(_norm_qkv_kernel, rope=rope, q_scale=q_scale),
        grid=(n // TM,),
        in_specs=in_specs,
        out_specs=out_specs,
        out_shape=out_shape,
        compiler_params=_params(("arbitrary",)),
        name="norm_qkv_rope" if rope else "norm_qkv",
    )(*args)


def _rope_tables(seq, batch, n_ctx_rows, head_dim):
    t = jnp.arange(seq, dtype=jnp.int32)
    row = (t // GRID_W).astype(F32)
    col = (t % GRID_W).astype(F32)
    axis_dim = head_dim // 2
    inv_freq = ROPE_BASE ** (-jnp.arange(0, axis_dim, 2, dtype=F32) / axis_dim)
    ang = jnp.concatenate([row[:, None] * inv_freq, col[:, None] * inv_freq], axis=-1)
    cos, sin = jnp.cos(ang), jnp.sin(ang)
    zero = jnp.zeros_like(sin)
    reps = LANES // head_dim

    def lay(first, second, ctx_value):
        tab = jnp.tile(jnp.concatenate([first, second], axis=-1), (batch, reps))
        return jnp.concatenate([tab, jnp.full((n_ctx_rows, LANES), ctx_value, F32)], axis=0)

    return lay(cos, cos, 1.0), lay(-sin, zero, 0.0), lay(zero, sin, 0.0)


def _pair_split_columns(n_heads, head_dim):
    p = np.arange(head_dim)
    old = np.where(p < head_dim // 2, 2 * p, 2 * (p - head_dim // 2) + 1)
    return (np.arange(n_heads)[:, None] * head_dim + old[None, :]).reshape(-1)


def _head_masks():
    lane = lax.broadcasted_iota(jnp.int32, (1, LANES), 1)
    first = lane < (LANES // 2)
    return first, jnp.logical_not(first)


def _na_kernel(q_ref, k0_ref, k1_ref, k2_ref, v0_ref, v1_ref, v2_ref, kc_ref, vc_ref, bias_ref, o_ref):
    q2 = q_ref[...]
    zero = jnp.zeros_like(q2)
    k_refs = (k0_ref, k1_ref, k2_ref)
    v_refs = (v0_ref, v1_ref, v2_ref)
    kb = k0_ref.shape[0]
    outs = []
    for t, sel in enumerate(_head_masks()):
        qm = jnp.where(sel, q2, zero)
        s = [_dot_nt(qm, kr[...]) + bias_ref[0, t, :, j * kb:(j + 1) * kb] for j, kr in enumerate(k_refs)]
        s.append(_dot_nt(qm, kc_ref[...]))
        m = functools.reduce(jnp.maximum, [jnp.max(x, axis=-1, keepdims=True) for x in s])
        p = [jnp.exp(x - m) for x in s]
        l = functools.reduce(jnp.add, [jnp.sum(x, axis=-1, keepdims=True) for x in p])
        acc = _dot(p[-1].astype(BF16), vc_ref[...])
        for pj, vr in zip(p[:-1], v_refs):
            acc = acc + _dot(pj.astype(BF16), vr[...])
        outs.append(acc / l)
    first, _ = _head_masks()
    o_ref[...] = jnp.where(first, outs[0], outs[1]).astype(o_ref.dtype)


def _na_bias_tables(rpb):
    qn, kn = NA_Q_ROWS * GRID_W, NA_K_BLOCKS * NA_Q_ROWS * GRID_W
    qr, qc = np.arange(qn) // GRID_W, np.arange(qn) % GRID_W
    kr, kc = np.arange(kn) // GRID_W, np.arange(kn) % GRID_W
    c0 = np.clip(qc - NA_WIN_COLS // 2, 0, GRID_W - NA_WIN_COLS)
    col_ok = (kc[None, :] >= c0[:, None]) & (kc[None, :] < c0[:, None] + NA_WIN_COLS)
    n_dr, n_dc = 2 * NA_WIN_ROWS - 1, 2 * NA_WIN_COLS - 1
    last_start = NA_K_BLOCKS * NA_Q_ROWS - NA_WIN_ROWS
    q_rows, k_rows = np.arange(NA_Q_ROWS), np.arange(NA_K_BLOCKS * NA_Q_ROWS)
    sel_dr, oks = [], []
    for delta, w0 in ((0, np.zeros_like(qr)), (NA_Q_ROWS, qr), (last_start + NA_Q_ROWS, np.full_like(qr, last_start))):
        row_ok = (kr[None, :] >= w0[:, None]) & (kr[None, :] < w0[:, None] + NA_WIN_ROWS)
        dr = np.clip(k_rows[None, :] - q_rows[:, None] - delta + (NA_WIN_ROWS - 1), 0, n_dr - 1)
        sel_dr.append(dr[:, :, None] == np.arange(n_dr))
        oks.append(row_ok & col_ok)
    cols = np.arange(GRID_W)
    dc = np.clip(cols[None, :] - cols[:, None] + (NA_WIN_COLS - 1), 0, n_dc - 1)
    sel_dr = jnp.asarray(np.stack(sel_dr), F32)
    sel_dc = jnp.asarray(dc[:, :, None] == np.arange(n_dc), F32)
    by_row = jnp.einsum("vabr,hrc->vhabc", sel_dr, rpb, precision=HIGHEST)
    tabs = jnp.einsum("vhabc,xyc->vhaxby", by_row, sel_dc, precision=HIGHEST)
    tabs = tabs.reshape(len(oks), rpb.shape[0], qn, kn)
    tabs = jnp.where(jnp.asarray(np.stack(oks))[:, None], tabs, NEG_BIG)
    return jnp.concatenate([tabs, jnp.full_like(tabs[:1], NEG_BIG)], axis=0).astype(F32)


def _na_attention(q, k, v, rpb, batch, seq, n_ctx):
    n, d = q.shape
    qb = NA_Q_ROWS * GRID_W
    nblk = seq // qb
    assert seq % qb == 0 and nblk >= NA_K_BLOCKS and n_ctx == qb and (batch * seq) % n_ctx == 0
    assert seq // GRID_W >= NA_WIN_ROWS and NA_K_BLOCKS * NA_Q_ROWS == NA_Q_ROWS + NA_WIN_ROWS
    bias = _na_bias_tables(rpb)
    ctx_blk0 = batch * seq // n_ctx
    n_pairs = d // LANES

    def kv_spec(j):
        return pl.BlockSpec((qb, LANES), lambda b, c, i: (b * nblk + jnp.clip(i - 1, 0, nblk - NA_K_BLOCKS) + j, c))

    ctx_spec = pl.BlockSpec((n_ctx, LANES), lambda b, c, i: (ctx_blk0 + b, c))
    q_spec = pl.BlockSpec((qb, LANES), lambda b, c, i: (jnp.where(i < nblk, b * nblk + i, ctx_blk0 + b), c))
    variant = lambda i: jnp.where(i == 0, 0, jnp.where(i < nblk - 1, 1, jnp.where(i == nblk - 1, 2, 3)))
    return pl.pallas_call(
        _na_kernel,
        grid=(batch, n_pairs, nblk + 1),
        in_specs=[q_spec] + [kv_spec(j) for j in range(NA_K_BLOCKS)] * 2
        + [ctx_spec, ctx_spec,
           pl.BlockSpec((1, 2, qb, NA_K_BLOCKS * qb), lambda b, c, i: (variant(i), c, 0, 0))],
        out_specs=q_spec,
        out_shape=jax.ShapeDtypeStruct((n, d), BF16),
        compiler_params=_params(("arbitrary", "arbitrary", "arbitrary")),
        name="na_attention",
    )(q, k, k, k, v, v, v, k, v, bias)


def _diff_scores(qm, k):
    s = _dot_nt(qm, k)
    return s, jnp.max(s, axis=1, keepdims=True)


def _diff_accumulate(s, m_cur, v_ref, m_prev, acc_prev):
    m_new = jnp.broadcast_to(m_cur, (s.shape[0], LANES)) if m_prev is None else jnp.maximum(m_prev, m_cur)
    m_wide = jnp.concatenate([m_new, m_new], axis=1)
    acc = None
    for c in range(0, s.shape[1], 2 * LANES):
        p = jnp.exp2(s[:, c:c + 2 * LANES] - m_wide).astype(BF16)
        pv = _dot(p, v_ref[0, c:c + 2 * LANES, :])
        acc = pv if acc is None else acc + pv
    if m_prev is not None:
        alpha = jnp.exp2(m_prev - m_new)
        acc = jnp.concatenate([alpha, alpha], axis=1) * acc_prev + acc
    return m_new, acc


def _diff_kernel(q_ref, k_ref, v_ref, kc_ref, vc_ref, lq1_ref, lk1_ref, lq2_ref, lk2_ref, sub_ref,
                 o_ref, m_sc, acc_sc, *, lambda_init):
    j = pl.program_id(3)
    tq = q_ref.shape[1]
    units = [(t, r) for r in range(0, tq, DIFF_STRIP) for t in range(2)]
    masks = _head_masks()

    def q_strip(t, r):
        qs = q_ref[0, r:r + DIFF_STRIP, :]
        return jnp.where(masks[t], qs, jnp.zeros_like(qs))

    @pl.when(j == 0)
    def _():
        for t, r in units:
            m, acc = _diff_accumulate(*_diff_scores(q_strip(t, r), kc_ref[0]), vc_ref, None, None)
            m_sc[t, r:r + DIFF_STRIP, :] = m
            acc_sc[t, r:r + DIFF_STRIP, :] = acc

    nxt = _diff_scores(q_strip(*units[0]), k_ref[0])
    for u, (t, r) in enumerate(units):
        cur = nxt
        if u + 1 < len(units):
            nxt = _diff_scores(q_strip(*units[u + 1]), k_ref[0])
        rows = slice(r, r + DIFF_STRIP)
        m, acc = _diff_accumulate(*cur, v_ref, m_sc[t, rows, :], acc_sc[t, rows, :])
        m_sc[t, rows, :] = m
        acc_sc[t, rows, :] = acc

    @pl.when(j == pl.num_programs(3) - 1)
    def _():
        lam = (jnp.exp(jnp.sum(lq1_ref[...] * lk1_ref[...], axis=-1, keepdims=True))
               - jnp.exp(jnp.sum(lq2_ref[...] * lk2_ref[...], axis=-1, keepdims=True)) + lambda_init)
        a1, a2 = acc_sc[0], acc_sc[1]
        o = a1[:, :LANES] / a1[:, LANES:] - lam * (a2[:, :LANES] / a2[:, LANES:])
        o_ref[...] = (_rms(o, sub_ref[...], DIFF_EPS) * (1.0 - lambda_init)).astype(o_ref.dtype)


def _diff_attention(q, k, v, lq1, lk1, lq2, lk2, subln, lambda_init, batch, seq, n_ctx):
    n_heads, vw = v.shape[0], v.shape[2]
    d = n_heads * LANES
    tq, tk = min(DIFF_TQ, seq), min(DIFF_TK, seq)
    assert seq % tq == 0 and seq % tk == 0 and tq % DIFF_STRIP == 0 and (batch * seq) % n_ctx == 0
    assert subln.shape[-1] == LANES and vw == 2 * LANES
    nq, nk = seq // tq, seq // tk
    ctx_blk0 = batch * seq // n_ctx
    q_spec = pl.BlockSpec((1, tq, LANES), lambda b, h, i, j: (h, b * nq + i, 0))
    k_spec = pl.BlockSpec((1, tk, LANES), lambda b, h, i, j: (h, b * nk + j, 0))
    v_spec = pl.BlockSpec((1, tk, vw), lambda b, h, i, j: (h, b * nk + j, 0))
    kc_spec = pl.BlockSpec((1, n_ctx, LANES), lambda b, h, i, j: (h, ctx_blk0 + b, 0))
    vc_spec = pl.BlockSpec((1, n_ctx, vw), lambda b, h, i, j: (h, ctx_blk0 + b, 0))
    vec = lambda a: a.reshape(1, -1).astype(F32)
    vec_spec = lambda a: pl.BlockSpec((1, a.shape[-1]), lambda b, h, i, j: (0, 0))
    return pl.pallas_call(
        functools.partial(_diff_kernel, lambda_init=lambda_init),
        grid=(batch, n_heads, nq, nk),
        in_specs=[q_spec, k_spec, v_spec, kc_spec, vc_spec,
                  vec_spec(lq1), vec_spec(lk1), vec_spec(lq2), vec_spec(lk2), vec_spec(subln)],
        out_specs=pl.BlockSpec((tq, LANES), lambda b, h, i, j: (b * nq + i, h)),
        out_shape=jax.ShapeDtypeStruct((batch * seq, d), BF16),
        scratch_shapes=[pltpu.VMEM((2, tq, LANES), F32), pltpu.VMEM((2, tq, vw), F32)],
        compiler_params=_params(("arbitrary", "arbitrary", "arbitrary", "arbitrary")),
        name="diff_attention",
    )(q, k, v, k, v, vec(lq1), vec(lk1), vec(lq2), vec(lk2), vec(subln))


def _lane_pack(cols, shape, dtype):
    lane = lax.broadcasted_iota(jnp.int32, shape, 1)
    out = jnp.zeros(shape, dtype)
    for kk, col in enumerate(cols):
        out = jnp.where(lane == kk, col.astype(dtype), out)
    return out


def _proj_post_kernel(o_ref, wo_ref, x_ref, npost_ref, g_ref, npre_ref, sh_ref, sc_ref, wr_ref, br_ref,
                      xo_ref, h_ref, idx_ref, gate_ref, rank_ref, cnt_ref, cnt_sc):
    @pl.when(pl.program_id(0) == 0)
    def _():
        cnt_sc[...] = jnp.zeros_like(cnt_sc)

    a = _dot(o_ref[...], wo_ref[...])
    xn = x_ref[...] + g_ref[0] * _rms(a, npost_ref[...], NORM_EPS)
    xo_ref[...] = xn
    h = _rms(xn, npre_ref[...], NORM_EPS) * (1.0 + sc_ref[0]) + sh_ref[0]
    h_ref[...] = h
    logits = jnp.dot(h, wr_ref[...], precision=HIGHEST, preferred_element_type=F32) + br_ref[...]

    tm = logits.shape[0]
    lane = lax.broadcasted_iota(jnp.int32, (tm, LANES), 1).astype(F32)
    vals, idxs, hits = [], [], []
    work = logits
    for _ in range(TOP_K):
        m = jnp.max(work, axis=-1, keepdims=True)
        idx = jnp.min(jnp.where(work == m, lane, float(LANES)), axis=-1, keepdims=True)
        hit = lane == idx
        work = jnp.where(hit, -jnp.inf, work)
        vals.append(m)
        idxs.append(idx)
        hits.append(hit)
    ex = [jnp.exp(vv - vals[0]) for vv in vals]
    den = functools.reduce(jnp.add, ex)
    gates = [e / den for e in ex]

    member = functools.reduce(jnp.logical_or, hits)
    row = lax.broadcasted_iota(jnp.int32, (tm, tm), 0)
    colm = lax.broadcasted_iota(jnp.int32, (tm, tm), 1)
    earlier = (colm < row).astype(BF16)
    before = _dot(earlier, member.astype(BF16)) + cnt_sc[...]
    ranks = [jnp.sum(jnp.where(hit, before, 0.0), axis=-1, keepdims=True) for hit in hits]
    cnt_sc[...] = cnt_sc[...] + jnp.sum(member.astype(F32), axis=0, keepdims=True)

    idx_ref[...] = _lane_pack(idxs, (tm, LANES), jnp.int32)
    gate_ref[...] = _lane_pack(gates, (tm, LANES), F32)
    rank_ref[...] = _lane_pack(ranks, (tm, LANES), jnp.int32)
    cnt_ref[...] = jnp.broadcast_to(cnt_sc[...], cnt_ref.shape)


def _proj_post(o, w_o, xs, norm_post, norm_pre, mod3, w_router, b_router, n_rows, tiles_per_batch, batch):
    d = xs.shape[1]
    n_exp = w_router.shape[1]
    wr = jnp.zeros((d, LANES), F32).at[:, :n_exp].set(w_router)
    br = jnp.full((1, LANES), NEG_BIG, F32).at[0, :n_exp].set(b_router)
    row = lambda t: (t, 0)
    const = lambda t: (0, 0)
    tm = TM_POST
    assert n_rows % tm == 0 and (tiles_per_batch * TM) % tm == 0
    mspec = lambda chunk: _mod_spec(d, chunk, tiles_per_batch * TM // tm, batch)
    wide = lambda dt: jax.ShapeDtypeStruct((n_rows, LANES), dt)
    return pl.pallas_call(
        _proj_post_kernel,
        grid=(n_rows // tm,),
        in_specs=[
            pl.BlockSpec((tm, d), row), pl.BlockSpec((d, d), const), pl.BlockSpec((tm, d), row),
            pl.BlockSpec((1, d), const), mspec(2), pl.BlockSpec((1, d), const), mspec(3), mspec(4),
            pl.BlockSpec((d, LANES), const), pl.BlockSpec((1, LANES), const),
        ],
        out_specs=[pl.BlockSpec((tm, d), row), pl.BlockSpec((tm, d), row),
                   pl.BlockSpec((tm, LANES), row), pl.BlockSpec((tm, LANES), row), pl.BlockSpec((tm, LANES), row),
                   pl.BlockSpec((SUBLANES, LANES), const)],
        out_shape=[jax.ShapeDtypeStruct((n_rows, d), F32), jax.ShapeDtypeStruct((n_rows, d), F32),
                   wide(jnp.int32), wide(F32), wide(jnp.int32), jax.ShapeDtypeStruct((SUBLANES, LANES), F32)],
        scratch_shapes=[pltpu.VMEM((1, LANES), F32)],
        compiler_params=_params(("arbitrary",)),
        name="proj_post_router",
    )(o, w_o, xs, norm_post.reshape(1, d), mod3, norm_pre.reshape(1, d), mod3, mod3, wr, br)


def _for_each_assignment(n_tokens, fn):
    def body(i, carry):
        for kk in range(TOP_K):
            fn(i, kk)
        return carry
    lax.fori_loop(0, n_tokens, body, 0, unroll=DMA_ISSUE_UNROLL)


def _dispatch_kernel(ends_ref, nu_ref, dest_ref, h_ref, buf_ref, zero_sc, sem, zero_sem):
    tm = h_ref.shape[0]
    blk = zero_sc.shape[0]

    @pl.when(pl.program_id(0) == 0)
    def _():
        zero_sc[...] = jnp.zeros_like(zero_sc)
        zero_copy = lambda row: pltpu.make_async_copy(zero_sc, buf_ref.at[pl.ds(row, blk), :], zero_sem)
        n_exp = ends_ref.shape[0]
        non_empty = [ends_ref[e] > (ends_ref[e - 1] if e else 0) for e in range(n_exp)]
        for action in ("start", "wait"):
            for e in range(n_exp):
                @pl.when(non_empty[e])
                def _():
                    getattr(zero_copy(pl.multiple_of(ends_ref[e] - blk, blk)), action)()

            def tail(b, carry):
                getattr(zero_copy(pl.multiple_of(b * blk, blk)), action)()
                return carry
            lax.fori_loop(nu_ref[0], buf_ref.shape[0] // blk, tail, 0)

    def row_copy(i, kk):
        row = dest_ref[i * TOP_K + kk]
        return pltpu.make_async_copy(h_ref.at[pl.ds(i, 1), :], buf_ref.at[pl.ds(row, 1), :], sem)

    _for_each_assignment(tm, lambda i, kk: row_copy(i, kk).start())
    for _ in range(TOP_K):
        pltpu.make_async_copy(h_ref, buf_ref.at[pl.ds(0, tm), :], sem).wait()


def _dispatch(h, dest_flat, pad_ends, n_used, n_buf_rows):
    n, d = h.shape
    grid_spec = pltpu.PrefetchScalarGridSpec(
        num_scalar_prefetch=2,
        grid=(n // TM,),
        in_specs=[pl.BlockSpec((TM * TOP_K,), lambda t, ends, nu: (t,), memory_space=pltpu.SMEM),
                  pl.BlockSpec((TM, d), lambda t, ends, nu: (t, 0))],
        out_specs=pl.BlockSpec(memory_space=pl.ANY),
        scratch_shapes=[pltpu.VMEM((EXPERT_ROWS, d), h.dtype), pltpu.SemaphoreType.DMA, pltpu.SemaphoreType.DMA],
    )
    return pl.pallas_call(
        _dispatch_kernel,
        grid_spec=grid_spec,
        out_shape=jax.ShapeDtypeStruct((n_buf_rows, d), h.dtype),
        compiler_params=_params(("arbitrary",), has_side_effects=True, disable_bounds_checks=True),
        name="moe_dispatch",
    )(pad_ends, n_used, dest_flat, h)


def _expert_kernel(be_ref, nu_ref, x_ref, wgu_ref, bgu_ref, wd_ref, bd_ref, o_ref, wgu_bf, wd_bf):
    b = pl.program_id(0)
    changed = jnp.logical_or(b == 0, be_ref[b] != be_ref[jnp.maximum(b - 1, 0)])

    @pl.when(changed)
    def _():
        wgu_bf[...] = wgu_ref[0].astype(BF16)
        wd_bf[...] = wd_ref[0].astype(BF16)

    @pl.when(b < nu_ref[0])
    def _():
        f = wd_bf.shape[0]
        gu = _dot(x_ref[...].astype(BF16), wgu_bf[...]) + bgu_ref[0]
        g = jnp.minimum(gu[:, :f], SWIGLU_LIMIT)
        u = jnp.clip(gu[:, f:], -SWIGLU_LIMIT, SWIGLU_LIMIT)
        act = (u + 1.0) * (g * jax.nn.sigmoid(SWIGLU_ALPHA * g))
        o_ref[...] = _dot(act.astype(BF16), wd_bf[...]) + bd_ref[0]

    @pl.when(b >= nu_ref[0])
    def _():
        o_ref[...] = jnp.zeros_like(o_ref)


def _experts(buf, block_expert, n_used, layer, w_gate_up, b_gate_up, w_down, b_down):
    n_rows, d = buf.shape
    depth, n_exp, _, f2 = w_gate_up.shape
    f = w_down.shape[2]
    n_blocks = n_rows // EXPERT_ROWS
    block_expert = block_expert + layer * n_exp
    w_gate_up, w_down = w_gate_up.reshape(depth * n_exp, d, f2), w_down.reshape(depth * n_exp, f, d)
    n_exp = depth * n_exp
    grid_spec = pltpu.PrefetchScalarGridSpec(
        num_scalar_prefetch=2,
        grid=(n_blocks,),
        in_specs=[
            pl.BlockSpec((EXPERT_ROWS, d), lambda b, be, nu: (b, 0)),
            pl.BlockSpec((1, d, f2), lambda b, be, nu: (be[b], 0, 0)),
            pl.BlockSpec((1, 1, f2), lambda b, be, nu: (be[b], 0, 0)),
            pl.BlockSpec((1, f, d), lambda b, be, nu: (be[b], 0, 0)),
            pl.BlockSpec((1, 1, d), lambda b, be, nu: (be[b], 0, 0)),
        ],
        out_specs=pl.BlockSpec((EXPERT_ROWS, d), lambda b, be, nu: (b, 0)),
        scratch_shapes=[pltpu.VMEM((d, f2), BF16), pltpu.VMEM((f, d), BF16)],
    )
    return pl.pallas_call(
        _expert_kernel,
        grid_spec=grid_spec,
        out_shape=jax.ShapeDtypeStruct((n_rows, d), F32),
        compiler_params=_params(("arbitrary",)),
        name="moe_experts",
    )(block_expert, n_used, buf, w_gate_up, b_gate_up.reshape(n_exp, 1, f2), w_down, b_down.reshape(n_exp, 1, d))


def _combine_kernel(dest_ref, gate_ref, x_ref, nw_ref, g_ref, y_ref, xo_ref, rows_sc, sem):
    def row_copy(i, kk):
        row = dest_ref[i * TOP_K + kk]
        return pltpu.make_async_copy(y_ref.at[pl.ds(row, 1), :], rows_sc.at[kk, pl.ds(i, 1), :], sem)

    tm = x_ref.shape[0]
    _for_each_assignment(tm, lambda i, kk: row_copy(i, kk).start())
    for kk in range(TOP_K):
        pltpu.make_async_copy(y_ref.at[pl.ds(0, tm), :], rows_sc.at[kk], sem).wait()
    gates = gate_ref[...]
    y = gates[:, 0:1] * rows_sc[0]
    for kk in range(1, TOP_K):
        y = y + gates[:, kk:kk + 1] * rows_sc[kk]
    xo_ref[...] = x_ref[...] + g_ref[0] * _rms(y, nw_ref[...], NORM_EPS)


def _combine(y_grouped, dest_flat, gates, xs, norm_w, mod3, n_rows, tiles_per_batch, batch):
    d = xs.shape[1]
    row = lambda t: (t, 0)
    return pl.pallas_call(
        _combine_kernel,
        grid=(n_rows // TM,),
        in_specs=[pl.BlockSpec((TM * TOP_K,), lambda t: (t,), memory_space=pltpu.SMEM),
                  pl.BlockSpec((TM, LANES), row), pl.BlockSpec((TM, d), row),
                  pl.BlockSpec((1, d), lambda t: (0, 0)), _mod_spec(d, 5, tiles_per_batch, batch),
                  pl.BlockSpec(memory_space=pl.ANY)],
        out_specs=pl.BlockSpec((TM, d), row),
        out_shape=jax.ShapeDtypeStruct((n_rows, d), F32),
        scratch_shapes=[pltpu.VMEM((TOP_K, TM, d), F32), pltpu.SemaphoreType.DMA],
        compiler_params=_params(("arbitrary",), disable_bounds_checks=True),
        name="moe_combine",
    )(dest_flat, gates, xs, norm_w.reshape(1, d), mod3, y_grouped)


def _moe_layer(h, idx, gates, rank, cnt, xs, norm_post, mod3, layer, w_gate_up, b_gate_up, w_down, b_down,
               n_rows, tiles_per_batch, batch):
    n_exp = w_gate_up.shape[1]
    blk = EXPERT_ROWS
    counts = cnt[0, :n_exp].astype(jnp.int32)
    padded = (counts + blk - 1) // blk * blk
    pad_ends = jnp.cumsum(padded)
    pad_starts = pad_ends - padded
    dest = (pad_starts[idx[:, :TOP_K]] + rank[:, :TOP_K]).reshape(-1)
    n_blocks = (n_rows * TOP_K + n_exp * (blk - 1) + blk - 1) // blk
    block_start = jnp.arange(n_blocks, dtype=jnp.int32) * blk
    block_expert = jnp.sum(block_start[:, None] >= pad_ends[None, :], axis=1)
    block_expert = jnp.minimum(block_expert, n_exp - 1).astype(jnp.int32)
    n_used = (pad_ends[-1:] // blk).astype(jnp.int32)
    buf = _dispatch(h, dest, pad_ends.astype(jnp.int32), n_used, n_blocks * blk)
    y_grouped = _experts(buf, block_expert, n_used, layer, w_gate_up, b_gate_up, w_down, b_down)
    return _combine(y_grouped, dest, gates, xs, norm_post, mod3, n_rows, tiles_per_batch, batch)


def kernel(x, c, ctx, c_ctx, w_mod, b_mod, norm_mix_pre, norm_mix_post, norm_ffn_pre, norm_ffn_post,
           na_w_qkv, na_w_o, na_rpb,
           diff_w_qkv, diff_w_o, diff_lambda_q1, diff_lambda_k1, diff_lambda_q2, diff_lambda_k2, diff_subln,
           moe_w_router, moe_b_router, moe_w_gate_up, moe_b_gate_up, moe_w_down, moe_b_down):
    batch, seq, d = x.shape
    n_ctx = ctx.shape[1]
    depth = w_mod.shape[0]
    assert depth == 2 and seq % TM == 0 and n_ctx % TM == 0
    n_lat = batch * seq
    n_all = n_lat + batch * n_ctx
    tiles_per_batch = seq // TM
    xs = jnp.concatenate([x.reshape(n_lat, d), ctx.reshape(batch * n_ctx, d)], axis=0)
    mod = _adaln_mod(c, c_ctx, w_mod, b_mod).reshape(depth, SUBLANES, 1, 6 * d)

    na_dh = d // NA_HEADS
    q, k, v = _norm_qkv(xs, norm_mix_pre[0], mod[0], na_w_qkv[0].astype(BF16), tiles_per_batch, batch,
                        q_scale=na_dh ** -0.5)
    o = _na_attention(q, k, v, na_rpb[0], batch, seq, n_ctx)
    xs, h, idx, gates, rank, cnt = _proj_post(o, na_w_o[0].astype(BF16), xs, norm_mix_post[0], norm_ffn_pre[0],
                                              mod[0], moe_w_router[0], moe_b_router[0], n_all, tiles_per_batch, batch)
    xs = _moe_layer(h, idx, gates, rank, cnt, xs, norm_ffn_post[0], mod[0], 0, moe_w_gate_up, moe_b_gate_up,
                    moe_w_down, moe_b_down, n_all, tiles_per_batch, batch)

    diff_dh = d // DIFF_HEADS // 2
    lambda_init = 0.8 - 0.6 * math.exp(-0.3 * 1)
    cols = _pair_split_columns(2 * DIFF_HEADS, diff_dh)
    w1 = diff_w_qkv[0]
    w1 = jnp.concatenate([w1[:, :d][:, cols], w1[:, d:2 * d][:, cols], w1[:, 2 * d:]], axis=1).astype(BF16)
    tables = _rope_tables(seq, batch, batch * n_ctx, diff_dh)
    q, k, v = _norm_qkv(xs, norm_mix_pre[1], mod[1], w1, tiles_per_batch, batch,
                        q_scale=diff_dh ** -0.5 * LOG2_E, rope_tables=tables)
    o = _diff_attention(q, k, v, diff_lambda_q1[0], diff_lambda_k1[0], diff_lambda_q2[0], diff_lambda_k2[0],
                        diff_subln[0], lambda_init, batch, seq, n_ctx)
    xs, h, idx, gates, rank, cnt = _proj_post(o, diff_w_o[0].astype(BF16), xs, norm_mix_post[1], norm_ffn_pre[1],
                                              mod[1], moe_w_router[1], moe_b_router[1], n_lat, tiles_per_batch, batch)
    xs = _moe_layer(h, idx, gates, rank, cnt, xs, norm_ffn_post[1], mod[1], 1, moe_w_gate_up, moe_b_gate_up,
                    moe_w_down, moe_b_down, n_lat, tiles_per_batch, batch)
    return xs.reshape(batch, seq, d)
```

```python
import functools
import math

import jax
import jax.numpy as jnp
import numpy as np
from jax import lax
from jax.experimental import pallas as pl
from jax.experimental.pallas import tpu as pltpu

F32 = jnp.float32
BF16 = jnp.bfloat16
HIGHEST = lax.Precision.HIGHEST

GRID_W = 64
NA_HEADS = 16
NA_WIN_ROWS = 8
NA_WIN_COLS = 16
DIFF_HEADS = 8
DIFF_EPS = 1e-5
ROPE_BASE = 10000.0
N_EXPERTS = 32
TOP_K = 4
SWIGLU_LIMIT = 7.0
SWIGLU_ALPHA = 1.702
NORM_EPS = 1e-6
NEG_BIG = -1e30
LOG2_E = math.log2(math.e)

LANES = 128
SUBLANES = 8
VMEM_LIMIT = 56 * 1024 * 1024

TM = 256
NA_Q_ROWS = 4
NA_K_BLOCKS = 3
NA_PAIRS_PER_STEP = 2
DIFF_TQ = 2048
DIFF_TK = 2048
DIFF_STRIP = 1024
EXPERT_ROWS = 512
TM_POST = 512
DMA_ISSUE_UNROLL = 8


def _params(semantics, **kw):
    return pltpu.CompilerParams(dimension_semantics=semantics, vmem_limit_bytes=VMEM_LIMIT, **kw)


def _rms(x, w, eps):
    return x * lax.rsqrt(jnp.mean(x * x, axis=-1, keepdims=True) + eps) * w


def _dot(a, b):
    return jnp.dot(a, b, preferred_element_type=F32)


def _dot_nt(a, b):
    return lax.dot_general(a, b, (((1,), (1,)), ((), ())), preferred_element_type=F32)


def _mod_kernel(c_ref, w_ref, b_ref, o_ref):
    cv = c_ref[...]
    act = cv * jax.nn.sigmoid(cv)
    o_ref[0] = jnp.dot(act, w_ref[0], precision=HIGHEST, preferred_element_type=F32) + b_ref[0]


def _adaln_mod(c, c_ctx, w_mod, b_mod):
    depth, d, d6 = w_mod.shape
    batch = c.shape[0]
    assert batch + 1 <= SUBLANES
    cc = jnp.zeros((SUBLANES, d), F32).at[:batch].set(c).at[batch].set(c_ctx)
    tn = d6 // 4
    return pl.pallas_call(
        _mod_kernel,
        grid=(depth, d6 // tn),
        in_specs=[
            pl.BlockSpec((SUBLANES, d), lambda l, j: (0, 0)),
            pl.BlockSpec((1, d, tn), lambda l, j: (l, 0, j)),
            pl.BlockSpec((1, 1, tn), lambda l, j: (l, 0, j)),
        ],
        out_specs=pl.BlockSpec((1, SUBLANES, tn), lambda l, j: (l, 0, j)),
        out_shape=jax.ShapeDtypeStruct((depth, SUBLANES, d6), F32),
        compiler_params=_params(("arbitrary", "arbitrary")),
        name="adaln_mod",
    )(cc, w_mod, b_mod.reshape(depth, 1, d6))


def _mod_spec(d, chunk, tiles_per_batch, batch):
    return pl.BlockSpec((1, 1, d), lambda t: (jnp.minimum(t // tiles_per_batch, batch), 0, chunk))


def _norm_qkv_kernel(x_ref, nw_ref, sh_ref, sc_ref, w_ref, *rest, rope, q_scale):
    if rope:
        cos_ref, s1_ref, s2_ref, q_ref, k_ref, v_ref = rest
    else:
        q_ref, k_ref, v_ref = rest
    d = x_ref.shape[1]
    h = (_rms(x_ref[...], nw_ref[...], NORM_EPS) * (1.0 + sc_ref[0]) + sh_ref[0]).astype(BF16)
    q = _dot(h, w_ref[:, :d])
    k = _dot(h, w_ref[:, d:2 * d])
    v = _dot(h, w_ref[:, 2 * d:])
    if rope:
        cos, s1, s2 = cos_ref[...], s1_ref[...], s2_ref[...]
    for j in range(d // LANES):
        sl = slice(j * LANES, (j + 1) * LANES)
        v_ref[j, :, :LANES] = v[:, sl].astype(v_ref.dtype)
        if rope:
            v_ref[j, :, LANES:] = jnp.ones((v_ref.shape[1], LANES), v_ref.dtype)
            for src, dst, scale in ((q, q_ref, q_scale), (k, k_ref, 1.0)):
                xs = src[:, sl]
                rot = xs * cos + pltpu.roll(xs, LANES - 32, 1) * s1 + pltpu.roll(xs, 32, 1) * s2
                dst[j] = (rot * scale).astype(dst.dtype)
        else:
            q_ref[j] = (q[:, sl] * q_scale).astype(q_ref.dtype)
            k_ref[j] = k[:, sl].astype(k_ref.dtype)


def _norm_qkv(xs, norm_w, mod3, w_qkv, tiles_per_batch, batch, q_scale, rope_tables=None):
    n, d = xs.shape
    rope = rope_tables is not None
    row = lambda t: (t, 0)
    const = lambda t: (0, 0)
    in_specs = [
        pl.BlockSpec((TM, d), row),
        pl.BlockSpec((1, d), const),
        _mod_spec(d, 0, tiles_per_batch, batch),
        _mod_spec(d, 1, tiles_per_batch, batch),
        pl.BlockSpec((d, 3 * d), const),
    ]
    args = [xs, norm_w.reshape(1, d), mod3, mod3, w_qkv]
    if rope:
        in_specs += [pl.BlockSpec((TM, LANES), row)] * 3
        args += list(rope_tables)
    heads = d // LANES
    widths = (LANES, LANES, 2 * LANES if rope else LANES)
    out_specs = [pl.BlockSpec((heads, TM, w), lambda t: (0, t, 0)) for w in widths]
    out_shape = [jax.ShapeDtypeStruct((heads, n, w), BF16) for w in widths]
    return pl.pallas_call(
        functools.partial(_norm_qkv_kernel, rope=rope, q_scale=q_scale),
        grid=(n // TM,),
        in_specs=in_specs,
        out_specs=out_specs,
        out_shape=out_shape,
        compiler_params=_params(("arbitrary",)),
        name="norm_qkv_rope" if rope else "norm_qkv",
    )(*args)


def _rope_tables(seq, batch, n_ctx_rows, head_dim):
    t = jnp.arange(seq, dtype=jnp.int32)
    row = (t // GRID_W).astype(F32)
    col = (t % GRID_W).astype(F32)
    axis_dim = head_dim // 2
    inv_freq = ROPE_BASE ** (-jnp.arange(0, axis_dim, 2, dtype=F32) / axis_dim)
    ang = jnp.concatenate([row[:, None] * inv_freq, col[:, None] * inv_freq], axis=-1)
    cos, sin = jnp.cos(ang), jnp.sin(ang)
    zero = jnp.zeros_like(sin)
    reps = LANES // head_dim

    def lay(first, second, ctx_value):
        tab = jnp.tile(jnp.concatenate([first, second], axis=-1), (batch, reps))
        return jnp.concatenate([tab, jnp.full((n_ctx_rows, LANES), ctx_value, F32)], axis=0)

    return lay(cos, cos, 1.0), lay(-sin, zero, 0.0), lay(zero, sin, 0.0)


def _pair_split_columns(n_heads, head_dim):
    p = np.arange(head_dim)
    old = np.where(p < head_dim // 2, 2 * p, 2 * (p - head_dim // 2) + 1)
    return (np.arange(n_heads)[:, None] * head_dim + old[None, :]).reshape(-1)


def _head_masks():
    lane = lax.broadcasted_iota(jnp.int32, (1, LANES), 1)
    first = lane < (LANES // 2)
    return first, jnp.logical_not(first)


def _na_kernel(q_ref, k0_ref, k1_ref, k2_ref, v0_ref, v1_ref, v2_ref, kc_ref, vc_ref, bias_ref, o_ref):
    masks = _head_masks()
    k_refs = (k0_ref, k1_ref, k2_ref)
    n_pairs, kb = k0_ref.shape[0], k0_ref.shape[1]
    ones = jnp.ones((kb, LANES), BF16)

    def scores(pr, t):
        q2 = q_ref[pr]
        qm = jnp.where(masks[t], q2, jnp.zeros_like(q2))
        s = [_dot_nt(qm, kr[pr]) + bias_ref[0, 2 * pr + t, :, j * kb:(j + 1) * kb] for j, kr in enumerate(k_refs)]
        s.append(_dot_nt(qm, kc_ref[pr]))
        return s, functools.reduce(jnp.maximum, [jnp.max(x, axis=-1, keepdims=True) for x in s])

    def attend(pr, s, m):
        v_ext = [jnp.concatenate([vr[pr], ones], axis=1) for vr in (v0_ref, v1_ref, v2_ref, vc_ref)]
        m_wide = jnp.broadcast_to(m, (m.shape[0], kb))
        acc = functools.reduce(jnp.add, [_dot(jnp.exp2(x - m_wide).astype(BF16), v) for x, v in zip(s, v_ext)])
        return acc[:, :LANES] / acc[:, LANES:]

    units = [(pr, t) for pr in range(n_pairs) for t in range(2)]
    nxt = scores(*units[0])
    outs = []
    for u, (pr, t) in enumerate(units):
        cur = nxt
        if u + 1 < len(units):
            nxt = scores(*units[u + 1])
        outs.append(attend(pr, *cur))
        if t == 1:
            o_ref[:, pr * LANES:(pr + 1) * LANES] = jnp.where(masks[0], outs[-2], outs[-1]).astype(o_ref.dtype)


def _na_bias_tables(rpb):
    qn, kn = NA_Q_ROWS * GRID_W, NA_K_BLOCKS * NA_Q_ROWS * GRID_W
    qr, qc = np.arange(qn) // GRID_W, np.arange(qn) % GRID_W
    kr, kc = np.arange(kn) // GRID_W, np.arange(kn) % GRID_W
    c0 = np.clip(qc - NA_WIN_COLS // 2, 0, GRID_W - NA_WIN_COLS)
    col_ok = (kc[None, :] >= c0[:, None]) & (kc[None, :] < c0[:, None] + NA_WIN_COLS)
    n_dr, n_dc = 2 * NA_WIN_ROWS - 1, 2 * NA_WIN_COLS - 1
    last_start = NA_K_BLOCKS * NA_Q_ROWS - NA_WIN_ROWS
    q_rows, k_rows = np.arange(NA_Q_ROWS), np.arange(NA_K_BLOCKS * NA_Q_ROWS)
    sel_dr, oks = [], []
    for delta, w0 in ((0, np.zeros_like(qr)), (NA_Q_ROWS, qr), (last_start + NA_Q_ROWS, np.full_like(qr, last_start))):
        row_ok = (kr[None, :] >= w0[:, None]) & (kr[None, :] < w0[:, None] + NA_WIN_ROWS)
        dr = np.clip(k_rows[None, :] - q_rows[:, None] - delta + (NA_WIN_ROWS - 1), 0, n_dr - 1)
        sel_dr.append(dr[:, :, None] == np.arange(n_dr))
        oks.append(row_ok & col_ok)
    cols = np.arange(GRID_W)
    dc = np.clip(cols[None, :] - cols[:, None] + (NA_WIN_COLS - 1), 0, n_dc - 1)
    sel_dr = jnp.asarray(np.stack(sel_dr), F32)
    sel_dc = jnp.asarray(dc[:, :, None] == np.arange(n_dc), F32)
    by_row = jnp.einsum("vabr,hrc->vhabc", sel_dr, rpb, precision=HIGHEST)
    tabs = jnp.einsum("vhabc,xyc->vhaxby", by_row, sel_dc, precision=HIGHEST)
    tabs = tabs.reshape(len(oks), rpb.shape[0], qn, kn)
    tabs = jnp.where(jnp.asarray(np.stack(oks))[:, None], tabs * LOG2_E, NEG_BIG)
    return jnp.concatenate([tabs, jnp.full_like(tabs[:1], NEG_BIG)], axis=0).astype(F32)


def _na_attention(q, k, v, rpb, batch, seq, n_ctx):
    n_pairs, n, _ = q.shape
    d = n_pairs * LANES
    qb = NA_Q_ROWS * GRID_W
    nblk = seq // qb
    assert seq % qb == 0 and nblk >= NA_K_BLOCKS and n_ctx == qb and (batch * seq) % n_ctx == 0
    assert seq // GRID_W >= NA_WIN_ROWS and NA_K_BLOCKS * NA_Q_ROWS == NA_Q_ROWS + NA_WIN_ROWS
    bias = _na_bias_tables(rpb)
    ctx_blk0 = batch * seq // n_ctx

    pp = NA_PAIRS_PER_STEP
    assert n_pairs % pp == 0

    def kv_spec(j):
        return pl.BlockSpec((pp, qb, LANES),
                            lambda b, c, i: (c, b * nblk + jnp.clip(i - 1, 0, nblk - NA_K_BLOCKS) + j, 0))

    ctx_spec = pl.BlockSpec((pp, n_ctx, LANES), lambda b, c, i: (c, ctx_blk0 + b, 0))
    q_row = lambda b, i: jnp.where(i < nblk, b * nblk + i, ctx_blk0 + b)
    q_spec = pl.BlockSpec((pp, qb, LANES), lambda b, c, i: (c, q_row(b, i), 0))
    variant = lambda i: jnp.where(i == 0, 0, jnp.where(i < nblk - 1, 1, jnp.where(i == nblk - 1, 2, 3)))
    return pl.pallas_call(
        _na_kernel,
        grid=(batch, n_pairs // pp, nblk + 1),
        in_specs=[q_spec] + [kv_spec(j) for j in range(NA_K_BLOCKS)] * 2
        + [ctx_spec, ctx_spec,
           pl.BlockSpec((1, 2 * pp, qb, NA_K_BLOCKS * qb), lambda b, c, i: (variant(i), c, 0, 0))],
        out_specs=pl.BlockSpec((qb, pp * LANES), lambda b, c, i: (q_row(b, i), c)),
        out_shape=jax.ShapeDtypeStruct((n, d), BF16),
        compiler_params=_params(("arbitrary", "arbitrary", "arbitrary")),
        name="na_attention",
    )(q, k, k, k, v, v, v, k, v, bias)


def _diff_scores(qm, k):
    s = _dot_nt(qm, k)
    return s, jnp.max(s, axis=1, keepdims=True)


def _diff_accumulate(s, m_cur, v_ref, m_prev, acc_prev):
    m_new = jnp.broadcast_to(m_cur, (s.shape[0], LANES)) if m_prev is None else jnp.maximum(m_prev, m_cur)
    m_wide = jnp.concatenate([m_new, m_new], axis=1)
    acc = None
    for c in range(0, s.shape[1], 2 * LANES):
        p = jnp.exp2(s[:, c:c + 2 * LANES] - m_wide).astype(BF16)
        pv = _dot(p, v_ref[0, c:c + 2 * LANES, :])
        acc = pv if acc is None else acc + pv
    if m_prev is not None:
        alpha = jnp.exp2(m_prev - m_new)
        acc = jnp.concatenate([alpha, alpha], axis=1) * acc_prev + acc
    return m_new, acc


def _diff_kernel(q_ref, k_ref, v_ref, kc_ref, vc_ref, lq1_ref, lk1_ref, lq2_ref, lk2_ref, sub_ref,
                 o_ref, m_sc, acc_sc, *, lambda_init):
    j = pl.program_id(3)
    tq = q_ref.shape[1]
    units = [(t, r) for r in range(0, tq, DIFF_STRIP) for t in range(2)]
    masks = _head_masks()

    def q_strip(t, r):
        qs = q_ref[0, r:r + DIFF_STRIP, :]
        return jnp.where(masks[t], qs, jnp.zeros_like(qs))

    @pl.when(j == 0)
    def _():
        for t, r in units:
            m, acc = _diff_accumulate(*_diff_scores(q_strip(t, r), kc_ref[0]), vc_ref, None, None)
            m_sc[t, r:r + DIFF_STRIP, :] = m
            acc_sc[t, r:r + DIFF_STRIP, :] = acc

    nxt = _diff_scores(q_strip(*units[0]), k_ref[0])
    for u, (t, r) in enumerate(units):
        cur = nxt
        if u + 1 < len(units):
            nxt = _diff_scores(q_strip(*units[u + 1]), k_ref[0])
        rows = slice(r, r + DIFF_STRIP)
        m, acc = _diff_accumulate(*cur, v_ref, m_sc[t, rows, :], acc_sc[t, rows, :])
        m_sc[t, rows, :] = m
        acc_sc[t, rows, :] = acc

    @pl.when(j == pl.num_programs(3) - 1)
    def _():
        lam = (jnp.exp(jnp.sum(lq1_ref[...] * lk1_ref[...], axis=-1, keepdims=True))
               - jnp.exp(jnp.sum(lq2_ref[...] * lk2_ref[...], axis=-1, keepdims=True)) + lambda_init)
        a1, a2 = acc_sc[0], acc_sc[1]
        o = a1[:, :LANES] / a1[:, LANES:] - lam * (a2[:, :LANES] / a2[:, LANES:])
        o_ref[...] = (_rms(o, sub_ref[...], DIFF_EPS) * (1.0 - lambda_init)).astype(o_ref.dtype)


def _diff_attention(q, k, v, lq1, lk1, lq2, lk2, subln, lambda_init, batch, seq, n_ctx):
    n_heads, vw = v.shape[0], v.shape[2]
    d = n_heads * LANES
    tq, tk = min(DIFF_TQ, seq), min(DIFF_TK, seq)
    assert seq % tq == 0 and seq % tk == 0 and tq % DIFF_STRIP == 0 and (batch * seq) % n_ctx == 0
    assert subln.shape[-1] == LANES and vw == 2 * LANES
    nq, nk = seq // tq, seq // tk
    ctx_blk0 = batch * seq // n_ctx
    q_spec = pl.BlockSpec((1, tq, LANES), lambda b, h, i, j: (h, b * nq + i, 0))
    k_spec = pl.BlockSpec((1, tk, LANES), lambda b, h, i, j: (h, b * nk + j, 0))
    v_spec = pl.BlockSpec((1, tk, vw), lambda b, h, i, j: (h, b * nk + j, 0))
    kc_spec = pl.BlockSpec((1, n_ctx, LANES), lambda b, h, i, j: (h, ctx_blk0 + b, 0))
    vc_spec = pl.BlockSpec((1, n_ctx, vw), lambda b, h, i, j: (h, ctx_blk0 + b, 0))
    vec = lambda a: a.reshape(1, -1).astype(F32)
    vec_spec = lambda a: pl.BlockSpec((1, a.shape[-1]), lambda b, h, i, j: (0, 0))
    return pl.pallas_call(
        functools.partial(_diff_kernel, lambda_init=lambda_init),
        grid=(batch, n_heads, nq, nk),
        in_specs=[q_spec, k_spec, v_spec, kc_spec, vc_spec,
                  vec_spec(lq1), vec_spec(lk1), vec_spec(lq2), vec_spec(lk2), vec_spec(subln)],
        out_specs=pl.BlockSpec((tq, LANES), lambda b, h, i, j: (b * nq + i, h)),
        out_shape=jax.ShapeDtypeStruct((batch * seq, d), BF16),
        scratch_shapes=[pltpu.VMEM((2, tq, LANES), F32), pltpu.VMEM((2, tq, vw), F32)],
        compiler_params=_params(("arbitrary", "arbitrary", "arbitrary", "arbitrary")),
        name="diff_attention",
    )(q, k, v, k, v, vec(lq1), vec(lk1), vec(lq2), vec(lk2), vec(subln))


def _lane_pack(cols, shape, dtype):
    lane = lax.broadcasted_iota(jnp.int32, shape, 1)
    out = jnp.zeros(shape, dtype)
    for kk, col in enumerate(cols):
        out = jnp.where(lane == kk, col.astype(dtype), out)
    return out


def _proj_post_kernel(o_ref, wo_ref, x_ref, npost_ref, g_ref, npre_ref, sh_ref, sc_ref, wr_ref, br_ref,
                      xo_ref, h_ref, idx_ref, gate_ref, rank_ref, cnt_ref, cnt_sc):
    @pl.when(pl.program_id(0) == 0)
    def _():
        cnt_sc[...] = jnp.zeros_like(cnt_sc)

    a = _dot(o_ref[...], wo_ref[...])
    xn = x_ref[...] + g_ref[0] * _rms(a, npost_ref[...], NORM_EPS)
    xo_ref[...] = xn
    h = _rms(xn, npre_ref[...], NORM_EPS) * (1.0 + sc_ref[0]) + sh_ref[0]
    h_ref[...] = h
    logits = jnp.dot(h, wr_ref[...], precision=HIGHEST, preferred_element_type=F32) + br_ref[...]

    tm = logits.shape[0]
    lane = lax.broadcasted_iota(jnp.int32, (tm, LANES), 1).astype(F32)
    vals, idxs, hits = [], [], []
    work = logits
    for _ in range(TOP_K):
        m = jnp.max(work, axis=-1, keepdims=True)
        idx = jnp.min(jnp.where(work == m, lane, float(LANES)), axis=-1, keepdims=True)
        hit = lane == idx
        work = jnp.where(hit, -jnp.inf, work)
        vals.append(m)
        idxs.append(idx)
        hits.append(hit)
    ex = [jnp.exp(vv - vals[0]) for vv in vals]
    den = functools.reduce(jnp.add, ex)
    gates = [e / den for e in ex]

    member = functools.reduce(jnp.logical_or, hits)
    row = lax.broadcasted_iota(jnp.int32, (tm, tm), 0)
    colm = lax.broadcasted_iota(jnp.int32, (tm, tm), 1)
    earlier = (colm < row).astype(BF16)
    before = _dot(earlier, member.astype(BF16)) + cnt_sc[...]
    ranks = [jnp.sum(jnp.where(hit, before, 0.0), axis=-1, keepdims=True) for hit in hits]
    cnt_sc[...] = cnt_sc[...] + jnp.sum(member.astype(F32), axis=0, keepdims=True)

    idx_ref[...] = _lane_pack(idxs, (tm, LANES), jnp.int32)
    gate_ref[...] = _lane_pack(gates, (tm, LANES), F32)
    rank_ref[...] = _lane_pack(ranks, (tm, LANES), jnp.int32)
    cnt_ref[...] = jnp.broadcast_to(cnt_sc[...], cnt_ref.shape)


def _proj_post(o, w_o, xs, norm_post, norm_pre, mod3, w_router, b_router, n_rows, tiles_per_batch, batch):
    d = xs.shape[1]
    n_exp = w_router.shape[1]
    wr = jnp.zeros((d, LANES), F32).at[:, :n_exp].set(w_router)
    br = jnp.full((1, LANES), NEG_BIG, F32).at[0, :n_exp].set(b_router)
    row = lambda t: (t, 0)
    const = lambda t: (0, 0)
    tm = TM_POST
    assert n_rows % tm == 0 and (tiles_per_batch * TM) % tm == 0
    mspec = lambda chunk: _mod_spec(d, chunk, tiles_per_batch * TM // tm, batch)
    wide = lambda dt: jax.ShapeDtypeStruct((n_rows, LANES), dt)
    return pl.pallas_call(
        _proj_post_kernel,
        grid=(n_rows // tm,),
        in_specs=[
            pl.BlockSpec((tm, d), row), pl.BlockSpec((d, d), const), pl.BlockSpec((tm, d), row),
            pl.BlockSpec((1, d), const), mspec(2), pl.BlockSpec((1, d), const), mspec(3), mspec(4),
            pl.BlockSpec((d, LANES), const), pl.BlockSpec((1, LANES), const),
        ],
        out_specs=[pl.BlockSpec((tm, d), row), pl.BlockSpec((tm, d), row),
                   pl.BlockSpec((tm, LANES), row), pl.BlockSpec((tm, LANES), row), pl.BlockSpec((tm, LANES), row),
                   pl.BlockSpec((SUBLANES, LANES), const)],
        out_shape=[jax.ShapeDtypeStruct((n_rows, d), F32), jax.ShapeDtypeStruct((n_rows, d), F32),
                   wide(jnp.int32), wide(F32), wide(jnp.int32), jax.ShapeDtypeStruct((SUBLANES, LANES), F32)],
        scratch_shapes=[pltpu.VMEM((1, LANES), F32)],
        compiler_params=_params(("arbitrary",)),
        name="proj_post_router",
    )(o, w_o, xs, norm_post.reshape(1, d), mod3, norm_pre.reshape(1, d), mod3, mod3, wr, br)


def _for_each_assignment(n_tokens, fn):
    def body(i, carry):
        for kk in range(TOP_K):
            fn(i, kk)
        return carry
    lax.fori_loop(0, n_tokens, body, 0, unroll=DMA_ISSUE_UNROLL)


def _dispatch_kernel(ends_ref, nu_ref, dest_ref, h_ref, buf_ref, zero_sc, sem, zero_sem):
    tm = h_ref.shape[0]
    blk = zero_sc.shape[0]

    @pl.when(pl.program_id(0) == 0)
    def _():
        zero_sc[...] = jnp.zeros_like(zero_sc)
        zero_copy = lambda row: pltpu.make_async_copy(zero_sc, buf_ref.at[pl.ds(row, blk), :], zero_sem)
        n_exp = ends_ref.shape[0]
        non_empty = [ends_ref[e] > (ends_ref[e - 1] if e else 0) for e in range(n_exp)]
        for action in ("start", "wait"):
            for e in range(n_exp):
                @pl.when(non_empty[e])
                def _():
                    getattr(zero_copy(pl.multiple_of(ends_ref[e] - blk, blk)), action)()

            def tail(b, carry):
                getattr(zero_copy(pl.multiple_of(b * blk, blk)), action)()
                return carry
            lax.fori_loop(nu_ref[0], buf_ref.shape[0] // blk, tail, 0)

    def row_copy(i, kk):
        row = dest_ref[i * TOP_K + kk]
        return pltpu.make_async_copy(h_ref.at[pl.ds(i, 1), :], buf_ref.at[pl.ds(row, 1), :], sem)

    _for_each_assignment(tm, lambda i, kk: row_copy(i, kk).start())
    for _ in range(TOP_K):
        pltpu.make_async_copy(h_ref, buf_ref.at[pl.ds(0, tm), :], sem).wait()


def _dispatch(h, dest_flat, pad_ends, n_used, n_buf_rows):
    n, d = h.shape
    grid_spec = pltpu.PrefetchScalarGridSpec(
        num_scalar_prefetch=2,
        grid=(n // TM,),
        in_specs=[pl.BlockSpec((TM * TOP_K,), lambda t, ends, nu: (t,), memory_space=pltpu.SMEM),
                  pl.BlockSpec((TM, d), lambda t, ends, nu: (t, 0))],
        out_specs=pl.BlockSpec(memory_space=pl.ANY),
        scratch_shapes=[pltpu.VMEM((EXPERT_ROWS, d), h.dtype), pltpu.SemaphoreType.DMA, pltpu.SemaphoreType.DMA],
    )
    return pl.pallas_call(
        _dispatch_kernel,
        grid_spec=grid_spec,
        out_shape=jax.ShapeDtypeStruct((n_buf_rows, d), h.dtype),
        compiler_params=_params(("arbitrary",), has_side_effects=True, disable_bounds_checks=True),
        name="moe_dispatch",
    )(pad_ends, n_used, dest_flat, h)


def _expert_kernel(be_ref, nu_ref, x_ref, wgu_ref, bgu_ref, wd_ref, bd_ref, o_ref, wgu_bf, wd_bf):
    b = pl.program_id(0)
    changed = jnp.logical_or(b == 0, be_ref[b] != be_ref[jnp.maximum(b - 1, 0)])

    @pl.when(changed)
    def _():
        wgu_bf[...] = wgu_ref[0].astype(BF16)
        wd_bf[...] = wd_ref[0].astype(BF16)

    @pl.when(b < nu_ref[0])
    def _():
        f = wd_bf.shape[0]
        gu = _dot(x_ref[...].astype(BF16), wgu_bf[...]) + bgu_ref[0]
        g = jnp.minimum(gu[:, :f], SWIGLU_LIMIT)
        u = jnp.clip(gu[:, f:], -SWIGLU_LIMIT, SWIGLU_LIMIT)
        act = (u + 1.0) * (g * jax.nn.sigmoid(SWIGLU_ALPHA * g))
        o_ref[...] = _dot(act.astype(BF16), wd_bf[...]) + bd_ref[0]

    @pl.when(b >= nu_ref[0])
    def _():
        o_ref[...] = jnp.zeros_like(o_ref)


def _experts(buf, block_expert, n_used, layer, w_gate_up, b_gate_up, w_down, b_down):
    n_rows, d = buf.shape
    depth, n_exp, _, f2 = w_gate_up.shape
    f = w_down.shape[2]
    n_blocks = n_rows // EXPERT_ROWS
    block_expert = block_expert + layer * n_exp
    w_gate_up, w_down = w_gate_up.reshape(depth * n_exp, d, f2), w_down.reshape(depth * n_exp, f, d)
    n_exp = depth * n_exp
    grid_spec = pltpu.PrefetchScalarGridSpec(
        num_scalar_prefetch=2,
        grid=(n_blocks,),
        in_specs=[
            pl.BlockSpec((EXPERT_ROWS, d), lambda b, be, nu: (b, 0)),
            pl.BlockSpec((1, d, f2), lambda b, be, nu: (be[b], 0, 0)),
            pl.BlockSpec((1, 1, f2), lambda b, be, nu: (be[b], 0, 0)),
            pl.BlockSpec((1, f, d), lambda b, be, nu: (be[b], 0, 0)),
            pl.BlockSpec((1, 1, d), lambda b, be, nu: (be[b], 0, 0)),
        ],
        out_specs=pl.BlockSpec((EXPERT_ROWS, d), lambda b, be, nu: (b, 0)),
        scratch_shapes=[pltpu.VMEM((d, f2), BF16), pltpu.VMEM((f, d), BF16)],
    )
    return pl.pallas_call(
        _expert_kernel,
        grid_spec=grid_spec,
        out_shape=jax.ShapeDtypeStruct((n_rows, d), F32),
        compiler_params=_params(("arbitrary",)),
        name="moe_experts",
    )(block_expert, n_used, buf, w_gate_up, b_gate_up.reshape(n_exp, 1, f2), w_down, b_down.reshape(n_exp, 1, d))


def _combine_kernel(dest_ref, gate_ref, x_ref, nw_ref, g_ref, y_ref, xo_ref, rows_sc, sem):
    def row_copy(i, kk):
        row = dest_ref[i * TOP_K + kk]
        return pltpu.make_async_copy(y_ref.at[pl.ds(row, 1), :], rows_sc.at[kk, pl.ds(i, 1), :], sem)

    tm = x_ref.shape[0]
    _for_each_assignment(tm, lambda i, kk: row_copy(i, kk).start())
    for kk in range(TOP_K):
        pltpu.make_async_copy(y_ref.at[pl.ds(0, tm), :], rows_sc.at[kk], sem).wait()
    gates = gate_ref[...]
    y = gates[:, 0:1] * rows_sc[0]
    for kk in range(1, TOP_K):
        y = y + gates[:, kk:kk + 1] * rows_sc[kk]
    xo_ref[...] = x_ref[...] + g_ref[0] * _rms(y, nw_ref[...], NORM_EPS)


def _combine(y_grouped, dest_flat, gates, xs, norm_w, mod3, n_rows, tiles_per_batch, batch):
    d = xs.shape[1]
    row = lambda t: (t, 0)
    return pl.pallas_call(
        _combine_kernel,
        grid=(n_rows // TM,),
        in_specs=[pl.BlockSpec((TM * TOP_K,), lambda t: (t,), memory_space=pltpu.SMEM),
                  pl.BlockSpec((TM, LANES), row), pl.BlockSpec((TM, d), row),
                  pl.BlockSpec((1, d), lambda t: (0, 0)), _mod_spec(d, 5, tiles_per_batch, batch),
                  pl.BlockSpec(memory_space=pl.ANY)],
        out_specs=pl.BlockSpec((TM, d), row),
        out_shape=jax.ShapeDtypeStruct((n_rows, d), F32),
        scratch_shapes=[pltpu.VMEM((TOP_K, TM, d), F32), pltpu.SemaphoreType.DMA],
        compiler_params=_params(("arbitrary",), disable_bounds_checks=True),
        name="moe_combine",
    )(dest_flat, gates, xs, norm_w.reshape(1, d), mod3, y_grouped)


def _moe_layer(h, idx, gates, rank, cnt, xs, norm_post, mod3, layer, w_gate_up, b_gate_up, w_down, b_down,
               n_rows, tiles_per_batch, batch):
    n_exp = w_gate_up.shape[1]
    blk = EXPERT_ROWS
    counts = cnt[0, :n_exp].astype(jnp.int32)
    padded = (counts + blk - 1) // blk * blk
    pad_ends = jnp.cumsum(padded)
    pad_starts = pad_ends - padded
    dest = (pad_starts[idx[:, :TOP_K]] + rank[:, :TOP_K]).reshape(-1)
    n_blocks = (n_rows * TOP_K + n_exp * (blk - 1) + blk - 1) // blk
    block_start = jnp.arange(n_blocks, dtype=jnp.int32) * blk
    block_expert = jnp.sum(block_start[:, None] >= pad_ends[None, :], axis=1)
    block_expert = jnp.minimum(block_expert, n_exp - 1).astype(jnp.int32)
    n_used = (pad_ends[-1:] // blk).astype(jnp.int32)
    buf = _dispatch(h, dest, pad_ends.astype(jnp.int32), n_used, n_blocks * blk)
    y_grouped = _experts(buf, block_expert, n_used, layer, w_gate_up, b_gate_up, w_down, b_down)
    return _combine(y_grouped, dest, gates, xs, norm_post, mod3, n_rows, tiles_per_batch, batch)


def kernel(x, c, ctx, c_ctx, w_mod, b_mod, norm_mix_pre, norm_mix_post, norm_ffn_pre, norm_ffn_post,
           na_w_qkv, na_w_o, na_rpb,
           diff_w_qkv, diff_w_o, diff_lambda_q1, diff_lambda_k1, diff_lambda_q2, diff_lambda_k2, diff_subln,
           moe_w_router, moe_b_router, moe_w_gate_up, moe_b_gate_up, moe_w_down, moe_b_down):
    batch, seq, d = x.shape
    n_ctx = ctx.shape[1]
    depth = w_mod.shape[0]
    assert depth == 2 and seq % TM == 0 and n_ctx % TM == 0
    n_lat = batch * seq
    n_all = n_lat + batch * n_ctx
    tiles_per_batch = seq // TM
    xs = jnp.concatenate([x.reshape(n_lat, d), ctx.reshape(batch * n_ctx, d)], axis=0)
    mod = _adaln_mod(c, c_ctx, w_mod, b_mod).reshape(depth, SUBLANES, 1, 6 * d)

    na_dh = d // NA_HEADS
    q, k, v = _norm_qkv(xs, norm_mix_pre[0], mod[0], na_w_qkv[0].astype(BF16), tiles_per_batch, batch,
                        q_scale=na_dh ** -0.5 * LOG2_E)
    o = _na_attention(q, k, v, na_rpb[0], batch, seq, n_ctx)
    xs, h, idx, gates, rank, cnt = _proj_post(o, na_w_o[0].astype(BF16), xs, norm_mix_post[0], norm_ffn_pre[0],
                                              mod[0], moe_w_router[0], moe_b_router[0], n_all, tiles_per_batch, batch)
    xs = _moe_layer(h, idx, gates, rank, cnt, xs, norm_ffn_post[0], mod[0], 0, moe_w_gate_up, moe_b_gate_up,
                    moe_w_down, moe_b_down, n_all, tiles_per_batch, batch)

    diff_dh = d // DIFF_HEADS // 2
    lambda_init = 0.8 - 0.6 * math.exp(-0.3 * 1)
    cols = _pair_split_columns(2 * DIFF_HEADS, diff_dh)
    w1 = diff_w_qkv[0]
    w1 = jnp.concatenate([w1[:, :d][:, cols], w1[:, d:2 * d][:, cols], w1[:, 2 * d:]], axis=1).astype(BF16)
    tables = _rope_tables(seq, batch, batch * n_ctx, diff_dh)
    q, k, v = _norm_qkv(xs, norm_mix_pre[1], mod[1], w1, tiles_per_batch, batch,
                        q_scale=diff_dh ** -0.5 * LOG2_E, rope_tables=tables)
    o = _diff_attention(q, k, v, diff_lambda_q1[0], diff_lambda_k1[0], diff_lambda_q2[0], diff_lambda_k2[0],
                        diff_subln[0], lambda_init, batch, seq, n_ctx)
    xs, h, idx, gates, rank, cnt = _proj_post(o, diff_w_o[0].astype(BF16), xs, norm_mix_post[1], norm_ffn_pre[1],
                                              mod[1], moe_w_router[1], moe_b_router[1], n_lat, tiles_per_batch, batch)
    xs = _moe_layer(h, idx, gates, rank, cnt, xs, norm_ffn_post[1], mod[1], 1, moe_w_gate_up, moe_b_gate_up,
                    moe_w_down, moe_b_down, n_lat, tiles_per_batch, batch)
    return xs.reshape(batch, seq, d)
```

```python
import functools
import math

import jax
import jax.numpy as jnp
import numpy as np
from jax import lax
from jax.experimental import pallas as pl
from jax.experimental.pallas import tpu as pltpu

F32 = jnp.float32
BF16 = jnp.bfloat16
HIGHEST = lax.Precision.HIGHEST

GRID_W = 64
NA_HEADS = 16
NA_WIN_ROWS = 8
NA_WIN_COLS = 16
DIFF_HEADS = 8
DIFF_EPS = 1e-5
ROPE_BASE = 10000.0
N_EXPERTS = 32
TOP_K = 4
SWIGLU_LIMIT = 7.0
SWIGLU_ALPHA = 1.702
NORM_EPS = 1e-6
NEG_BIG = -1e30
LOG2_E = math.log2(math.e)

LANES = 128
SUBLANES = 8
VMEM_LIMIT = 56 * 1024 * 1024

TM = 256
NA_Q_ROWS = 4
NA_K_BLOCKS = 3
NA_PAIRS_PER_STEP = 4
DIFF_TQ = 2048
DIFF_TK = 2048
DIFF_STRIP = 1024
EXPERT_ROWS = 512
TM_POST = 512
POST_PARTS = 2
DMA_ISSUE_UNROLL = 8


def _params(semantics, **kw):
    return pltpu.CompilerParams(dimension_semantics=semantics, vmem_limit_bytes=VMEM_LIMIT, **kw)


def _rms(x, w, eps):
    return x * lax.rsqrt(jnp.mean(x * x, axis=-1, keepdims=True) + eps) * w


def _dot(a, b):
    return jnp.dot(a, b, preferred_element_type=F32)


def _dot_nt(a, b):
    return lax.dot_general(a, b, (((1,), (1,)), ((), ())), preferred_element_type=F32)


def _mod_kernel(c_ref, w_ref, b_ref, o_ref):
    cv = c_ref[...]
    act = cv * jax.nn.sigmoid(cv)
    o_ref[0] = jnp.dot(act, w_ref[0], precision=HIGHEST, preferred_element_type=F32) + b_ref[0]


def _adaln_mod(c, c_ctx, w_mod, b_mod):
    depth, d, d6 = w_mod.shape
    batch = c.shape[0]
    assert batch + 1 <= SUBLANES
    cc = jnp.zeros((SUBLANES, d), F32).at[:batch].set(c).at[batch].set(c_ctx)
    tn = d6 // 4
    return pl.pallas_call(
        _mod_kernel,
        grid=(depth, d6 // tn),
        in_specs=[
            pl.BlockSpec((SUBLANES, d), lambda l, j: (0, 0)),
            pl.BlockSpec((1, d, tn), lambda l, j: (l, 0, j)),
            pl.BlockSpec((1, 1, tn), lambda l, j: (l, 0, j)),
        ],
        out_specs=pl.BlockSpec((1, SUBLANES, tn), lambda l, j: (l, 0, j)),
        out_shape=jax.ShapeDtypeStruct((depth, SUBLANES, d6), F32),
        compiler_params=_params(("arbitrary", "arbitrary")),
        name="adaln_mod",
    )(cc, w_mod, b_mod.reshape(depth, 1, d6))


def _mod_spec(d, chunk, tiles_per_batch, batch):
    return pl.BlockSpec((1, 1, d), lambda t: (jnp.minimum(t // tiles_per_batch, batch), 0, chunk))


def _norm_qkv_kernel(x_ref, nw_ref, sh_ref, sc_ref, w_ref, *rest, rope, q_scale):
    if rope:
        cos_ref, s1_ref, s2_ref, q_ref, k_ref, v_ref = rest
    else:
        q_ref, k_ref, v_ref = rest
    d = x_ref.shape[1]
    h = (_rms(x_ref[...], nw_ref[...], NORM_EPS) * (1.0 + sc_ref[0]) + sh_ref[0]).astype(BF16)
    q = _dot(h, w_ref[:, :d])
    k = _dot(h, w_ref[:, d:2 * d])
    v = _dot(h, w_ref[:, 2 * d:])
    if rope:
        cos, s1, s2 = cos_ref[...], s1_ref[...], s2_ref[...]
    for j in range(d // LANES):
        sl = slice(j * LANES, (j + 1) * LANES)
        v_ref[j, :, :LANES] = v[:, sl].astype(v_ref.dtype)
        if rope:
            v_ref[j, :, LANES:] = jnp.ones((v_ref.shape[1], LANES), v_ref.dtype)
            for src, dst, scale in ((q, q_ref, q_scale), (k, k_ref, 1.0)):
                xs = src[:, sl]
                rot = xs * cos + pltpu.roll(xs, LANES - 32, 1) * s1 + pltpu.roll(xs, 32, 1) * s2
                dst[j] = (rot * scale).astype(dst.dtype)
        else:
            q_ref[j] = (q[:, sl] * q_scale).astype(q_ref.dtype)
            k_ref[j] = k[:, sl].astype(k_ref.dtype)


def _norm_qkv(xs, norm_w, mod3, w_qkv, tiles_per_batch, batch, q_scale, rope_tables=None):
    n, d = xs.shape
    rope = rope_tables is not None
    row = lambda t: (t, 0)
    const = lambda t: (0, 0)
    in_specs = [
        pl.BlockSpec((TM, d), row),
        pl.BlockSpec((1, d), const),
        _mod_spec(d, 0, tiles_per_batch, batch),
        _mod_spec(d, 1, tiles_per_batch, batch),
        pl.BlockSpec((d, 3 * d), const),
    ]
    args = [xs, norm_w.reshape(1, d), mod3, mod3, w_qkv]
    if rope:
        in_specs += [pl.BlockSpec((TM, LANES), row)] * 3
        args += list(rope_tables)
    heads = d // LANES
    widths = (LANES, LANES, 2 * LANES if rope else LANES)
    out_specs = [pl.BlockSpec((heads, TM, w), lambda t: (0, t, 0)) for w in widths]
    out_shape = [jax.ShapeDtypeStruct((heads, n, w), BF16) for w in widths]
    return pl.pallas_call(
        functools.partial(_norm_qkv_kernel, rope=rope, q_scale=q_scale),
        grid=(n // TM,),
        in_specs=in_specs,
        out_specs=out_specs,
        out_shape=out_shape,
        compiler_params=_params(("arbitrary",)),
        name="norm_qkv_rope" if rope else "norm_qkv",
    )(*args)


def _rope_tables(seq, batch, n_ctx_rows, head_dim):
    t = jnp.arange(seq, dtype=jnp.int32)
    row = (t // GRID_W).astype(F32)
    col = (t % GRID_W).astype(F32)
    axis_dim = head_dim // 2
    inv_freq = ROPE_BASE ** (-jnp.arange(0, axis_dim, 2, dtype=F32) / axis_dim)
    ang = jnp.concatenate([row[:, None] * inv_freq, col[:, None] * inv_freq], axis=-1)
    cos, sin = jnp.cos(ang), jnp.sin(ang)
    zero = jnp.zeros_like(sin)
    reps = LANES // head_dim

    def lay(first, second, ctx_value):
        tab = jnp.tile(jnp.concatenate([first, second], axis=-1), (batch, reps))
        return jnp.concatenate([tab, jnp.full((n_ctx_rows, LANES), ctx_value, F32)], axis=0)

    return lay(cos, cos, 1.0), lay(-sin, zero, 0.0), lay(zero, sin, 0.0)


def _pair_split_columns(n_heads, head_dim):
    p = np.arange(head_dim)
    old = np.where(p < head_dim // 2, 2 * p, 2 * (p - head_dim // 2) + 1)
    return (np.arange(n_heads)[:, None] * head_dim + old[None, :]).reshape(-1)


def _head_masks():
    lane = lax.broadcasted_iota(jnp.int32, (1, LANES), 1)
    first = lane < (LANES // 2)
    return first, jnp.logical_not(first)


def _na_kernel(q_ref, k0_ref, k1_ref, k2_ref, v0_ref, v1_ref, v2_ref, kc_ref, vc_ref, bias_ref, o_ref):
    masks = _head_masks()
    k_refs = (k0_ref, k1_ref, k2_ref)
    n_pairs, kb = k0_ref.shape[0], k0_ref.shape[1]
    ones = jnp.ones((kb, LANES), BF16)

    def scores(pr, t):
        q2 = q_ref[pr]
        qm = jnp.where(masks[t], q2, jnp.zeros_like(q2))
        s = [_dot_nt(qm, kr[pr]) + bias_ref[0, 2 * pr + t, :, j * kb:(j + 1) * kb] for j, kr in enumerate(k_refs)]
        s.append(_dot_nt(qm, kc_ref[pr]))
        return s, functools.reduce(jnp.maximum, [jnp.max(x, axis=-1, keepdims=True) for x in s])

    def attend(pr, s, m):
        v_ext = [jnp.concatenate([vr[pr], ones], axis=1) for vr in (v0_ref, v1_ref, v2_ref, vc_ref)]
        m_wide = jnp.broadcast_to(m, (m.shape[0], kb))
        acc = functools.reduce(jnp.add, [_dot(jnp.exp2(x - m_wide).astype(BF16), v) for x, v in zip(s, v_ext)])
        return acc[:, :LANES] / acc[:, LANES:]

    units = [(pr, t) for pr in range(n_pairs) for t in range(2)]
    nxt = scores(*units[0])
    outs = []
    for u, (pr, t) in enumerate(units):
        cur = nxt
        if u + 1 < len(units):
            nxt = scores(*units[u + 1])
        outs.append(attend(pr, *cur))
        if t == 1:
            o_ref[:, pr * LANES:(pr + 1) * LANES] = jnp.where(masks[0], outs[-2], outs[-1]).astype(o_ref.dtype)


def _na_bias_tables(rpb):
    qn, kn = NA_Q_ROWS * GRID_W, NA_K_BLOCKS * NA_Q_ROWS * GRID_W
    qr, qc = np.arange(qn) // GRID_W, np.arange(qn) % GRID_W
    kr, kc = np.arange(kn) // GRID_W, np.arange(kn) % GRID_W
    c0 = np.clip(qc - NA_WIN_COLS // 2, 0, GRID_W - NA_WIN_COLS)
    col_ok = (kc[None, :] >= c0[:, None]) & (kc[None, :] < c0[:, None] + NA_WIN_COLS)
    n_dr, n_dc = 2 * NA_WIN_ROWS - 1, 2 * NA_WIN_COLS - 1
    last_start = NA_K_BLOCKS * NA_Q_ROWS - NA_WIN_ROWS
    q_rows, k_rows = np.arange(NA_Q_ROWS), np.arange(NA_K_BLOCKS * NA_Q_ROWS)
    sel_dr, oks = [], []
    for delta, w0 in ((0, np.zeros_like(qr)), (NA_Q_ROWS, qr), (last_start + NA_Q_ROWS, np.full_like(qr, last_start))):
        row_ok = (kr[None, :] >= w0[:, None]) & (kr[None, :] < w0[:, None] + NA_WIN_ROWS)
        dr = np.clip(k_rows[None, :] - q_rows[:, None] - delta + (NA_WIN_ROWS - 1), 0, n_dr - 1)
        sel_dr.append(dr[:, :, None] == np.arange(n_dr))
        oks.append(row_ok & col_ok)
    cols = np.arange(GRID_W)
    dc = np.clip(cols[None, :] - cols[:, None] + (NA_WIN_COLS - 1), 0, n_dc - 1)
    sel_dr = jnp.asarray(np.stack(sel_dr), F32)
    sel_dc = jnp.asarray(dc[:, :, None] == np.arange(n_dc), F32)
    by_row = jnp.einsum("vabr,hrc->vhabc", sel_dr, rpb, precision=HIGHEST)
    tabs = jnp.einsum("vhabc,xyc->vhaxby", by_row, sel_dc, precision=HIGHEST)
    tabs = tabs.reshape(len(oks), rpb.shape[0], qn, kn)
    tabs = jnp.where(jnp.asarray(np.stack(oks))[:, None], tabs * LOG2_E, NEG_BIG)
    return jnp.concatenate([tabs, jnp.full_like(tabs[:1], NEG_BIG)], axis=0).astype(F32)


def _na_attention(q, k, v, rpb, batch, seq, n_ctx):
    n_pairs, n, _ = q.shape
    d = n_pairs * LANES
    qb = NA_Q_ROWS * GRID_W
    nblk = seq // qb
    assert seq % qb == 0 and nblk >= NA_K_BLOCKS and n_ctx == qb and (batch * seq) % n_ctx == 0
    assert seq // GRID_W >= NA_WIN_ROWS and NA_K_BLOCKS * NA_Q_ROWS == NA_Q_ROWS + NA_WIN_ROWS
    bias = _na_bias_tables(rpb)
    ctx_blk0 = batch * seq // n_ctx

    pp = NA_PAIRS_PER_STEP
    assert n_pairs % pp == 0

    def kv_spec(j):
        return pl.BlockSpec((pp, qb, LANES),
                            lambda b, c, i: (c, b * nblk + jnp.clip(i - 1, 0, nblk - NA_K_BLOCKS) + j, 0))

    ctx_spec = pl.BlockSpec((pp, n_ctx, LANES), lambda b, c, i: (c, ctx_blk0 + b, 0))
    q_row = lambda b, i: jnp.where(i < nblk, b * nblk + i, ctx_blk0 + b)
    q_spec = pl.BlockSpec((pp, qb, LANES), lambda b, c, i: (c, q_row(b, i), 0))
    variant = lambda i: jnp.where(i == 0, 0, jnp.where(i < nblk - 1, 1, jnp.where(i == nblk - 1, 2, 3)))
    return pl.pallas_call(
        _na_kernel,
        grid=(batch, n_pairs // pp, nblk + 1),
        in_specs=[q_spec] + [kv_spec(j) for j in range(NA_K_BLOCKS)] * 2
        + [ctx_spec, ctx_spec,
           pl.BlockSpec((1, 2 * pp, qb, NA_K_BLOCKS * qb), lambda b, c, i: (variant(i), c, 0, 0))],
        out_specs=pl.BlockSpec((qb, pp * LANES), lambda b, c, i: (q_row(b, i), c)),
        out_shape=jax.ShapeDtypeStruct((n, d), BF16),
        compiler_params=_params(("arbitrary", "arbitrary", "arbitrary")),
        name="na_attention",
    )(q, k, k, k, v, v, v, k, v, bias)


def _diff_scores(qm, k):
    s = _dot_nt(qm, k)
    return s, jnp.max(s, axis=1, keepdims=True)


def _diff_accumulate(s, m_cur, v_ref, m_prev, acc_prev):
    m_new = jnp.broadcast_to(m_cur, (s.shape[0], LANES)) if m_prev is None else jnp.maximum(m_prev, m_cur)
    m_wide = jnp.concatenate([m_new, m_new], axis=1)
    acc = None
    for c in range(0, s.shape[1], 2 * LANES):
        p = jnp.exp2(s[:, c:c + 2 * LANES] - m_wide).astype(BF16)
        pv = _dot(p, v_ref[0, c:c + 2 * LANES, :])
        acc = pv if acc is None else acc + pv
    if m_prev is not None:
        alpha = jnp.exp2(m_prev - m_new)
        acc = jnp.concatenate([alpha, alpha], axis=1) * acc_prev + acc
    return m_new, acc


def _diff_kernel(q_ref, k_ref, v_ref, kc_ref, vc_ref, lq1_ref, lk1_ref, lq2_ref, lk2_ref, sub_ref,
                 o_ref, m_sc, acc_sc, *, lambda_init):
    j = pl.program_id(3)
    tq = q_ref.shape[1]
    units = [(t, r) for r in range(0, tq, DIFF_STRIP) for t in range(2)]
    masks = _head_masks()

    def q_strip(t, r):
        qs = q_ref[0, r:r + DIFF_STRIP, :]
        return jnp.where(masks[t], qs, jnp.zeros_like(qs))

    @pl.when(j == 0)
    def _():
        for t, r in units:
            m, acc = _diff_accumulate(*_diff_scores(q_strip(t, r), kc_ref[0]), vc_ref, None, None)
            m_sc[t, r:r + DIFF_STRIP, :] = m
            acc_sc[t, r:r + DIFF_STRIP, :] = acc

    nxt = _diff_scores(q_strip(*units[0]), k_ref[0])
    for u, (t, r) in enumerate(units):
        cur = nxt
        if u + 1 < len(units):
            nxt = _diff_scores(q_strip(*units[u + 1]), k_ref[0])
        rows = slice(r, r + DIFF_STRIP)
        m, acc = _diff_accumulate(*cur, v_ref, m_sc[t, rows, :], acc_sc[t, rows, :])
        m_sc[t, rows, :] = m
        acc_sc[t, rows, :] = acc

    @pl.when(j == pl.num_programs(3) - 1)
    def _():
        lam = (jnp.exp(jnp.sum(lq1_ref[...] * lk1_ref[...], axis=-1, keepdims=True))
               - jnp.exp(jnp.sum(lq2_ref[...] * lk2_ref[...], axis=-1, keepdims=True)) + lambda_init)
        a1, a2 = acc_sc[0], acc_sc[1]
        o = a1[:, :LANES] / a1[:, LANES:] - lam * (a2[:, :LANES] / a2[:, LANES:])
        o_ref[...] = (_rms(o, sub_ref[...], DIFF_EPS) * (1.0 - lambda_init)).astype(o_ref.dtype)


def _diff_attention(q, k, v, lq1, lk1, lq2, lk2, subln, lambda_init, batch, seq, n_ctx):
    n_heads, vw = v.shape[0], v.shape[2]
    d = n_heads * LANES
    tq, tk = min(DIFF_TQ, seq), min(DIFF_TK, seq)
    assert seq % tq == 0 and seq % tk == 0 and tq % DIFF_STRIP == 0 and (batch * seq) % n_ctx == 0
    assert subln.shape[-1] == LANES and vw == 2 * LANES
    nq, nk = seq // tq, seq // tk
    ctx_blk0 = batch * seq // n_ctx
    q_spec = pl.BlockSpec((1, tq, LANES), lambda b, h, i, j: (h, b * nq + i, 0))
    k_spec = pl.BlockSpec((1, tk, LANES), lambda b, h, i, j: (h, b * nk + j, 0))
    v_spec = pl.BlockSpec((1, tk, vw), lambda b, h, i, j: (h, b * nk + j, 0))
    kc_spec = pl.BlockSpec((1, n_ctx, LANES), lambda b, h, i, j: (h, ctx_blk0 + b, 0))
    vc_spec = pl.BlockSpec((1, n_ctx, vw), lambda b, h, i, j: (h, ctx_blk0 + b, 0))
    vec = lambda a: a.reshape(1, -1).astype(F32)
    vec_spec = lambda a: pl.BlockSpec((1, a.shape[-1]), lambda b, h, i, j: (0, 0))
    return pl.pallas_call(
        functools.partial(_diff_kernel, lambda_init=lambda_init),
        grid=(batch, n_heads, nq, nk),
        in_specs=[q_spec, k_spec, v_spec, kc_spec, vc_spec,
                  vec_spec(lq1), vec_spec(lk1), vec_spec(lq2), vec_spec(lk2), vec_spec(subln)],
        out_specs=pl.BlockSpec((tq, LANES), lambda b, h, i, j: (b * nq + i, h)),
        out_shape=jax.ShapeDtypeStruct((batch * seq, d), BF16),
        scratch_shapes=[pltpu.VMEM((2, tq, LANES), F32), pltpu.VMEM((2, tq, vw), F32)],
        compiler_params=_params(("arbitrary", "arbitrary", "arbitrary", "arbitrary")),
        name="diff_attention",
    )(q, k, v, k, v, vec(lq1), vec(lk1), vec(lq2), vec(lk2), vec(subln))


def _lane_pack(cols, shape, dtype):
    lane = lax.broadcasted_iota(jnp.int32, shape, 1)
    out = jnp.zeros(shape, dtype)
    for kk, col in enumerate(cols):
        out = jnp.where(lane == kk, col.astype(dtype), out)
    return out


def _proj_post_kernel(o_ref, wo_ref, x_ref, npost_ref, g_ref, npre_ref, sh_ref, sc_ref, wr_ref, br_ref,
                      xo_ref, h_ref, idx_ref, gate_ref, rank_ref, cnt_ref, cnt_sc):
    @pl.when(pl.program_id(0) == 0)
    def _():
        cnt_sc[...] = jnp.zeros_like(cnt_sc)

    tm = o_ref.shape[0]
    part = tm // POST_PARTS
    parts = [slice(p * part, (p + 1) * part) for p in range(POST_PARTS)]
    proj = [_dot(o_ref[rows, :], wo_ref[...]) for rows in parts]
    hs = []
    for rows, a in zip(parts, proj):
        xn = x_ref[rows, :] + g_ref[0] * _rms(a, npost_ref[...], NORM_EPS)
        xo_ref[rows, :] = xn
        h = _rms(xn, npre_ref[...], NORM_EPS) * (1.0 + sc_ref[0]) + sh_ref[0]
        h_ref[rows, :] = h
        hs.append(h)
    all_logits = [jnp.dot(h, wr_ref[...], precision=HIGHEST, preferred_element_type=F32) + br_ref[...] for h in hs]

    lane = lax.broadcasted_iota(jnp.int32, (part, LANES), 1).astype(F32)
    row = lax.broadcasted_iota(jnp.int32, (part, part), 0)
    colm = lax.broadcasted_iota(jnp.int32, (part, part), 1)
    earlier = (colm < row).astype(BF16)
    routed = []
    for logits in all_logits:
        vals, idxs, hits = [], [], []
        work = logits
        for _ in range(TOP_K):
            m = jnp.max(work, axis=-1, keepdims=True)
            idx = jnp.min(jnp.where(work == m, lane, float(LANES)), axis=-1, keepdims=True)
            hit = lane == idx
            work = jnp.where(hit, -jnp.inf, work)
            vals.append(m)
            idxs.append(idx)
            hits.append(hit)
        ex = [jnp.exp(vv - vals[0]) for vv in vals]
        den = functools.reduce(jnp.add, ex)
        member = functools.reduce(jnp.logical_or, hits)
        within = _dot(earlier, member.astype(BF16))
        routed.append((idxs, [e / den for e in ex], hits, member, within))

    counts = cnt_sc[...]
    for rows, (idxs, gates, hits, member, within) in zip(parts, routed):
        before = within + counts
        ranks = [jnp.sum(jnp.where(hit, before, 0.0), axis=-1, keepdims=True) for hit in hits]
        counts = counts + jnp.sum(member.astype(F32), axis=0, keepdims=True)
        idx_ref[rows, :] = _lane_pack(idxs, (part, LANES), jnp.int32)
        gate_ref[rows, :] = _lane_pack(gates, (part, LANES), F32)
        rank_ref[rows, :] = _lane_pack(ranks, (part, LANES), jnp.int32)
    cnt_sc[...] = counts
    cnt_ref[...] = jnp.broadcast_to(counts, cnt_ref.shape)


def _proj_post(o, w_o, xs, norm_post, norm_pre, mod3, w_router, b_router, n_rows, tiles_per_batch, batch):
    d = xs.shape[1]
    n_exp = w_router.shape[1]
    wr = jnp.zeros((d, LANES), F32).at[:, :n_exp].set(w_router)
    br = jnp.full((1, LANES), NEG_BIG, F32).at[0, :n_exp].set(b_router)
    row = lambda t: (t, 0)
    const = lambda t: (0, 0)
    tm = TM_POST
    assert n_rows % tm == 0 and (tiles_per_batch * TM) % tm == 0
    mspec = lambda chunk: _mod_spec(d, chunk, tiles_per_batch * TM // tm, batch)
    wide = lambda dt: jax.ShapeDtypeStruct((n_rows, LANES), dt)
    return pl.pallas_call(
        _proj_post_kernel,
        grid=(n_rows // tm,),
        in_specs=[
            pl.BlockSpec((tm, d), row), pl.BlockSpec((d, d), const), pl.BlockSpec((tm, d), row),
            pl.BlockSpec((1, d), const), mspec(2), pl.BlockSpec((1, d), const), mspec(3), mspec(4),
            pl.BlockSpec((d, LANES), const), pl.BlockSpec((1, LANES), const),
        ],
        out_specs=[pl.BlockSpec((tm, d), row), pl.BlockSpec((tm, d), row),
                   pl.BlockSpec((tm, LANES), row), pl.BlockSpec((tm, LANES), row), pl.BlockSpec((tm, LANES), row),
                   pl.BlockSpec((SUBLANES, LANES), const)],
        out_shape=[jax.ShapeDtypeStruct((n_rows, d), F32), jax.ShapeDtypeStruct((n_rows, d), F32),
                   wide(jnp.int32), wide(F32), wide(jnp.int32), jax.ShapeDtypeStruct((SUBLANES, LANES), F32)],
        scratch_shapes=[pltpu.VMEM((1, LANES), F32)],
        compiler_params=_params(("arbitrary",)),
        name="proj_post_router",
    )(o, w_o, xs, norm_post.reshape(1, d), mod3, norm_pre.reshape(1, d), mod3, mod3, wr, br)


def _for_each_assignment(n_tokens, fn):
    def body(i, carry):
        for kk in range(TOP_K):
            fn(i, kk)
        return carry
    lax.fori_loop(0, n_tokens, body, 0, unroll=DMA_ISSUE_UNROLL)


def _dispatch_kernel(ends_ref, nu_ref, dest_ref, h_ref, buf_ref, zero_sc, sem, zero_sem):
    tm = h_ref.shape[0]
    blk = zero_sc.shape[0]

    @pl.when(pl.program_id(0) == 0)
    def _():
        zero_sc[...] = jnp.zeros_like(zero_sc)
        zero_copy = lambda row: pltpu.make_async_copy(zero_sc, buf_ref.at[pl.ds(row, blk), :], zero_sem)
        n_exp = ends_ref.shape[0]
        non_empty = [ends_ref[e] > (ends_ref[e - 1] if e else 0) for e in range(n_exp)]
        for action in ("start", "wait"):
            for e in range(n_exp):
                @pl.when(non_empty[e])
                def _():
                    getattr(zero_copy(pl.multiple_of(ends_ref[e] - blk, blk)), action)()

            def tail(b, carry):
                getattr(zero_copy(pl.multiple_of(b * blk, blk)), action)()
                return carry
            lax.fori_loop(nu_ref[0], buf_ref.shape[0] // blk, tail, 0)

    def row_copy(i, kk):
        row = dest_ref[i * TOP_K + kk]
        return pltpu.make_async_copy(h_ref.at[pl.ds(i, 1), :], buf_ref.at[pl.ds(row, 1), :], sem)

    _for_each_assignment(tm, lambda i, kk: row_copy(i, kk).start())
    for _ in range(TOP_K):
        pltpu.make_async_copy(h_ref, buf_ref.at[pl.ds(0, tm), :], sem).wait()


def _dispatch(h, dest_flat, pad_ends, n_used, n_buf_rows):
    n, d = h.shape
    grid_spec = pltpu.PrefetchScalarGridSpec(
        num_scalar_prefetch=2,
        grid=(n // TM,),
        in_specs=[pl.BlockSpec((TM * TOP_K,), lambda t, ends, nu: (t,), memory_space=pltpu.SMEM),
                  pl.BlockSpec((TM, d), lambda t, ends, nu: (t, 0))],
        out_specs=pl.BlockSpec(memory_space=pl.ANY),
        scratch_shapes=[pltpu.VMEM((EXPERT_ROWS, d), h.dtype), pltpu.SemaphoreType.DMA, pltpu.SemaphoreType.DMA],
    )
    return pl.pallas_call(
        _dispatch_kernel,
        grid_spec=grid_spec,
        out_shape=jax.ShapeDtypeStruct((n_buf_rows, d), h.dtype),
        compiler_params=_params(("arbitrary",), has_side_effects=True, disable_bounds_checks=True),
        name="moe_dispatch",
    )(pad_ends, n_used, dest_flat, h)


def _expert_kernel(be_ref, nu_ref, x_ref, wgu_ref, bgu_ref, wd_ref, bd_ref, o_ref, wgu_bf, wd_bf):
    b = pl.program_id(0)
    changed = jnp.logical_or(b == 0, be_ref[b] != be_ref[jnp.maximum(b - 1, 0)])

    @pl.when(changed)
    def _():
        wgu_bf[...] = wgu_ref[0].astype(BF16)
        wd_bf[...] = wd_ref[0].astype(BF16)

    @pl.when(b < nu_ref[0])
    def _():
        f = wd_bf.shape[0]
        gu = _dot(x_ref[...].astype(BF16), wgu_bf[...]) + bgu_ref[0]
        g = jnp.minimum(gu[:, :f], SWIGLU_LIMIT)
        u = jnp.clip(gu[:, f:], -SWIGLU_LIMIT, SWIGLU_LIMIT)
        act = (u + 1.0) * (g * jax.nn.sigmoid(SWIGLU_ALPHA * g))
        o_ref[...] = _dot(act.astype(BF16), wd_bf[...]) + bd_ref[0]

    @pl.when(b >= nu_ref[0])
    def _():
        o_ref[...] = jnp.zeros_like(o_ref)


def _experts(buf, block_expert, n_used, layer, w_gate_up, b_gate_up, w_down, b_down):
    n_rows, d = buf.shape
    depth, n_exp, _, f2 = w_gate_up.shape
    f = w_down.shape[2]
    n_blocks = n_rows // EXPERT_ROWS
    block_expert = block_expert + layer * n_exp
    w_gate_up, w_down = w_gate_up.reshape(depth * n_exp, d, f2), w_down.reshape(depth * n_exp, f, d)
    n_exp = depth * n_exp
    grid_spec = pltpu.PrefetchScalarGridSpec(
        num_scalar_prefetch=2,
        grid=(n_blocks,),
        in_specs=[
            pl.BlockSpec((EXPERT_ROWS, d), lambda b, be, nu: (b, 0)),
            pl.BlockSpec((1, d, f2), lambda b, be, nu: (be[b], 0, 0)),
            pl.BlockSpec((1, 1, f2), lambda b, be, nu: (be[b], 0, 0)),
            pl.BlockSpec((1, f, d), lambda b, be, nu: (be[b], 0, 0)),
            pl.BlockSpec((1, 1, d), lambda b, be, nu: (be[b], 0, 0)),
        ],
        out_specs=pl.BlockSpec((EXPERT_ROWS, d), lambda b, be, nu: (b, 0)),
        scratch_shapes=[pltpu.VMEM((d, f2), BF16), pltpu.VMEM((f, d), BF16)],
    )
    return pl.pallas_call(
        _expert_kernel,
        grid_spec=grid_spec,
        out_shape=jax.ShapeDtypeStruct((n_rows, d), F32),
        compiler_params=_params(("arbitrary",)),
        name="moe_experts",
    )(block_expert, n_used, buf, w_gate_up, b_gate_up.reshape(n_exp, 1, f2), w_down, b_down.reshape(n_exp, 1, d))


def _combine_kernel(dest_ref, gate_ref, x_ref, nw_ref, g_ref, y_ref, xo_ref, rows_sc, sem):
    def row_copy(i, kk):
        row = dest_ref[i * TOP_K + kk]
        return pltpu.make_async_copy(y_ref.at[pl.ds(row, 1), :], rows_sc.at[kk, pl.ds(i, 1), :], sem)

    tm = x_ref.shape[0]
    _for_each_assignment(tm, lambda i, kk: row_copy(i, kk).start())
    for kk in range(TOP_K):
        pltpu.make_async_copy(y_ref.at[pl.ds(0, tm), :], rows_sc.at[kk], sem).wait()
    gates = gate_ref[...]
    y = gates[:, 0:1] * rows_sc[0]
    for kk in range(1, TOP_K):
        y = y + gates[:, kk:kk + 1] * rows_sc[kk]
    xo_ref[...] = x_ref[...] + g_ref[0] * _rms(y, nw_ref[...], NORM_EPS)


def _combine(y_grouped, dest_flat, gates, xs, norm_w, mod3, n_rows, tiles_per_batch, batch):
    d = xs.shape[1]
    row = lambda t: (t, 0)
    return pl.pallas_call(
        _combine_kernel,
        grid=(n_rows // TM,),
        in_specs=[pl.BlockSpec((TM * TOP_K,), lambda t: (t,), memory_space=pltpu.SMEM),
                  pl.BlockSpec((TM, LANES), row), pl.BlockSpec((TM, d), row),
                  pl.BlockSpec((1, d), lambda t: (0, 0)), _mod_spec(d, 5, tiles_per_batch, batch),
                  pl.BlockSpec(memory_space=pl.ANY)],
        out_specs=pl.BlockSpec((TM, d), row),
        out_shape=jax.ShapeDtypeStruct((n_rows, d), F32),
        scratch_shapes=[pltpu.VMEM((TOP_K, TM, d), F32), pltpu.SemaphoreType.DMA],
        compiler_params=_params(("arbitrary",), disable_bounds_checks=True),
        name="moe_combine",
    )(dest_flat, gates, xs, norm_w.reshape(1, d), mod3, y_grouped)


def _moe_layer(h, idx, gates, rank, cnt, xs, norm_post, mod3, layer, w_gate_up, b_gate_up, w_down, b_down,
               n_rows, tiles_per_batch, batch):
    n_exp = w_gate_up.shape[1]
    blk = EXPERT_ROWS
    counts = cnt[0, :n_exp].astype(jnp.int32)
    padded = (counts + blk - 1) // blk * blk
    pad_ends = jnp.cumsum(padded)
    pad_starts = pad_ends - padded
    dest = (pad_starts[idx[:, :TOP_K]] + rank[:, :TOP_K]).reshape(-1)
    n_blocks = (n_rows * TOP_K + n_exp * (blk - 1) + blk - 1) // blk
    block_start = jnp.arange(n_blocks, dtype=jnp.int32) * blk
    block_expert = jnp.sum(block_start[:, None] >= pad_ends[None, :], axis=1)
    block_expert = jnp.minimum(block_expert, n_exp - 1).astype(jnp.int32)
    n_used = (pad_ends[-1:] // blk).astype(jnp.int32)
    buf = _dispatch(h, dest, pad_ends.astype(jnp.int32), n_used, n_blocks * blk)
    y_grouped = _experts(buf, block_expert, n_used, layer, w_gate_up, b_gate_up, w_down, b_down)
    return _combine(y_grouped, dest, gates, xs, norm_post, mod3, n_rows, tiles_per_batch, batch)


def kernel(x, c, ctx, c_ctx, w_mod, b_mod, norm_mix_pre, norm_mix_post, norm_ffn_pre, norm_ffn_post,
           na_w_qkv, na_w_o, na_rpb,
           diff_w_qkv, diff_w_o, diff_lambda_q1, diff_lambda_k1, diff_lambda_q2, diff_lambda_k2, diff_subln,
           moe_w_router, moe_b_router, moe_w_gate_up, moe_b_gate_up, moe_w_down, moe_b_down):
    batch, seq, d = x.shape
    n_ctx = ctx.shape[1]
    depth = w_mod.shape[0]
    assert depth == 2 and seq % TM == 0 and n_ctx % TM == 0
    n_lat = batch * seq
    n_all = n_lat + batch * n_ctx
    tiles_per_batch = seq // TM
    xs = jnp.concatenate([x.reshape(n_lat, d), ctx.reshape(batch * n_ctx, d)], axis=0)
    mod = _adaln_mod(c, c_ctx, w_mod, b_mod).reshape(depth, SUBLANES, 1, 6 * d)

    na_dh = d // NA_HEADS
    q, k, v = _norm_qkv(xs, norm_mix_pre[0], mod[0], na_w_qkv[0].astype(BF16), tiles_per_batch, batch,
                        q_scale=na_dh ** -0.5 * LOG2_E)
    o = _na_attention(q, k, v, na_rpb[0], batch, seq, n_ctx)
    xs, h, idx, gates, rank, cnt = _proj_post(o, na_w_o[0].astype(BF16), xs, norm_mix_post[0], norm_ffn_pre[0],
                                              mod[0], moe_w_router[0], moe_b_router[0], n_all, tiles_per_batch, batch)
    xs = _moe_layer(h, idx, gates, rank, cnt, xs, norm_ffn_post[0], mod[0], 0, moe_w_gate_up, moe_b_gate_up,
                    moe_w_down, moe_b_down, n_all, tiles_per_batch, batch)

    diff_dh = d // DIFF_HEADS // 2
    lambda_init = 0.8 - 0.6 * math.exp(-0.3 * 1)
    cols = _pair_split_columns(2 * DIFF_HEADS, diff_dh)
    w1 = diff_w_qkv[0]
    w1 = jnp.concatenate([w1[:, :d][:, cols], w1[:, d:2 * d][:, cols], w1[:, 2 * d:]], axis=1).astype(BF16)
    tables = _rope_tables(seq, batch, batch * n_ctx, diff_dh)
    q, k, v = _norm_qkv(xs, norm_mix_pre[1], mod[1], w1, tiles_per_batch, batch,
                        q_scale=diff_dh ** -0.5 * LOG2_E, rope_tables=tables)
    o = _diff_attention(q, k, v, diff_lambda_q1[0], diff_lambda_k1[0], diff_lambda_q2[0], diff_lambda_k2[0],
                        diff_subln[0], lambda_init, batch, seq, n_ctx)
    xs, h, idx, gates, rank, cnt = _proj_post(o, diff_w_o[0].astype(BF16), xs, norm_mix_post[1], norm_ffn_pre[1],
                                              mod[1], moe_w_router[1], moe_b_router[1], n_lat, tiles_per_batch, batch)
    xs = _moe_layer(h, idx, gates, rank, cnt, xs, norm_ffn_post[1], mod[1], 1, moe_w_gate_up, moe_b_gate_up,
                    moe_w_down, moe_b_down, n_lat, tiles_per_batch, batch)
    return xs.reshape(batch, seq, d)
```

```python
import functools
import math

import jax
import jax.numpy as jnp
import numpy as np
from jax import lax
from jax.experimental import pallas as pl
from jax.experimental.pallas import tpu as pltpu

F32 = jnp.float32
BF16 = jnp.bfloat16
HIGHEST = lax.Precision.HIGHEST

GRID_W = 64
NA_HEADS = 16
NA_WIN_ROWS = 8
NA_WIN_COLS = 16
DIFF_HEADS = 8
DIFF_EPS = 1e-5
ROPE_BASE = 10000.0
N_EXPERTS = 32
TOP_K = 4
SWIGLU_LIMIT = 7.0
SWIGLU_ALPHA = 1.702
NORM_EPS = 1e-6
NEG_BIG = -1e30
LOG2_E = math.log2(math.e)

LANES = 128
SUBLANES = 8
VMEM_LIMIT = 56 * 1024 * 1024

TM = 256
NA_Q_ROWS = 4
NA_K_BLOCKS = 3
NA_PAIRS_PER_STEP = 4
DIFF_TQ = 4096
DIFF_TK = 2048
DIFF_STRIP = 1024
EXPERT_ROWS = 512
TM_POST = 512
POST_PARTS = 2


def _params(semantics, **kw):
    return pltpu.CompilerParams(dimension_semantics=semantics, vmem_limit_bytes=VMEM_LIMIT, **kw)


def _rms(x, w, eps):
    return x * lax.rsqrt(jnp.mean(x * x, axis=-1, keepdims=True) + eps) * w


def _dot(a, b):
    return jnp.dot(a, b, preferred_element_type=F32)


def _dot_nt(a, b):
    return lax.dot_general(a, b, (((1,), (1,)), ((), ())), preferred_element_type=F32)


def _mod_kernel(c_ref, w_ref, b_ref, o_ref):
    cv = c_ref[...]
    act = cv * jax.nn.sigmoid(cv)
    o_ref[0] = jnp.dot(act, w_ref[0], precision=HIGHEST, preferred_element_type=F32) + b_ref[0]


def _adaln_mod(c, c_ctx, w_mod, b_mod):
    depth, d, d6 = w_mod.shape
    batch = c.shape[0]
    assert batch + 1 <= SUBLANES
    cc = jnp.zeros((SUBLANES, d), F32).at[:batch].set(c).at[batch].set(c_ctx)
    tn = d6 // 4
    return pl.pallas_call(
        _mod_kernel,
        grid=(depth, d6 // tn),
        in_specs=[
            pl.BlockSpec((SUBLANES, d), lambda l, j: (0, 0)),
            pl.BlockSpec((1, d, tn), lambda l, j: (l, 0, j)),
            pl.BlockSpec((1, 1, tn), lambda l, j: (l, 0, j)),
        ],
        out_specs=pl.BlockSpec((1, SUBLANES, tn), lambda l, j: (l, 0, j)),
        out_shape=jax.ShapeDtypeStruct((depth, SUBLANES, d6), F32),
        compiler_params=_params(("arbitrary", "arbitrary")),
        name="adaln_mod",
    )(cc, w_mod, b_mod.reshape(depth, 1, d6))


def _mod_spec(d, chunk, tiles_per_batch, batch):
    return pl.BlockSpec((1, 1, d), lambda t: (jnp.minimum(t // tiles_per_batch, batch), 0, chunk))


def _norm_qkv_kernel(x_ref, nw_ref, sh_ref, sc_ref, w_ref, *rest, rope, q_scale):
    if rope:
        cos_ref, s1_ref, s2_ref, q_ref, k_ref, v_ref = rest
    else:
        q_ref, k_ref, v_ref = rest
    d = x_ref.shape[1]
    h = (_rms(x_ref[...], nw_ref[...], NORM_EPS) * (1.0 + sc_ref[0]) + sh_ref[0]).astype(BF16)
    q = _dot(h, w_ref[:, :d])
    k = _dot(h, w_ref[:, d:2 * d])
    v = _dot(h, w_ref[:, 2 * d:])
    if rope:
        cos, s1, s2 = cos_ref[...], s1_ref[...], s2_ref[...]
    for j in range(d // LANES):
        sl = slice(j * LANES, (j + 1) * LANES)
        v_ref[j, :, :LANES] = v[:, sl].astype(v_ref.dtype)
        if rope:
            v_ref[j, :, LANES:] = jnp.ones((v_ref.shape[1], LANES), v_ref.dtype)
            for src, dst, scale in ((q, q_ref, q_scale), (k, k_ref, 1.0)):
                xs = src[:, sl]
                rot = xs * cos + pltpu.roll(xs, LANES - 32, 1) * s1 + pltpu.roll(xs, 32, 1) * s2
                dst[j] = (rot * scale).astype(dst.dtype)
        else:
            q_ref[j] = (q[:, sl] * q_scale).astype(q_ref.dtype)
            k_ref[j] = k[:, sl].astype(k_ref.dtype)


def _norm_qkv(xs, norm_w, mod3, w_qkv, tiles_per_batch, batch, q_scale, rope_tables=None):
    n, d = xs.shape
    rope = rope_tables is not None
    row = lambda t: (t, 0)
    const = lambda t: (0, 0)
    in_specs = [
        pl.BlockSpec((TM, d), row),
        pl.BlockSpec((1, d), const),
        _mod_spec(d, 0, tiles_per_batch, batch),
        _mod_spec(d, 1, tiles_per_batch, batch),
        pl.BlockSpec((d, 3 * d), const),
    ]
    args = [xs, norm_w.reshape(1, d), mod3, mod3, w_qkv]
    if rope:
        in_specs += [pl.BlockSpec((TM, LANES), row)] * 3
        args += list(rope_tables)
    heads = d // LANES
    widths = (LANES, LANES, 2 * LANES if rope else LANES)
    out_specs = [pl.BlockSpec((heads, TM, w), lambda t: (0, t, 0)) for w in widths]
    out_shape = [jax.ShapeDtypeStruct((heads, n, w), BF16) for w in widths]
    return pl.pallas_call(
        functools.partial(_norm_qkv_kernel, rope=rope, q_scale=q_scale),
        grid=(n // TM,),
        in_specs=in_specs,
        out_specs=out_specs,
        out_shape=out_shape,
        compiler_params=_params(("arbitrary",)),
        name="norm_qkv_rope" if rope else "norm_qkv",
    )(*args)


def _rope_tables(seq, batch, n_ctx_rows, head_dim):
    t = jnp.arange(seq, dtype=jnp.int32)
    row = (t // GRID_W).astype(F32)
    col = (t % GRID_W).astype(F32)
    axis_dim = head_dim // 2
    inv_freq = ROPE_BASE ** (-jnp.arange(0, axis_dim, 2, dtype=F32) / axis_dim)
    ang = jnp.concatenate([row[:, None] * inv_freq, col[:, None] * inv_freq], axis=-1)
    cos, sin = jnp.cos(ang), jnp.sin(ang)
    zero = jnp.zeros_like(sin)
    reps = LANES // head_dim

    def lay(first, second, ctx_value):
        tab = jnp.tile(jnp.concatenate([first, second], axis=-1), (batch, reps))
        return jnp.concatenate([tab, jnp.full((n_ctx_rows, LANES), ctx_value, F32)], axis=0)

    return lay(cos, cos, 1.0), lay(-sin, zero, 0.0), lay(zero, sin, 0.0)


def _pair_split_columns(n_heads, head_dim):
    p = np.arange(head_dim)
    old = np.where(p < head_dim // 2, 2 * p, 2 * (p - head_dim // 2) + 1)
    return (np.arange(n_heads)[:, None] * head_dim + old[None, :]).reshape(-1)


def _head_masks():
    lane = lax.broadcasted_iota(jnp.int32, (1, LANES), 1)
    first = lane < (LANES // 2)
    return first, jnp.logical_not(first)


def _na_kernel(q_ref, k0_ref, k1_ref, k2_ref, v0_ref, v1_ref, v2_ref, kc_ref, vc_ref, bias_ref, o_ref):
    masks = _head_masks()
    k_refs = (k0_ref, k1_ref, k2_ref)
    n_pairs, kb = k0_ref.shape[0], k0_ref.shape[1]
    ones = jnp.ones((kb, LANES), BF16)

    def scores(pr, t):
        q2 = q_ref[pr]
        qm = jnp.where(masks[t], q2, jnp.zeros_like(q2))
        s = [_dot_nt(qm, kr[pr]) + bias_ref[0, 2 * pr + t, :, j * kb:(j + 1) * kb] for j, kr in enumerate(k_refs)]
        s.append(_dot_nt(qm, kc_ref[pr]))
        return s, functools.reduce(jnp.maximum, [jnp.max(x, axis=-1, keepdims=True) for x in s])

    def attend(pr, s, m):
        v_ext = [jnp.concatenate([vr[pr], ones], axis=1) for vr in (v0_ref, v1_ref, v2_ref, vc_ref)]
        m_wide = jnp.broadcast_to(m, (m.shape[0], kb))
        acc = functools.reduce(jnp.add, [_dot(jnp.exp2(x - m_wide).astype(BF16), v) for x, v in zip(s, v_ext)])
        return acc[:, :LANES] / acc[:, LANES:]

    units = [(pr, t) for pr in range(n_pairs) for t in range(2)]
    nxt = scores(*units[0])
    outs = []
    for u, (pr, t) in enumerate(units):
        cur = nxt
        if u + 1 < len(units):
            nxt = scores(*units[u + 1])
        outs.append(attend(pr, *cur))
        if t == 1:
            o_ref[:, pr * LANES:(pr + 1) * LANES] = jnp.where(masks[0], outs[-2], outs[-1]).astype(o_ref.dtype)


def _na_bias_tables(rpb):
    qn, kn = NA_Q_ROWS * GRID_W, NA_K_BLOCKS * NA_Q_ROWS * GRID_W
    qr, qc = np.arange(qn) // GRID_W, np.arange(qn) % GRID_W
    kr, kc = np.arange(kn) // GRID_W, np.arange(kn) % GRID_W
    c0 = np.clip(qc - NA_WIN_COLS // 2, 0, GRID_W - NA_WIN_COLS)
    col_ok = (kc[None, :] >= c0[:, None]) & (kc[None, :] < c0[:, None] + NA_WIN_COLS)
    n_dr, n_dc = 2 * NA_WIN_ROWS - 1, 2 * NA_WIN_COLS - 1
    last_start = NA_K_BLOCKS * NA_Q_ROWS - NA_WIN_ROWS
    q_rows, k_rows = np.arange(NA_Q_ROWS), np.arange(NA_K_BLOCKS * NA_Q_ROWS)
    sel_dr, oks = [], []
    for delta, w0 in ((0, np.zeros_like(qr)), (NA_Q_ROWS, qr), (last_start + NA_Q_ROWS, np.full_like(qr, last_start))):
        row_ok = (kr[None, :] >= w0[:, None]) & (kr[None, :] < w0[:, None] + NA_WIN_ROWS)
        dr = np.clip(k_rows[None, :] - q_rows[:, None] - delta + (NA_WIN_ROWS - 1), 0, n_dr - 1)
        sel_dr.append(dr[:, :, None] == np.arange(n_dr))
        oks.append(row_ok & col_ok)
    cols = np.arange(GRID_W)
    dc = np.clip(cols[None, :] - cols[:, None] + (NA_WIN_COLS - 1), 0, n_dc - 1)
    sel_dr = jnp.asarray(np.stack(sel_dr), F32)
    sel_dc = jnp.asarray(dc[:, :, None] == np.arange(n_dc), F32)
    by_row = jnp.einsum("vabr,hrc->vhabc", sel_dr, rpb, precision=HIGHEST)
    tabs = jnp.einsum("vhabc,xyc->vhaxby", by_row, sel_dc, precision=HIGHEST)
    tabs = tabs.reshape(len(oks), rpb.shape[0], qn, kn)
    tabs = jnp.where(jnp.asarray(np.stack(oks))[:, None], tabs * LOG2_E, NEG_BIG)
    return jnp.concatenate([tabs, jnp.full_like(tabs[:1], NEG_BIG)], axis=0).astype(F32)


def _na_attention(q, k, v, rpb, batch, seq, n_ctx):
    n_pairs, n, _ = q.shape
    d = n_pairs * LANES
    qb = NA_Q_ROWS * GRID_W
    nblk = seq // qb
    assert seq % qb == 0 and nblk >= NA_K_BLOCKS and n_ctx == qb and (batch * seq) % n_ctx == 0
    assert seq // GRID_W >= NA_WIN_ROWS and NA_K_BLOCKS * NA_Q_ROWS == NA_Q_ROWS + NA_WIN_ROWS
    bias = _na_bias_tables(rpb)
    ctx_blk0 = batch * seq // n_ctx

    pp = NA_PAIRS_PER_STEP
    assert n_pairs % pp == 0

    def kv_spec(j):
        return pl.BlockSpec((pp, qb, LANES),
                            lambda b, c, i: (c, b * nblk + jnp.clip(i - 1, 0, nblk - NA_K_BLOCKS) + j, 0))

    ctx_spec = pl.BlockSpec((pp, n_ctx, LANES), lambda b, c, i: (c, ctx_blk0 + b, 0))
    q_row = lambda b, i: jnp.where(i < nblk, b * nblk + i, ctx_blk0 + b)
    q_spec = pl.BlockSpec((pp, qb, LANES), lambda b, c, i: (c, q_row(b, i), 0))
    variant = lambda i: jnp.where(i == 0, 0, jnp.where(i < nblk - 1, 1, jnp.where(i == nblk - 1, 2, 3)))
    return pl.pallas_call(
        _na_kernel,
        grid=(batch, n_pairs // pp, nblk + 1),
        in_specs=[q_spec] + [kv_spec(j) for j in range(NA_K_BLOCKS)] * 2
        + [ctx_spec, ctx_spec,
           pl.BlockSpec((1, 2 * pp, qb, NA_K_BLOCKS * qb), lambda b, c, i: (variant(i), c, 0, 0))],
        out_specs=pl.BlockSpec((qb, pp * LANES), lambda b, c, i: (q_row(b, i), c)),
        out_shape=jax.ShapeDtypeStruct((n, d), BF16),
        compiler_params=_params(("arbitrary", "arbitrary", "arbitrary")),
        name="na_attention",
    )(q, k, k, k, v, v, v, k, v, bias)


def _diff_scores(qm, k):
    s = _dot_nt(qm, k)
    return s, jnp.max(s, axis=1, keepdims=True)


def _diff_accumulate(s, m_cur, v_ref, m_prev, acc_prev):
    m_new = jnp.broadcast_to(m_cur, (s.shape[0], LANES)) if m_prev is None else jnp.maximum(m_prev, m_cur)
    m_wide = jnp.concatenate([m_new, m_new], axis=1)
    acc = None
    for c in range(0, s.shape[1], 2 * LANES):
        p = jnp.exp2(s[:, c:c + 2 * LANES] - m_wide).astype(BF16)
        pv = _dot(p, v_ref[0, c:c + 2 * LANES, :])
        acc = pv if acc is None else acc + pv
    if m_prev is not None:
        alpha = jnp.exp2(m_prev - m_new)
        acc = jnp.concatenate([alpha, alpha], axis=1) * acc_prev + acc
    return m_new, acc


def _diff_kernel(q_ref, k_ref, v_ref, kc_ref, vc_ref, lq1_ref, lk1_ref, lq2_ref, lk2_ref, sub_ref,
                 o_ref, m_sc, acc_sc, *, lambda_init):
    j = pl.program_id(3)
    tq = q_ref.shape[1]
    units = [(t, r) for r in range(0, tq, DIFF_STRIP) for t in range(2)]
    masks = _head_masks()

    def q_strip(t, r):
        qs = q_ref[0, r:r + DIFF_STRIP, :]
        return jnp.where(masks[t], qs, jnp.zeros_like(qs))

    @pl.when(j == 0)
    def _():
        for t, r in units:
            m, acc = _diff_accumulate(*_diff_scores(q_strip(t, r), kc_ref[0]), vc_ref, None, None)
            m_sc[t, r:r + DIFF_STRIP, :] = m
            acc_sc[t, r:r + DIFF_STRIP, :] = acc

    nxt = _diff_scores(q_strip(*units[0]), k_ref[0])
    for u, (t, r) in enumerate(units):
        cur = nxt
        if u + 1 < len(units):
            nxt = _diff_scores(q_strip(*units[u + 1]), k_ref[0])
        rows = slice(r, r + DIFF_STRIP)
        m, acc = _diff_accumulate(*cur, v_ref, m_sc[t, rows, :], acc_sc[t, rows, :])
        m_sc[t, rows, :] = m
        acc_sc[t, rows, :] = acc

    @pl.when(j == pl.num_programs(3) - 1)
    def _():
        lam = (jnp.exp(jnp.sum(lq1_ref[...] * lk1_ref[...], axis=-1, keepdims=True))
               - jnp.exp(jnp.sum(lq2_ref[...] * lk2_ref[...], axis=-1, keepdims=True)) + lambda_init)
        a1, a2 = acc_sc[0], acc_sc[1]
        o = a1[:, :LANES] / a1[:, LANES:] - lam * (a2[:, :LANES] / a2[:, LANES:])
        o_ref[...] = (_rms(o, sub_ref[...], DIFF_EPS) * (1.0 - lambda_init)).astype(o_ref.dtype)


def _diff_attention(q, k, v, lq1, lk1, lq2, lk2, subln, lambda_init, batch, seq, n_ctx):
    n_heads, vw = v.shape[0], v.shape[2]
    d = n_heads * LANES
    tq, tk = min(DIFF_TQ, seq), min(DIFF_TK, seq)
    assert seq % tq == 0 and seq % tk == 0 and tq % DIFF_STRIP == 0 and (batch * seq) % n_ctx == 0
    assert subln.shape[-1] == LANES and vw == 2 * LANES
    nq, nk = seq // tq, seq // tk
    ctx_blk0 = batch * seq // n_ctx
    q_spec = pl.BlockSpec((1, tq, LANES), lambda b, h, i, j: (h, b * nq + i, 0))
    k_spec = pl.BlockSpec((1, tk, LANES), lambda b, h, i, j: (h, b * nk + j, 0))
    v_spec = pl.BlockSpec((1, tk, vw), lambda b, h, i, j: (h, b * nk + j, 0))
    kc_spec = pl.BlockSpec((1, n_ctx, LANES), lambda b, h, i, j: (h, ctx_blk0 + b, 0))
    vc_spec = pl.BlockSpec((1, n_ctx, vw), lambda b, h, i, j: (h, ctx_blk0 + b, 0))
    vec = lambda a: a.reshape(1, -1).astype(F32)
    vec_spec = lambda a: pl.BlockSpec((1, a.shape[-1]), lambda b, h, i, j: (0, 0))
    return pl.pallas_call(
        functools.partial(_diff_kernel, lambda_init=lambda_init),
        grid=(batch, n_heads, nq, nk),
        in_specs=[q_spec, k_spec, v_spec, kc_spec, vc_spec,
                  vec_spec(lq1), vec_spec(lk1), vec_spec(lq2), vec_spec(lk2), vec_spec(subln)],
        out_specs=pl.BlockSpec((tq, LANES), lambda b, h, i, j: (b * nq + i, h)),
        out_shape=jax.ShapeDtypeStruct((batch * seq, d), BF16),
        scratch_shapes=[pltpu.VMEM((2, tq, LANES), F32), pltpu.VMEM((2, tq, vw), F32)],
        compiler_params=_params(("arbitrary", "arbitrary", "arbitrary", "arbitrary")),
        name="diff_attention",
    )(q, k, v, k, v, vec(lq1), vec(lk1), vec(lq2), vec(lk2), vec(subln))


def _lane_pack(cols, shape, dtype):
    lane = lax.broadcasted_iota(jnp.int32, shape, 1)
    out = jnp.zeros(shape, dtype)
    for kk, col in enumerate(cols):
        out = jnp.where(lane == kk, col.astype(dtype), out)
    return out


def _proj_post_kernel(o_ref, wo_ref, x_ref, npost_ref, g_ref, npre_ref, sh_ref, sc_ref, wr_ref, br_ref,
                      xo_ref, h_ref, idx_ref, gate_ref, rank_ref, cnt_ref, cnt_sc):
    @pl.when(pl.program_id(0) == 0)
    def _():
        cnt_sc[...] = jnp.zeros_like(cnt_sc)

    tm = o_ref.shape[0]
    part = tm // POST_PARTS
    parts = [slice(p * part, (p + 1) * part) for p in range(POST_PARTS)]
    proj = [_dot(o_ref[rows, :], wo_ref[...]) for rows in parts]
    hs = []
    for rows, a in zip(parts, proj):
        xn = x_ref[rows, :] + g_ref[0] * _rms(a, npost_ref[...], NORM_EPS)
        xo_ref[rows, :] = xn
        h = _rms(xn, npre_ref[...], NORM_EPS) * (1.0 + sc_ref[0]) + sh_ref[0]
        h_ref[rows, :] = h
        hs.append(h)
    all_logits = [jnp.dot(h, wr_ref[...], precision=HIGHEST, preferred_element_type=F32) + br_ref[...] for h in hs]

    lane = lax.broadcasted_iota(jnp.int32, (part, LANES), 1).astype(F32)
    row = lax.broadcasted_iota(jnp.int32, (part, part), 0)
    colm = lax.broadcasted_iota(jnp.int32, (part, part), 1)
    earlier = (colm < row).astype(BF16)
    routed = []
    for logits in all_logits:
        vals, idxs, hits = [], [], []
        work = logits
        for _ in range(TOP_K):
            m = jnp.max(work, axis=-1, keepdims=True)
            idx = jnp.min(jnp.where(work == m, lane, float(LANES)), axis=-1, keepdims=True)
            hit = lane == idx
            work = jnp.where(hit, -jnp.inf, work)
            vals.append(m)
            idxs.append(idx)
            hits.append(hit)
        ex = [jnp.exp(vv - vals[0]) for vv in vals]
        den = functools.reduce(jnp.add, ex)
        member = functools.reduce(jnp.logical_or, hits)
        within = _dot(earlier, member.astype(BF16))
        routed.append((idxs, [e / den for e in ex], hits, member, within))

    counts = cnt_sc[...]
    for rows, (idxs, gates, hits, member, within) in zip(parts, routed):
        before = within + counts
        ranks = [jnp.sum(jnp.where(hit, before, 0.0), axis=-1, keepdims=True) for hit in hits]
        counts = counts + jnp.sum(member.astype(F32), axis=0, keepdims=True)
        idx_ref[rows, :] = _lane_pack(idxs, (part, LANES), jnp.int32)
        gate_ref[rows, :] = _lane_pack(gates, (part, LANES), F32)
        rank_ref[rows, :] = _lane_pack(ranks, (part, LANES), jnp.int32)
    cnt_sc[...] = counts
    cnt_ref[...] = jnp.broadcast_to(counts, cnt_ref.shape)


def _proj_post(o, w_o, xs, norm_post, norm_pre, mod3, w_router, b_router, n_rows, tiles_per_batch, batch):
    d = xs.shape[1]
    n_exp = w_router.shape[1]
    wr = jnp.zeros((d, LANES), F32).at[:, :n_exp].set(w_router)
    br = jnp.full((1, LANES), NEG_BIG, F32).at[0, :n_exp].set(b_router)
    row = lambda t: (t, 0)
    const = lambda t: (0, 0)
    tm = TM_POST
    assert n_rows % tm == 0 and (tiles_per_batch * TM) % tm == 0
    mspec = lambda chunk: _mod_spec(d, chunk, tiles_per_batch * TM // tm, batch)
    wide = lambda dt: jax.ShapeDtypeStruct((n_rows, LANES), dt)
    return pl.pallas_call(
        _proj_post_kernel,
        grid=(n_rows // tm,),
        in_specs=[
            pl.BlockSpec((tm, d), row), pl.BlockSpec((d, d), const), pl.BlockSpec((tm, d), row),
            pl.BlockSpec((1, d), const), mspec(2), pl.BlockSpec((1, d), const), mspec(3), mspec(4),
            pl.BlockSpec((d, LANES), const), pl.BlockSpec((1, LANES), const),
        ],
        out_specs=[pl.BlockSpec((tm, d), row), pl.BlockSpec((tm, d), row),
                   pl.BlockSpec((tm, LANES), row), pl.BlockSpec((tm, LANES), row), pl.BlockSpec((tm, LANES), row),
                   pl.BlockSpec((SUBLANES, LANES), const)],
        out_shape=[jax.ShapeDtypeStruct((n_rows, d), F32), jax.ShapeDtypeStruct((n_rows, d), F32),
                   wide(jnp.int32), wide(F32), wide(jnp.int32), jax.ShapeDtypeStruct((SUBLANES, LANES), F32)],
        scratch_shapes=[pltpu.VMEM((1, LANES), F32)],
        compiler_params=_params(("arbitrary",)),
        name="proj_post_router",
    )(o, w_o, xs, norm_post.reshape(1, d), mod3, norm_pre.reshape(1, d), mod3, mod3, wr, br)


def _for_each_assignment(n_tokens, fn):
    def body(g, carry):
        for u in range(SUBLANES):
            for kk in range(TOP_K):
                fn(g, u, kk, g * (SUBLANES * TOP_K) + (u * TOP_K + kk))
        return carry
    lax.fori_loop(0, n_tokens // SUBLANES, body, 0)


def _dispatch_kernel(ends_ref, nu_ref, dest_ref, h_ref, buf_ref, zero_sc, sem, zero_sem):
    tm = h_ref.shape[0] * h_ref.shape[1]
    blk = zero_sc.shape[0]

    @pl.when(pl.program_id(0) == 0)
    def _():
        zero_sc[...] = jnp.zeros_like(zero_sc)
        zero_copy = lambda row: pltpu.make_async_copy(zero_sc, buf_ref.at[pl.ds(row, blk), :], zero_sem)
        n_exp = ends_ref.shape[0]
        non_empty = [ends_ref[e] > (ends_ref[e - 1] if e else 0) for e in range(n_exp)]
        for action in ("start", "wait"):
            for e in range(n_exp):
                @pl.when(non_empty[e])
                def _():
                    getattr(zero_copy(pl.multiple_of(ends_ref[e] - blk, blk)), action)()

            def tail(b, carry):
                getattr(zero_copy(pl.multiple_of(b * blk, blk)), action)()
                return carry
            lax.fori_loop(nu_ref[0], buf_ref.shape[0] // blk, tail, 0)

    def row_copy(g, u, kk, flat):
        row = dest_ref[flat]
        return pltpu.make_async_copy(h_ref.at[g, pl.ds(u, 1), :], buf_ref.at[pl.ds(row, 1), :], sem)

    _for_each_assignment(tm, lambda *a: row_copy(*a).start())
    for _ in range(TOP_K):
        pltpu.make_async_copy(buf_ref.at[pl.ds(0, tm), :], buf_ref.at[pl.ds(0, tm), :], sem).wait()


def _dispatch(h, dest_flat, pad_ends, n_used, n_buf_rows):
    n, d = h.shape
    grid_spec = pltpu.PrefetchScalarGridSpec(
        num_scalar_prefetch=2,
        grid=(n // TM,),
        in_specs=[pl.BlockSpec((TM * TOP_K,), lambda t, ends, nu: (t,), memory_space=pltpu.SMEM),
                  pl.BlockSpec((TM // SUBLANES, SUBLANES, d), lambda t, ends, nu: (t, 0, 0))],
        out_specs=pl.BlockSpec(memory_space=pl.ANY),
        scratch_shapes=[pltpu.VMEM((EXPERT_ROWS, d), h.dtype), pltpu.SemaphoreType.DMA, pltpu.SemaphoreType.DMA],
    )
    return pl.pallas_call(
        _dispatch_kernel,
        grid_spec=grid_spec,
        out_shape=jax.ShapeDtypeStruct((n_buf_rows, d), h.dtype),
        compiler_params=_params(("arbitrary",), has_side_effects=True, disable_bounds_checks=True),
        name="moe_dispatch",
    )(pad_ends, n_used, dest_flat, h.reshape(n // SUBLANES, SUBLANES, d))


def _expert_kernel(be_ref, nu_ref, x_ref, wgu_ref, bgu_ref, wd_ref, bd_ref, o_ref, wgu_bf, wd_bf):
    b = pl.program_id(0)
    changed = jnp.logical_or(b == 0, be_ref[b] != be_ref[jnp.maximum(b - 1, 0)])

    @pl.when(changed)
    def _():
        wgu_bf[...] = wgu_ref[0].astype(BF16)
        wd_bf[...] = wd_ref[0].astype(BF16)

    @pl.when(b < nu_ref[0])
    def _():
        f = wd_bf.shape[0]
        gu = _dot(x_ref[...].astype(BF16), wgu_bf[...]) + bgu_ref[0]
        g = jnp.minimum(gu[:, :f], SWIGLU_LIMIT)
        u = jnp.clip(gu[:, f:], -SWIGLU_LIMIT, SWIGLU_LIMIT)
        act = (u + 1.0) * (g * jax.nn.sigmoid(SWIGLU_ALPHA * g))
        o_ref[...] = _dot(act.astype(BF16), wd_bf[...]) + bd_ref[0]

    @pl.when(b >= nu_ref[0])
    def _():
        o_ref[...] = jnp.zeros_like(o_ref)


def _experts(buf, block_expert, n_used, layer, w_gate_up, b_gate_up, w_down, b_down):
    n_rows, d = buf.shape
    depth, n_exp, _, f2 = w_gate_up.shape
    f = w_down.shape[2]
    n_blocks = n_rows // EXPERT_ROWS
    block_expert = block_expert + layer * n_exp
    w_gate_up, w_down = w_gate_up.reshape(depth * n_exp, d, f2), w_down.reshape(depth * n_exp, f, d)
    n_exp = depth * n_exp
    grid_spec = pltpu.PrefetchScalarGridSpec(
        num_scalar_prefetch=2,
        grid=(n_blocks,),
        in_specs=[
            pl.BlockSpec((EXPERT_ROWS, d), lambda b, be, nu: (b, 0)),
            pl.BlockSpec((1, d, f2), lambda b, be, nu: (be[b], 0, 0)),
            pl.BlockSpec((1, 1, f2), lambda b, be, nu: (be[b], 0, 0)),
            pl.BlockSpec((1, f, d), lambda b, be, nu: (be[b], 0, 0)),
            pl.BlockSpec((1, 1, d), lambda b, be, nu: (be[b], 0, 0)),
        ],
        out_specs=pl.BlockSpec((EXPERT_ROWS, d), lambda b, be, nu: (b, 0)),
        scratch_shapes=[pltpu.VMEM((d, f2), BF16), pltpu.VMEM((f, d), BF16)],
    )
    return pl.pallas_call(
        _expert_kernel,
        grid_spec=grid_spec,
        out_shape=jax.ShapeDtypeStruct((n_rows, d), F32),
        compiler_params=_params(("arbitrary",)),
        name="moe_experts",
    )(block_expert, n_used, buf, w_gate_up, b_gate_up.reshape(n_exp, 1, f2), w_down, b_down.reshape(n_exp, 1, d))


def _combine_kernel(dest_ref, gate_ref, x_ref, nw_ref, g_ref, y_ref, xo_ref, rows_sc, sem):
    def row_copy(g, u, kk, flat):
        row = dest_ref[flat]
        return pltpu.make_async_copy(y_ref.at[pl.ds(row, 1), :], rows_sc.at[kk, g, pl.ds(u, 1), :], sem)

    tm, d = x_ref.shape
    _for_each_assignment(tm, lambda *a: row_copy(*a).start())
    for _ in range(TOP_K):
        pltpu.make_async_copy(y_ref.at[pl.ds(0, tm), :], y_ref.at[pl.ds(0, tm), :], sem).wait()
    gates = gate_ref[...]
    y = gates[:, 0:1] * rows_sc[0].reshape(tm, d)
    for kk in range(1, TOP_K):
        y = y + gates[:, kk:kk + 1] * rows_sc[kk].reshape(tm, d)
    xo_ref[...] = x_ref[...] + g_ref[0] * _rms(y, nw_ref[...], NORM_EPS)


def _combine(y_grouped, dest_flat, gates, xs, norm_w, mod3, n_rows, tiles_per_batch, batch):
    d = xs.shape[1]
    row = lambda t: (t, 0)
    return pl.pallas_call(
        _combine_kernel,
        grid=(n_rows // TM,),
        in_specs=[pl.BlockSpec((TM * TOP_K,), lambda t: (t,), memory_space=pltpu.SMEM),
                  pl.BlockSpec((TM, LANES), row), pl.BlockSpec((TM, d), row),
                  pl.BlockSpec((1, d), lambda t: (0, 0)), _mod_spec(d, 5, tiles_per_batch, batch),
                  pl.BlockSpec(memory_space=pl.ANY)],
        out_specs=pl.BlockSpec((TM, d), row),
        out_shape=jax.ShapeDtypeStruct((n_rows, d), F32),
        scratch_shapes=[pltpu.VMEM((TOP_K, TM // SUBLANES, SUBLANES, d), F32), pltpu.SemaphoreType.DMA],
        compiler_params=_params(("arbitrary",), disable_bounds_checks=True),
        name="moe_combine",
    )(dest_flat, gates, xs, norm_w.reshape(1, d), mod3, y_grouped)


def _moe_layer(h, idx, gates, rank, cnt, xs, norm_post, mod3, layer, w_gate_up, b_gate_up, w_down, b_down,
               n_rows, tiles_per_batch, batch):
    n_exp = w_gate_up.shape[1]
    blk = EXPERT_ROWS
    counts = cnt[0, :n_exp].astype(jnp.int32)
    padded = (counts + blk - 1) // blk * blk
    pad_ends = jnp.cumsum(padded)
    pad_starts = pad_ends - padded
    dest = (pad_starts[idx[:, :TOP_K]] + rank[:, :TOP_K]).reshape(-1)
    n_blocks = (n_rows * TOP_K + n_exp * (blk - 1) + blk - 1) // blk
    block_start = jnp.arange(n_blocks, dtype=jnp.int32) * blk
    block_expert = jnp.sum(block_start[:, None] >= pad_ends[None, :], axis=1)
    block_expert = jnp.minimum(block_expert, n_exp - 1).astype(jnp.int32)
    n_used = (pad_ends[-1:] // blk).astype(jnp.int32)
    buf = _dispatch(h, dest, pad_ends.astype(jnp.int32), n_used, n_blocks * blk)
    y_grouped = _experts(buf, block_expert, n_used, layer, w_gate_up, b_gate_up, w_down, b_down)
    return _combine(y_grouped, dest, gates, xs, norm_post, mod3, n_rows, tiles_per_batch, batch)


def kernel(x, c, ctx, c_ctx, w_mod, b_mod, norm_mix_pre, norm_mix_post, norm_ffn_pre, norm_ffn_post,
           na_w_qkv, na_w_o, na_rpb,
           diff_w_qkv, diff_w_o, diff_lambda_q1, diff_lambda_k1, diff_lambda_q2, diff_lambda_k2, diff_subln,
           moe_w_router, moe_b_router, moe_w_gate_up, moe_b_gate_up, moe_w_down, moe_b_down):
    batch, seq, d = x.shape
    n_ctx = ctx.shape[1]
    depth = w_mod.shape[0]
    assert depth == 2 and seq % TM == 0 and n_ctx % TM == 0
    n_lat = batch * seq
    n_all = n_lat + batch * n_ctx
    tiles_per_batch = seq // TM
    xs = jnp.concatenate([x.reshape(n_lat, d), ctx.reshape(batch * n_ctx, d)], axis=0)
    mod = _adaln_mod(c, c_ctx, w_mod, b_mod).reshape(depth, SUBLANES, 1, 6 * d)

    na_dh = d // NA_HEADS
    q, k, v = _norm_qkv(xs, norm_mix_pre[0], mod[0], na_w_qkv[0].astype(BF16), tiles_per_batch, batch,
                        q_scale=na_dh ** -0.5 * LOG2_E)
    o = _na_attention(q, k, v, na_rpb[0], batch, seq, n_ctx)
    xs, h, idx, gates, rank, cnt = _proj_post(o, na_w_o[0].astype(BF16), xs, norm_mix_post[0], norm_ffn_pre[0],
                                              mod[0], moe_w_router[0], moe_b_router[0], n_all, tiles_per_batch, batch)
    xs = _moe_layer(h, idx, gates, rank, cnt, xs, norm_ffn_post[0], mod[0], 0, moe_w_gate_up, moe_b_gate_up,
                    moe_w_down, moe_b_down, n_all, tiles_per_batch, batch)

    diff_dh = d // DIFF_HEADS // 2
    lambda_init = 0.8 - 0.6 * math.exp(-0.3 * 1)
    cols = _pair_split_columns(2 * DIFF_HEADS, diff_dh)
    w1 = diff_w_qkv[0]
    w1 = jnp.concatenate([w1[:, :d][:, cols], w1[:, d:2 * d][:, cols], w1[:, 2 * d:]], axis=1).astype(BF16)
    tables = _rope_tables(seq, batch, batch * n_ctx, diff_dh)
    q, k, v = _norm_qkv(xs, norm_mix_pre[1], mod[1], w1, tiles_per_batch, batch,
                        q_scale=diff_dh ** -0.5 * LOG2_E, rope_tables=tables)
    o = _diff_attention(q, k, v, diff_lambda_q1[0], diff_lambda_k1[0], diff_lambda_q2[0], diff_lambda_k2[0],
                        diff_subln[0], lambda_init, batch, seq, n_ctx)
    xs, h, idx, gates, rank, cnt = _proj_post(o, diff_w_o[0].astype(BF16), xs, norm_mix_post[1], norm_ffn_pre[1],
                                              mod[1], moe_w_router[1], moe_b_router[1], n_lat, tiles_per_batch, batch)
    xs = _moe_layer(h, idx, gates, rank, cnt, xs, norm_ffn_post[1], mod[1], 1, moe_w_gate_up, moe_b_gate_up,
                    moe_w_down, moe_b_down, n_lat, tiles_per_batch, batch)
    return xs.reshape(batch, seq, d)
```

```python
import functools
import math

import jax
import jax.numpy as jnp
import numpy as np
from jax import lax
from jax.experimental import pallas as pl
from jax.experimental.pallas import tpu as pltpu

F32 = jnp.float32
BF16 = jnp.bfloat16
HIGHEST = lax.Precision.HIGHEST

GRID_W = 64
NA_HEADS = 16
NA_WIN_ROWS = 8
NA_WIN_COLS = 16
DIFF_HEADS = 8
DIFF_EPS = 1e-5
ROPE_BASE = 10000.0
N_EXPERTS = 32
TOP_K = 4
SWIGLU_LIMIT = 7.0
SWIGLU_ALPHA = 1.702
NORM_EPS = 1e-6
NEG_BIG = -1e30
LOG2_E = math.log2(math.e)

LANES = 128
SUBLANES = 8
VMEM_LIMIT = 56 * 1024 * 1024

TM = 512
NA_Q_ROWS = 4
NA_K_BLOCKS = 3
NA_PAIRS_PER_STEP = 8
DIFF_TQ = 4096
DIFF_TK = 2048
DIFF_STRIP = 1024
EXPERT_ROWS = 512
TM_POST = 512
POST_PARTS = 2


def _params(semantics, **kw):
    return pltpu.CompilerParams(dimension_semantics=semantics, vmem_limit_bytes=VMEM_LIMIT, **kw)


def _rms(x, w, eps):
    return x * lax.rsqrt(jnp.mean(x * x, axis=-1, keepdims=True) + eps) * w


def _dot(a, b):
    return jnp.dot(a, b, preferred_element_type=F32)


def _dot_nt(a, b):
    return lax.dot_general(a, b, (((1,), (1,)), ((), ())), preferred_element_type=F32)


def _mod_kernel(c_ref, w_ref, b_ref, o_ref):
    cv = c_ref[...]
    act = cv * jax.nn.sigmoid(cv)
    o_ref[0] = jnp.dot(act, w_ref[0], precision=HIGHEST, preferred_element_type=F32) + b_ref[0]


def _adaln_mod(c, c_ctx, w_mod, b_mod):
    depth, d, d6 = w_mod.shape
    batch = c.shape[0]
    assert batch + 1 <= SUBLANES
    cc = jnp.zeros((SUBLANES, d), F32).at[:batch].set(c).at[batch].set(c_ctx)
    tn = d6 // 4
    return pl.pallas_call(
        _mod_kernel,
        grid=(depth, d6 // tn),
        in_specs=[
            pl.BlockSpec((SUBLANES, d), lambda l, j: (0, 0)),
            pl.BlockSpec((1, d, tn), lambda l, j: (l, 0, j)),
            pl.BlockSpec((1, 1, tn), lambda l, j: (l, 0, j)),
        ],
        out_specs=pl.BlockSpec((1, SUBLANES, tn), lambda l, j: (l, 0, j)),
        out_shape=jax.ShapeDtypeStruct((depth, SUBLANES, d6), F32),
        compiler_params=_params(("arbitrary", "arbitrary")),
        name="adaln_mod",
    )(cc, w_mod, b_mod.reshape(depth, 1, d6))


def _mod_spec(d, chunk, tiles_per_batch, batch):
    return pl.BlockSpec((1, 1, d), lambda t: (jnp.minimum(t // tiles_per_batch, batch), 0, chunk))


def _norm_qkv_kernel(x_ref, nw_ref, sh_ref, sc_ref, w_ref, *rest, rope, q_scale):
    if rope:
        cos_ref, s1_ref, s2_ref, q_ref, k_ref, v_ref = rest
    else:
        q_ref, k_ref, v_ref = rest
    d = x_ref.shape[1]
    h = (_rms(x_ref[...], nw_ref[...], NORM_EPS) * (1.0 + sc_ref[0]) + sh_ref[0]).astype(BF16)
    q = _dot(h, w_ref[:, :d])
    k = _dot(h, w_ref[:, d:2 * d])
    v = _dot(h, w_ref[:, 2 * d:])
    if rope:
        cos, s1, s2 = cos_ref[...], s1_ref[...], s2_ref[...]
    for j in range(d // LANES):
        sl = slice(j * LANES, (j + 1) * LANES)
        v_ref[j, :, :LANES] = v[:, sl].astype(v_ref.dtype)
        if rope:
            v_ref[j, :, LANES:] = jnp.ones((v_ref.shape[1], LANES), v_ref.dtype)
            for src, dst, scale in ((q, q_ref, q_scale), (k, k_ref, 1.0)):
                xs = src[:, sl]
                rot = xs * cos + pltpu.roll(xs, LANES - 32, 1) * s1 + pltpu.roll(xs, 32, 1) * s2
                dst[j] = (rot * scale).astype(dst.dtype)
        else:
            q_ref[j] = (q[:, sl] * q_scale).astype(q_ref.dtype)
            k_ref[j] = k[:, sl].astype(k_ref.dtype)


def _norm_qkv(xs, norm_w, mod3, w_qkv, tiles_per_batch, batch, q_scale, rope_tables=None):
    n, d = xs.shape
    rope = rope_tables is not None
    row = lambda t: (t, 0)
    const = lambda t: (0, 0)
    in_specs = [
        pl.BlockSpec((TM, d), row),
        pl.BlockSpec((1, d), const),
        _mod_spec(d, 0, tiles_per_batch, batch),
        _mod_spec(d, 1, tiles_per_batch, batch),
        pl.BlockSpec((d, 3 * d), const),
    ]
    args = [xs, norm_w.reshape(1, d), mod3, mod3, w_qkv]
    if rope:
        in_specs += [pl.BlockSpec((TM, LANES), row)] * 3
        args += list(rope_tables)
    heads = d // LANES
    widths = (LANES, LANES, 2 * LANES if rope else LANES)
    out_specs = [pl.BlockSpec((heads, TM, w), lambda t: (0, t, 0)) for w in widths]
    out_shape = [jax.ShapeDtypeStruct((heads, n, w), BF16) for w in widths]
    return pl.pallas_call(
        functools.partial(_norm_qkv_kernel, rope=rope, q_scale=q_scale),
        grid=(n // TM,),
        in_specs=in_specs,
        out_specs=out_specs,
        out_shape=out_shape,
        compiler_params=_params(("arbitrary",)),
        name="norm_qkv_rope" if rope else "norm_qkv",
    )(*args)


def _rope_tables(seq, batch, n_ctx_rows, head_dim):
    t = jnp.arange(seq, dtype=jnp.int32)
    row = (t // GRID_W).astype(F32)
    col = (t % GRID_W).astype(F32)
    axis_dim = head_dim // 2
    inv_freq = ROPE_BASE ** (-jnp.arange(0, axis_dim, 2, dtype=F32) / axis_dim)
    ang = jnp.concatenate([row[:, None] * inv_freq, col[:, None] * inv_freq], axis=-1)
    cos, sin = jnp.cos(ang), jnp.sin(ang)
    zero = jnp.zeros_like(sin)
    reps = LANES // head_dim

    def lay(first, second, ctx_value):
        tab = jnp.tile(jnp.concatenate([first, second], axis=-1), (batch, reps))
        return jnp.concatenate([tab, jnp.full((n_ctx_rows, LANES), ctx_value, F32)], axis=0)

    return lay(cos, cos, 1.0), lay(-sin, zero, 0.0), lay(zero, sin, 0.0)


def _pair_split_columns(n_heads, head_dim):
    p = np.arange(head_dim)
    old = np.where(p < head_dim // 2, 2 * p, 2 * (p - head_dim // 2) + 1)
    return (np.arange(n_heads)[:, None] * head_dim + old[None, :]).reshape(-1)


def _head_masks():
    lane = lax.broadcasted_iota(jnp.int32, (1, LANES), 1)
    first = lane < (LANES // 2)
    return first, jnp.logical_not(first)


def _na_kernel(q_ref, k0_ref, k1_ref, k2_ref, v0_ref, v1_ref, v2_ref, kc_ref, vc_ref, bias_ref, o_ref):
    masks = _head_masks()
    k_refs = (k0_ref, k1_ref, k2_ref)
    n_pairs, kb = k0_ref.shape[0], k0_ref.shape[1]
    ones = jnp.ones((kb, LANES), BF16)

    def scores(pr, t):
        q2 = q_ref[pr]
        qm = jnp.where(masks[t], q2, jnp.zeros_like(q2))
        s = [_dot_nt(qm, kr[pr]) + bias_ref[0, 2 * pr + t, :, j * kb:(j + 1) * kb] for j, kr in enumerate(k_refs)]
        s.append(_dot_nt(qm, kc_ref[pr]))
        return s, functools.reduce(jnp.maximum, [jnp.max(x, axis=-1, keepdims=True) for x in s])

    def attend(pr, s, m):
        v_ext = [jnp.concatenate([vr[pr], ones], axis=1) for vr in (v0_ref, v1_ref, v2_ref, vc_ref)]
        m_wide = jnp.broadcast_to(m, (m.shape[0], kb))
        acc = functools.reduce(jnp.add, [_dot(jnp.exp2(x - m_wide).astype(BF16), v) for x, v in zip(s, v_ext)])
        return acc[:, :LANES] / acc[:, LANES:]

    units = [(pr, t) for pr in range(n_pairs) for t in range(2)]
    nxt = scores(*units[0])
    outs = []
    for u, (pr, t) in enumerate(units):
        cur = nxt
        if u + 1 < len(units):
            nxt = scores(*units[u + 1])
        outs.append(attend(pr, *cur))
        if t == 1:
            o_ref[:, pr * LANES:(pr + 1) * LANES] = jnp.where(masks[0], outs[-2], outs[-1]).astype(o_ref.dtype)


def _na_bias_tables(rpb):
    qn, kn = NA_Q_ROWS * GRID_W, NA_K_BLOCKS * NA_Q_ROWS * GRID_W
    qr, qc = np.arange(qn) // GRID_W, np.arange(qn) % GRID_W
    kr, kc = np.arange(kn) // GRID_W, np.arange(kn) % GRID_W
    c0 = np.clip(qc - NA_WIN_COLS // 2, 0, GRID_W - NA_WIN_COLS)
    col_ok = (kc[None, :] >= c0[:, None]) & (kc[None, :] < c0[:, None] + NA_WIN_COLS)
    n_dr, n_dc = 2 * NA_WIN_ROWS - 1, 2 * NA_WIN_COLS - 1
    last_start = NA_K_BLOCKS * NA_Q_ROWS - NA_WIN_ROWS
    q_rows, k_rows = np.arange(NA_Q_ROWS), np.arange(NA_K_BLOCKS * NA_Q_ROWS)
    sel_dr, oks = [], []
    for delta, w0 in ((0, np.zeros_like(qr)), (NA_Q_ROWS, qr), (last_start + NA_Q_ROWS, np.full_like(qr, last_start))):
        row_ok = (kr[None, :] >= w0[:, None]) & (kr[None, :] < w0[:, None] + NA_WIN_ROWS)
        dr = np.clip(k_rows[None, :] - q_rows[:, None] - delta + (NA_WIN_ROWS - 1), 0, n_dr - 1)
        sel_dr.append(dr[:, :, None] == np.arange(n_dr))
        oks.append(row_ok & col_ok)
    cols = np.arange(GRID_W)
    dc = np.clip(cols[None, :] - cols[:, None] + (NA_WIN_COLS - 1), 0, n_dc - 1)
    sel_dr = jnp.asarray(np.stack(sel_dr), F32)
    sel_dc = jnp.asarray(dc[:, :, None] == np.arange(n_dc), F32)
    by_row = jnp.einsum("vabr,hrc->vhabc", sel_dr, rpb * LOG2_E, precision=HIGHEST)
    tabs = jnp.einsum("vhabc,xyc->vhaxby", by_row, sel_dc, precision=HIGHEST)
    tabs = tabs.reshape(len(oks), rpb.shape[0], qn, kn)
    tabs = jnp.where(jnp.asarray(np.stack(oks))[:, None], tabs, NEG_BIG)
    return jnp.concatenate([tabs, jnp.full_like(tabs[:1], NEG_BIG)], axis=0).astype(F32)


def _na_attention(q, k, v, rpb, batch, seq, n_ctx):
    n_pairs, n, _ = q.shape
    d = n_pairs * LANES
    qb = NA_Q_ROWS * GRID_W
    nblk = seq // qb
    assert seq % qb == 0 and nblk >= NA_K_BLOCKS and n_ctx == qb and (batch * seq) % n_ctx == 0
    assert seq // GRID_W >= NA_WIN_ROWS and NA_K_BLOCKS * NA_Q_ROWS == NA_Q_ROWS + NA_WIN_ROWS
    bias = _na_bias_tables(rpb)
    ctx_blk0 = batch * seq // n_ctx

    pp = NA_PAIRS_PER_STEP
    assert n_pairs % pp == 0

    def kv_spec(j):
        return pl.BlockSpec((pp, qb, LANES),
                            lambda b, c, i: (c, b * nblk + jnp.clip(i - 1, 0, nblk - NA_K_BLOCKS) + j, 0))

    ctx_spec = pl.BlockSpec((pp, n_ctx, LANES), lambda b, c, i: (c, ctx_blk0 + b, 0))
    q_row = lambda b, i: jnp.where(i < nblk, b * nblk + i, ctx_blk0 + b)
    q_spec = pl.BlockSpec((pp, qb, LANES), lambda b, c, i: (c, q_row(b, i), 0))
    variant = lambda i: jnp.where(i == 0, 0, jnp.where(i < nblk - 1, 1, jnp.where(i == nblk - 1, 2, 3)))
    return pl.pallas_call(
        _na_kernel,
        grid=(batch, n_pairs // pp, nblk + 1),
        in_specs=[q_spec] + [kv_spec(j) for j in range(NA_K_BLOCKS)] * 2
        + [ctx_spec, ctx_spec,
           pl.BlockSpec((1, 2 * pp, qb, NA_K_BLOCKS * qb), lambda b, c, i: (variant(i), c, 0, 0))],
        out_specs=pl.BlockSpec((qb, pp * LANES), lambda b, c, i: (q_row(b, i), c)),
        out_shape=jax.ShapeDtypeStruct((n, d), BF16),
        compiler_params=_params(("arbitrary", "arbitrary", "arbitrary")),
        name="na_attention",
    )(q, k, k, k, v, v, v, k, v, bias)


def _diff_scores(qm, k):
    s = _dot_nt(qm, k)
    return s, jnp.max(s, axis=1, keepdims=True)


def _diff_accumulate(s, m_cur, v_ref, m_prev, acc_prev):
    m_new = jnp.broadcast_to(m_cur, (s.shape[0], LANES)) if m_prev is None else jnp.maximum(m_prev, m_cur)
    m_wide = jnp.concatenate([m_new, m_new], axis=1)
    acc = None
    for c in range(0, s.shape[1], 2 * LANES):
        p = jnp.exp2(s[:, c:c + 2 * LANES] - m_wide).astype(BF16)
        pv = _dot(p, v_ref[0, c:c + 2 * LANES, :])
        acc = pv if acc is None else acc + pv
    if m_prev is not None:
        alpha = jnp.exp2(m_prev - m_new)
        acc = jnp.concatenate([alpha, alpha], axis=1) * acc_prev + acc
    return m_new, acc


def _diff_kernel(q_ref, k_ref, v_ref, kc_ref, vc_ref, lq1_ref, lk1_ref, lq2_ref, lk2_ref, sub_ref,
                 o_ref, m_sc, acc_sc, *, lambda_init):
    j = pl.program_id(3)
    tq = q_ref.shape[1]
    units = [(t, r) for r in range(0, tq, DIFF_STRIP) for t in range(2)]
    masks = _head_masks()

    def q_strip(t, r):
        qs = q_ref[0, r:r + DIFF_STRIP, :]
        return jnp.where(masks[t], qs, jnp.zeros_like(qs))

    @pl.when(j == 0)
    def _():
        for t, r in units:
            m, acc = _diff_accumulate(*_diff_scores(q_strip(t, r), kc_ref[0]), vc_ref, None, None)
            m_sc[t, r:r + DIFF_STRIP, :] = m
            acc_sc[t, r:r + DIFF_STRIP, :] = acc

    nxt = _diff_scores(q_strip(*units[0]), k_ref[0])
    for u, (t, r) in enumerate(units):
        cur = nxt
        if u + 1 < len(units):
            nxt = _diff_scores(q_strip(*units[u + 1]), k_ref[0])
        rows = slice(r, r + DIFF_STRIP)
        m, acc = _diff_accumulate(*cur, v_ref, m_sc[t, rows, :], acc_sc[t, rows, :])
        m_sc[t, rows, :] = m
        acc_sc[t, rows, :] = acc

    @pl.when(j == pl.num_programs(3) - 1)
    def _():
        lam = (jnp.exp(jnp.sum(lq1_ref[...] * lk1_ref[...], axis=-1, keepdims=True))
               - jnp.exp(jnp.sum(lq2_ref[...] * lk2_ref[...], axis=-1, keepdims=True)) + lambda_init)
        a1, a2 = acc_sc[0], acc_sc[1]
        o = a1[:, :LANES] / a1[:, LANES:] - lam * (a2[:, :LANES] / a2[:, LANES:])
        o_ref[...] = (_rms(o, sub_ref[...], DIFF_EPS) * (1.0 - lambda_init)).astype(o_ref.dtype)


def _diff_attention(q, k, v, lq1, lk1, lq2, lk2, subln, lambda_init, batch, seq, n_ctx):
    n_heads, vw = v.shape[0], v.shape[2]
    d = n_heads * LANES
    tq, tk = min(DIFF_TQ, seq), min(DIFF_TK, seq)
    assert seq % tq == 0 and seq % tk == 0 and tq % DIFF_STRIP == 0 and (batch * seq) % n_ctx == 0
    assert subln.shape[-1] == LANES and vw == 2 * LANES
    nq, nk = seq // tq, seq // tk
    ctx_blk0 = batch * seq // n_ctx
    q_spec = pl.BlockSpec((1, tq, LANES), lambda b, h, i, j: (h, b * nq + i, 0))
    k_spec = pl.BlockSpec((1, tk, LANES), lambda b, h, i, j: (h, b * nk + j, 0))
    v_spec = pl.BlockSpec((1, tk, vw), lambda b, h, i, j: (h, b * nk + j, 0))
    kc_spec = pl.BlockSpec((1, n_ctx, LANES), lambda b, h, i, j: (h, ctx_blk0 + b, 0))
    vc_spec = pl.BlockSpec((1, n_ctx, vw), lambda b, h, i, j: (h, ctx_blk0 + b, 0))
    vec = lambda a: a.reshape(1, -1).astype(F32)
    vec_spec = lambda a: pl.BlockSpec((1, a.shape[-1]), lambda b, h, i, j: (0, 0))
    return pl.pallas_call(
        functools.partial(_diff_kernel, lambda_init=lambda_init),
        grid=(batch, n_heads, nq, nk),
        in_specs=[q_spec, k_spec, v_spec, kc_spec, vc_spec,
                  vec_spec(lq1), vec_spec(lk1), vec_spec(lq2), vec_spec(lk2), vec_spec(subln)],
        out_specs=pl.BlockSpec((tq, LANES), lambda b, h, i, j: (b * nq + i, h)),
        out_shape=jax.ShapeDtypeStruct((batch * seq, d), BF16),
        scratch_shapes=[pltpu.VMEM((2, tq, LANES), F32), pltpu.VMEM((2, tq, vw), F32)],
        compiler_params=_params(("arbitrary", "arbitrary", "arbitrary", "arbitrary")),
        name="diff_attention",
    )(q, k, v, k, v, vec(lq1), vec(lk1), vec(lq2), vec(lk2), vec(subln))


def _lane_pack(cols, shape, dtype):
    lane = lax.broadcasted_iota(jnp.int32, shape, 1)
    out = jnp.zeros(shape, dtype)
    for kk, col in enumerate(cols):
        out = jnp.where(lane == kk, col.astype(dtype), out)
    return out


def _proj_post_kernel(o_ref, wo_ref, x_ref, npost_ref, g_ref, npre_ref, sh_ref, sc_ref, wr_ref, br_ref,
                      xo_ref, h_ref, idx_ref, gate_ref, rank_ref, cnt_ref, cnt_sc):
    @pl.when(pl.program_id(0) == 0)
    def _():
        cnt_sc[...] = jnp.zeros_like(cnt_sc)

    tm = o_ref.shape[0]
    part = tm // POST_PARTS
    parts = [slice(p * part, (p + 1) * part) for p in range(POST_PARTS)]
    proj = [_dot(o_ref[rows, :], wo_ref[...]) for rows in parts]
    hs = []
    for rows, a in zip(parts, proj):
        xn = x_ref[rows, :] + g_ref[0] * _rms(a, npost_ref[...], NORM_EPS)
        xo_ref[rows, :] = xn
        h = _rms(xn, npre_ref[...], NORM_EPS) * (1.0 + sc_ref[0]) + sh_ref[0]
        h_ref[rows, :] = h
        hs.append(h)
    all_logits = [jnp.dot(h, wr_ref[...], precision=HIGHEST, preferred_element_type=F32) + br_ref[...] for h in hs]

    lane = lax.broadcasted_iota(jnp.int32, (part, LANES), 1).astype(F32)
    row = lax.broadcasted_iota(jnp.int32, (part, part), 0)
    colm = lax.broadcasted_iota(jnp.int32, (part, part), 1)
    earlier = (colm < row).astype(BF16)
    routed = []
    for logits in all_logits:
        vals, idxs, hits = [], [], []
        work = logits
        for _ in range(TOP_K):
            m = jnp.max(work, axis=-1, keepdims=True)
            idx = jnp.min(jnp.where(work == m, lane, float(LANES)), axis=-1, keepdims=True)
            hit = lane == idx
            work = jnp.where(hit, -jnp.inf, work)
            vals.append(m)
            idxs.append(idx)
            hits.append(hit)
        ex = [jnp.exp(vv - vals[0]) for vv in vals]
        den = functools.reduce(jnp.add, ex)
        member = functools.reduce(jnp.logical_or, hits)
        within = _dot(earlier, member.astype(BF16))
        routed.append((idxs, [e / den for e in ex], hits, member, within))

    counts = cnt_sc[...]
    for rows, (idxs, gates, hits, member, within) in zip(parts, routed):
        before = within + counts
        ranks = [jnp.sum(jnp.where(hit, before, 0.0), axis=-1, keepdims=True) for hit in hits]
        counts = counts + jnp.sum(member.astype(F32), axis=0, keepdims=True)
        idx_ref[rows, :] = _lane_pack(idxs, (part, LANES), jnp.int32)
        gate_ref[rows, :] = _lane_pack(gates, (part, LANES), F32)
        rank_ref[rows, :] = _lane_pack(ranks, (part, LANES), jnp.int32)
    cnt_sc[...] = counts
    cnt_ref[...] = jnp.broadcast_to(counts, cnt_ref.shape)


def _proj_post(o, w_o, xs, norm_post, norm_pre, mod3, w_router, b_router, n_rows, tiles_per_batch, batch):
    d = xs.shape[1]
    n_exp = w_router.shape[1]
    wr = jnp.zeros((d, LANES), F32).at[:, :n_exp].set(w_router)
    br = jnp.full((1, LANES), NEG_BIG, F32).at[0, :n_exp].set(b_router)
    row = lambda t: (t, 0)
    const = lambda t: (0, 0)
    tm = TM_POST
    assert n_rows % tm == 0 and (tiles_per_batch * TM) % tm == 0
    mspec = lambda chunk: _mod_spec(d, chunk, tiles_per_batch * TM // tm, batch)
    wide = lambda dt: jax.ShapeDtypeStruct((n_rows, LANES), dt)
    return pl.pallas_call(
        _proj_post_kernel,
        grid=(n_rows // tm,),
        in_specs=[
            pl.BlockSpec((tm, d), row), pl.BlockSpec((d, d), const), pl.BlockSpec((tm, d), row),
            pl.BlockSpec((1, d), const), mspec(2), pl.BlockSpec((1, d), const), mspec(3), mspec(4),
            pl.BlockSpec((d, LANES), const), pl.BlockSpec((1, LANES), const),
        ],
        out_specs=[pl.BlockSpec((tm, d), row), pl.BlockSpec((tm, d), row),
                   pl.BlockSpec((tm, LANES), row), pl.BlockSpec((tm, LANES), row), pl.BlockSpec((tm, LANES), row),
                   pl.BlockSpec((SUBLANES, LANES), const)],
        out_shape=[jax.ShapeDtypeStruct((n_rows, d), F32), jax.ShapeDtypeStruct((n_rows, d), F32),
                   wide(jnp.int32), wide(F32), wide(jnp.int32), jax.ShapeDtypeStruct((SUBLANES, LANES), F32)],
        scratch_shapes=[pltpu.VMEM((1, LANES), F32)],
        compiler_params=_params(("arbitrary",)),
        name="proj_post_router",
    )(o, w_o, xs, norm_post.reshape(1, d), mod3, norm_pre.reshape(1, d), mod3, mod3, wr, br)


def _for_each_assignment(n_tokens, fn):
    def body(g, carry):
        for u in range(SUBLANES):
            for kk in range(TOP_K):
                fn(g, u, kk, g * (SUBLANES * TOP_K) + (u * TOP_K + kk))
        return carry
    lax.fori_loop(0, n_tokens // SUBLANES, body, 0)


def _dispatch_kernel(ends_ref, nu_ref, dest_ref, h_ref, buf_ref, zero_sc, sem, zero_sem):
    tm = h_ref.shape[0] * h_ref.shape[1]
    blk = zero_sc.shape[0]

    @pl.when(pl.program_id(0) == 0)
    def _():
        zero_sc[...] = jnp.zeros_like(zero_sc)
        zero_copy = lambda row: pltpu.make_async_copy(zero_sc, buf_ref.at[pl.ds(row, blk), :], zero_sem)
        n_exp = ends_ref.shape[0]
        non_empty = [ends_ref[e] > (ends_ref[e - 1] if e else 0) for e in range(n_exp)]
        for action in ("start", "wait"):
            for e in range(n_exp):
                @pl.when(non_empty[e])
                def _():
                    getattr(zero_copy(pl.multiple_of(ends_ref[e] - blk, blk)), action)()

            def tail(b, carry):
                getattr(zero_copy(pl.multiple_of(b * blk, blk)), action)()
                return carry
            lax.fori_loop(nu_ref[0], buf_ref.shape[0] // blk, tail, 0)

    def row_copy(g, u, kk, flat):
        row = dest_ref[flat]
        return pltpu.make_async_copy(h_ref.at[g, pl.ds(u, 1), :], buf_ref.at[pl.ds(row, 1), :], sem)

    _for_each_assignment(tm, lambda *a: row_copy(*a).start())
    for _ in range(TOP_K):
        pltpu.make_async_copy(buf_ref.at[pl.ds(0, tm), :], buf_ref.at[pl.ds(0, tm), :], sem).wait()


def _dispatch(h, dest_flat, pad_ends, n_used, n_buf_rows):
    n, d = h.shape
    grid_spec = pltpu.PrefetchScalarGridSpec(
        num_scalar_prefetch=2,
        grid=(n // TM,),
        in_specs=[pl.BlockSpec((TM * TOP_K,), lambda t, ends, nu: (t,), memory_space=pltpu.SMEM),
                  pl.BlockSpec((TM // SUBLANES, SUBLANES, d), lambda t, ends, nu: (t, 0, 0))],
        out_specs=pl.BlockSpec(memory_space=pl.ANY),
        scratch_shapes=[pltpu.VMEM((EXPERT_ROWS, d), h.dtype), pltpu.SemaphoreType.DMA, pltpu.SemaphoreType.DMA],
    )
    return pl.pallas_call(
        _dispatch_kernel,
        grid_spec=grid_spec,
        out_shape=jax.ShapeDtypeStruct((n_buf_rows, d), h.dtype),
        compiler_params=_params(("arbitrary",), has_side_effects=True, disable_bounds_checks=True),
        name="moe_dispatch",
    )(pad_ends, n_used, dest_flat, h.reshape(n // SUBLANES, SUBLANES, d))


def _expert_kernel(be_ref, nu_ref, x_ref, wgu_ref, bgu_ref, wd_ref, bd_ref, o_ref, wgu_bf, wd_bf):
    b = pl.program_id(0)
    changed = jnp.logical_or(b == 0, be_ref[b] != be_ref[jnp.maximum(b - 1, 0)])

    @pl.when(changed)
    def _():
        wgu_bf[...] = wgu_ref[0].astype(BF16)
        wd_bf[...] = wd_ref[0].astype(BF16)

    @pl.when(b < nu_ref[0])
    def _():
        f = wd_bf.shape[0]
        gu = _dot(x_ref[...].astype(BF16), wgu_bf[...]) + bgu_ref[0]
        g = jnp.minimum(gu[:, :f], SWIGLU_LIMIT)
        u = jnp.clip(gu[:, f:], -SWIGLU_LIMIT, SWIGLU_LIMIT)
        act = (u + 1.0) * (g * jax.nn.sigmoid(SWIGLU_ALPHA * g))
        o_ref[...] = _dot(act.astype(BF16), wd_bf[...]) + bd_ref[0]

    @pl.when(b >= nu_ref[0])
    def _():
        o_ref[...] = jnp.zeros_like(o_ref)


def _experts(buf, block_expert, n_used, layer, w_gate_up, b_gate_up, w_down, b_down):
    n_rows, d = buf.shape
    depth, n_exp, _, f2 = w_gate_up.shape
    f = w_down.shape[2]
    n_blocks = n_rows // EXPERT_ROWS
    block_expert = block_expert + layer * n_exp
    w_gate_up, w_down = w_gate_up.reshape(depth * n_exp, d, f2), w_down.reshape(depth * n_exp, f, d)
    n_exp = depth * n_exp
    grid_spec = pltpu.PrefetchScalarGridSpec(
        num_scalar_prefetch=2,
        grid=(n_blocks,),
        in_specs=[
            pl.BlockSpec((EXPERT_ROWS, d), lambda b, be, nu: (b, 0)),
            pl.BlockSpec((1, d, f2), lambda b, be, nu: (be[b], 0, 0)),
            pl.BlockSpec((1, 1, f2), lambda b, be, nu: (be[b], 0, 0)),
            pl.BlockSpec((1, f, d), lambda b, be, nu: (be[b], 0, 0)),
            pl.BlockSpec((1, 1, d), lambda b, be, nu: (be[b], 0, 0)),
        ],
        out_specs=pl.BlockSpec((EXPERT_ROWS, d), lambda b, be, nu: (b, 0)),
        scratch_shapes=[pltpu.VMEM((d, f2), BF16), pltpu.VMEM((f, d), BF16)],
    )
    return pl.pallas_call(
        _expert_kernel,
        grid_spec=grid_spec,
        out_shape=jax.ShapeDtypeStruct((n_rows, d), F32),
        compiler_params=_params(("arbitrary",)),
        name="moe_experts",
    )(block_expert, n_used, buf, w_gate_up, b_gate_up.reshape(n_exp, 1, f2), w_down, b_down.reshape(n_exp, 1, d))


def _combine_kernel(dest_ref, gate_ref, x_ref, nw_ref, g_ref, y_ref, xo_ref, rows_sc, sem):
    def row_copy(g, u, kk, flat):
        row = dest_ref[flat]
        return pltpu.make_async_copy(y_ref.at[pl.ds(row, 1), :], rows_sc.at[kk, g, pl.ds(u, 1), :], sem)

    tm, d = x_ref.shape
    _for_each_assignment(tm, lambda *a: row_copy(*a).start())
    for _ in range(TOP_K):
        pltpu.make_async_copy(y_ref.at[pl.ds(0, tm), :], y_ref.at[pl.ds(0, tm), :], sem).wait()
    gates = gate_ref[...]
    y = gates[:, 0:1] * rows_sc[0].reshape(tm, d)
    for kk in range(1, TOP_K):
        y = y + gates[:, kk:kk + 1] * rows_sc[kk].reshape(tm, d)
    xo_ref[...] = x_ref[...] + g_ref[0] * _rms(y, nw_ref[...], NORM_EPS)


def _combine(y_grouped, dest_flat, gates, xs, norm_w, mod3, n_rows, tiles_per_batch, batch):
    d = xs.shape[1]
    row = lambda t: (t, 0)
    return pl.pallas_call(
        _combine_kernel,
        grid=(n_rows // TM,),
        in_specs=[pl.BlockSpec((TM * TOP_K,), lambda t: (t,), memory_space=pltpu.SMEM),
                  pl.BlockSpec((TM, LANES), row), pl.BlockSpec((TM, d), row),
                  pl.BlockSpec((1, d), lambda t: (0, 0)), _mod_spec(d, 5, tiles_per_batch, batch),
                  pl.BlockSpec(memory_space=pl.ANY)],
        out_specs=pl.BlockSpec((TM, d), row),
        out_shape=jax.ShapeDtypeStruct((n_rows, d), F32),
        scratch_shapes=[pltpu.VMEM((TOP_K, TM // SUBLANES, SUBLANES, d), F32), pltpu.SemaphoreType.DMA],
        compiler_params=_params(("arbitrary",), disable_bounds_checks=True),
        name="moe_combine",
    )(dest_flat, gates, xs, norm_w.reshape(1, d), mod3, y_grouped)


def _moe_layer(h, idx, gates, rank, cnt, xs, norm_post, mod3, layer, w_gate_up, b_gate_up, w_down, b_down,
               n_rows, tiles_per_batch, batch):
    n_exp = w_gate_up.shape[1]
    blk = EXPERT_ROWS
    counts = cnt[0, :n_exp].astype(jnp.int32)
    padded = (counts + blk - 1) // blk * blk
    pad_ends = jnp.cumsum(padded)
    pad_starts = pad_ends - padded
    dest = (pad_starts[idx[:, :TOP_K]] + rank[:, :TOP_K]).reshape(-1)
    n_blocks = (n_rows * TOP_K + n_exp * (blk - 1) + blk - 1) // blk
    block_start = jnp.arange(n_blocks, dtype=jnp.int32) * blk
    block_expert = jnp.sum(block_start[:, None] >= pad_ends[None, :], axis=1)
    block_expert = jnp.minimum(block_expert, n_exp - 1).astype(jnp.int32)
    n_used = (pad_ends[-1:] // blk).astype(jnp.int32)
    buf = _dispatch(h, dest, pad_ends.astype(jnp.int32), n_used, n_blocks * blk)
    y_grouped = _experts(buf, block_expert, n_used, layer, w_gate_up, b_gate_up, w_down, b_down)
    return _combine(y_grouped, dest, gates, xs, norm_post, mod3, n_rows, tiles_per_batch, batch)


def kernel(x, c, ctx, c_ctx, w_mod, b_mod, norm_mix_pre, norm_mix_post, norm_ffn_pre, norm_ffn_post,
           na_w_qkv, na_w_o, na_rpb,
           diff_w_qkv, diff_w_o, diff_lambda_q1, diff_lambda_k1, diff_lambda_q2, diff_lambda_k2, diff_subln,
           moe_w_router, moe_b_router, moe_w_gate_up, moe_b_gate_up, moe_w_down, moe_b_down):
    batch, seq, d = x.shape
    n_ctx = ctx.shape[1]
    depth = w_mod.shape[0]
    assert depth == 2 and seq % TM == 0 and (batch * n_ctx) % TM == 0
    n_lat = batch * seq
    n_all = n_lat + batch * n_ctx
    tiles_per_batch = seq // TM
    xs = jnp.concatenate([x.reshape(n_lat, d), ctx.reshape(batch * n_ctx, d)], axis=0)
    mod = _adaln_mod(c, c_ctx, w_mod, b_mod).reshape(depth, SUBLANES, 1, 6 * d)

    na_dh = d // NA_HEADS
    q, k, v = _norm_qkv(xs, norm_mix_pre[0], mod[0], na_w_qkv[0].astype(BF16), tiles_per_batch, batch,
                        q_scale=na_dh ** -0.5 * LOG2_E)
    o = _na_attention(q, k, v, na_rpb[0], batch, seq, n_ctx)
    xs, h, idx, gates, rank, cnt = _proj_post(o, na_w_o[0].astype(BF16), xs, norm_mix_post[0], norm_ffn_pre[0],
                                              mod[0], moe_w_router[0], moe_b_router[0], n_all, tiles_per_batch, batch)
    xs = _moe_layer(h, idx, gates, rank, cnt, xs, norm_ffn_post[0], mod[0], 0, moe_w_gate_up, moe_b_gate_up,
                    moe_w_down, moe_b_down, n_all, tiles_per_batch, batch)

    diff_dh = d // DIFF_HEADS // 2
    lambda_init = 0.8 - 0.6 * math.exp(-0.3 * 1)
    cols = _pair_split_columns(2 * DIFF_HEADS, diff_dh)
    w1 = diff_w_qkv[0]
    w1 = jnp.concatenate([w1[:, :d][:, cols], w1[:, d:2 * d][:, cols], w1[:, 2 * d:]], axis=1).astype(BF16)
    tables = _rope_tables(seq, batch, batch * n_ctx, diff_dh)
    q, k, v = _norm_qkv(xs, norm_mix_pre[1], mod[1], w1, tiles_per_batch, batch,
                        q_scale=diff_dh ** -0.5 * LOG2_E, rope_tables=tables)
    o = _diff_attention(q, k, v, diff_lambda_q1[0], diff_lambda_k1[0], diff_lambda_q2[0], diff_lambda_k2[0],
                        diff_subln[0], lambda_init, batch, seq, n_ctx)
    xs, h, idx, gates, rank, cnt = _proj_post(o, diff_w_o[0].astype(BF16), xs, norm_mix_post[1], norm_ffn_pre[1],
                                              mod[1], moe_w_router[1], moe_b_router[1], n_lat, tiles_per_batch, batch)
    xs = _moe_layer(h, idx, gates, rank, cnt, xs, norm_ffn_post[1], mod[1], 1, moe_w_gate_up, moe_b_gate_up,
                    moe_w_down, moe_b_down, n_lat, tiles_per_batch, batch)
    return xs.reshape(batch, seq, d)
```

```python
import functools
import math

import jax
import jax.numpy as jnp
import numpy as np
from jax import lax
from jax.experimental import pallas as pl
from jax.experimental.pallas import tpu as pltpu

F32 = jnp.float32
BF16 = jnp.bfloat16
HIGHEST = lax.Precision.HIGHEST

GRID_W = 64
NA_HEADS = 16
NA_WIN_ROWS = 8
NA_WIN_COLS = 16
DIFF_HEADS = 8
DIFF_EPS = 1e-5
ROPE_BASE = 10000.0
N_EXPERTS = 32
TOP_K = 4
SWIGLU_LIMIT = 7.0
SWIGLU_ALPHA = 1.702
NORM_EPS = 1e-6
NEG_BIG = -1e30
LOG2_E = math.log2(math.e)

LANES = 128
SUBLANES = 8
VMEM_LIMIT = 56 * 1024 * 1024

TM = 512
NA_Q_ROWS = 4
NA_K_BLOCKS = 3
NA_PAIRS_PER_STEP = 8
DIFF_TQ = 4096
DIFF_TK = 2048
DIFF_STRIP = 1024
EXPERT_ROWS = 512
TM_POST = 512
POST_PARTS = 2


def _params(semantics, **kw):
    return pltpu.CompilerParams(dimension_semantics=semantics, vmem_limit_bytes=VMEM_LIMIT, **kw)


def _rms(x, w, eps):
    return x * lax.rsqrt(jnp.mean(x * x, axis=-1, keepdims=True) + eps) * w


def _dot(a, b):
    return jnp.dot(a, b, preferred_element_type=F32)


def _dot_nt(a, b):
    return lax.dot_general(a, b, (((1,), (1,)), ((), ())), preferred_element_type=F32)


def _mod_kernel(c_ref, w_ref, b_ref, o_ref):
    cv = c_ref[...]
    act = cv * jax.nn.sigmoid(cv)
    o_ref[0] = jnp.dot(act, w_ref[0], precision=HIGHEST, preferred_element_type=F32) + b_ref[0]


def _adaln_mod(c, c_ctx, w_mod, b_mod):
    depth, d, d6 = w_mod.shape
    batch = c.shape[0]
    assert batch + 1 <= SUBLANES
    cc = jnp.zeros((SUBLANES, d), F32).at[:batch].set(c).at[batch].set(c_ctx)
    tn = d6 // 4
    return pl.pallas_call(
        _mod_kernel,
        grid=(depth, d6 // tn),
        in_specs=[
            pl.BlockSpec((SUBLANES, d), lambda l, j: (0, 0)),
            pl.BlockSpec((1, d, tn), lambda l, j: (l, 0, j)),
            pl.BlockSpec((1, 1, tn), lambda l, j: (l, 0, j)),
        ],
        out_specs=pl.BlockSpec((1, SUBLANES, tn), lambda l, j: (l, 0, j)),
        out_shape=jax.ShapeDtypeStruct((depth, SUBLANES, d6), F32),
        compiler_params=_params(("arbitrary", "arbitrary")),
        name="adaln_mod",
    )(cc, w_mod, b_mod.reshape(depth, 1, d6))


def _mod_spec(d, chunk, tiles_per_batch, batch):
    return pl.BlockSpec((1, 1, d), lambda t: (jnp.minimum(t // tiles_per_batch, batch), 0, chunk))


def _norm_qkv_kernel(x_ref, nw_ref, sh_ref, sc_ref, w_ref, *rest, rope, q_scale):
    if rope:
        cos_ref, s1_ref, s2_ref, q_ref, k_ref, v_ref = rest
    else:
        q_ref, k_ref, v_ref = rest
    d = x_ref.shape[1]
    h = (_rms(x_ref[...], nw_ref[...], NORM_EPS) * (1.0 + sc_ref[0]) + sh_ref[0]).astype(BF16)
    q = _dot(h, w_ref[:, :d])
    k = _dot(h, w_ref[:, d:2 * d])
    v = _dot(h, w_ref[:, 2 * d:])
    if rope:
        cos, s1, s2 = cos_ref[...], s1_ref[...], s2_ref[...]
    for j in range(d // LANES):
        sl = slice(j * LANES, (j + 1) * LANES)
        v_ref[j, :, :LANES] = v[:, sl].astype(v_ref.dtype)
        if rope:
            v_ref[j, :, LANES:] = jnp.ones((v_ref.shape[1], LANES), v_ref.dtype)
            for src, dst, scale in ((q, q_ref, q_scale), (k, k_ref, 1.0)):
                xs = src[:, sl]
                rot = xs * cos + pltpu.roll(xs, LANES - 32, 1) * s1 + pltpu.roll(xs, 32, 1) * s2
                dst[j] = (rot * scale).astype(dst.dtype)
        else:
            q_ref[j] = (q[:, sl] * q_scale).astype(q_ref.dtype)
            k_ref[j] = k[:, sl].astype(k_ref.dtype)


def _norm_qkv(xs, norm_w, mod3, w_qkv, tiles_per_batch, batch, q_scale, rope_tables=None):
    n, d = xs.shape
    rope = rope_tables is not None
    row = lambda t: (t, 0)
    const = lambda t: (0, 0)
    in_specs = [
        pl.BlockSpec((TM, d), row),
        pl.BlockSpec((1, d), const),
        _mod_spec(d, 0, tiles_per_batch, batch),
        _mod_spec(d, 1, tiles_per_batch, batch),
        pl.BlockSpec((d, 3 * d), const),
    ]
    args = [xs, norm_w.reshape(1, d), mod3, mod3, w_qkv]
    if rope:
        in_specs += [pl.BlockSpec((TM, LANES), row)] * 3
        args += list(rope_tables)
    heads = d // LANES
    widths = (LANES, LANES, 2 * LANES if rope else LANES)
    out_specs = [pl.BlockSpec((heads, TM, w), lambda t: (0, t, 0)) for w in widths]
    out_shape = [jax.ShapeDtypeStruct((heads, n, w), BF16) for w in widths]
    return pl.pallas_call(
        functools.partial(_norm_qkv_kernel, rope=rope, q_scale=q_scale),
        grid=(n // TM,),
        in_specs=in_specs,
        out_specs=out_specs,
        out_shape=out_shape,
        compiler_params=_params(("arbitrary",)),
        name="norm_qkv_rope" if rope else "norm_qkv",
    )(*args)


def _rope_tables(seq, batch, n_ctx_rows, head_dim):
    t = jnp.arange(seq, dtype=jnp.int32)
    row = (t // GRID_W).astype(F32)
    col = (t % GRID_W).astype(F32)
    axis_dim = head_dim // 2
    inv_freq = ROPE_BASE ** (-jnp.arange(0, axis_dim, 2, dtype=F32) / axis_dim)
    ang = jnp.concatenate([row[:, None] * inv_freq, col[:, None] * inv_freq], axis=-1)
    cos, sin = jnp.cos(ang), jnp.sin(ang)
    zero = jnp.zeros_like(sin)
    reps = LANES // head_dim

    def lay(first, second, ctx_value):
        tab = jnp.tile(jnp.concatenate([first, second], axis=-1), (batch, reps))
        return jnp.concatenate([tab, jnp.full((n_ctx_rows, LANES), ctx_value, F32)], axis=0)

    return lay(cos, cos, 1.0), lay(-sin, zero, 0.0), lay(zero, sin, 0.0)


def _pair_split_columns(n_heads, head_dim):
    p = np.arange(head_dim)
    old = np.where(p < head_dim // 2, 2 * p, 2 * (p - head_dim // 2) + 1)
    return (np.arange(n_heads)[:, None] * head_dim + old[None, :]).reshape(-1)


def _head_masks():
    lane = lax.broadcasted_iota(jnp.int32, (1, LANES), 1)
    first = lane < (LANES // 2)
    return first, jnp.logical_not(first)


def _na_kernel(q_ref, k0_ref, k1_ref, k2_ref, v0_ref, v1_ref, v2_ref, kc_ref, vc_ref, bias_ref, o_ref):
    masks = _head_masks()
    k_refs = (k0_ref, k1_ref, k2_ref)
    n_pairs, kb = k0_ref.shape[0], k0_ref.shape[1]
    ones = jnp.ones((kb, LANES), BF16)

    def scores(pr, t):
        q2 = q_ref[pr]
        qm = jnp.where(masks[t], q2, jnp.zeros_like(q2))
        s = [_dot_nt(qm, kr[pr]) + bias_ref[0, 2 * pr + t, :, j * kb:(j + 1) * kb] for j, kr in enumerate(k_refs)]
        s.append(_dot_nt(qm, kc_ref[pr]))
        return s, functools.reduce(jnp.maximum, [jnp.max(x, axis=-1, keepdims=True) for x in s])

    def attend(pr, s, m):
        v_ext = [jnp.concatenate([vr[pr], ones], axis=1) for vr in (v0_ref, v1_ref, v2_ref, vc_ref)]
        m_wide = jnp.broadcast_to(m, (m.shape[0], kb))
        acc = functools.reduce(jnp.add, [_dot(jnp.exp2(x - m_wide).astype(BF16), v) for x, v in zip(s, v_ext)])
        return acc[:, :LANES] / acc[:, LANES:]

    units = [(pr, t) for pr in range(n_pairs) for t in range(2)]
    nxt = scores(*units[0])
    outs = []
    for u, (pr, t) in enumerate(units):
        cur = nxt
        if u + 1 < len(units):
            nxt = scores(*units[u + 1])
        outs.append(attend(pr, *cur))
        if t == 1:
            o_ref[:, pr * LANES:(pr + 1) * LANES] = jnp.where(masks[0], outs[-2], outs[-1]).astype(o_ref.dtype)


def _na_bias_tables(rpb):
    qn, kn = NA_Q_ROWS * GRID_W, NA_K_BLOCKS * NA_Q_ROWS * GRID_W
    qr, qc = np.arange(qn) // GRID_W, np.arange(qn) % GRID_W
    kr, kc = np.arange(kn) // GRID_W, np.arange(kn) % GRID_W
    c0 = np.clip(qc - NA_WIN_COLS // 2, 0, GRID_W - NA_WIN_COLS)
    col_ok = (kc[None, :] >= c0[:, None]) & (kc[None, :] < c0[:, None] + NA_WIN_COLS)
    n_dr, n_dc = 2 * NA_WIN_ROWS - 1, 2 * NA_WIN_COLS - 1
    last_start = NA_K_BLOCKS * NA_Q_ROWS - NA_WIN_ROWS
    q_rows, k_rows = np.arange(NA_Q_ROWS), np.arange(NA_K_BLOCKS * NA_Q_ROWS)
    sel_dr, oks = [], []
    for delta, w0 in ((0, np.zeros_like(qr)), (NA_Q_ROWS, qr), (last_start + NA_Q_ROWS, np.full_like(qr, last_start))):
        row_ok = (kr[None, :] >= w0[:, None]) & (kr[None, :] < w0[:, None] + NA_WIN_ROWS)
        dr = np.clip(k_rows[None, :] - q_rows[:, None] - delta + (NA_WIN_ROWS - 1), 0, n_dr - 1)
        sel_dr.append(dr[:, :, None] == np.arange(n_dr))
        oks.append(row_ok & col_ok)
    cols = np.arange(GRID_W)
    dc = np.clip(cols[None, :] - cols[:, None] + (NA_WIN_COLS - 1), 0, n_dc - 1)
    sel_dr = jnp.asarray(np.stack(sel_dr), F32)
    sel_dc = jnp.asarray(dc[:, :, None] == np.arange(n_dc), F32)
    by_row = jnp.einsum("vabr,hrc->vhabc", sel_dr, rpb * LOG2_E, precision=HIGHEST)
    tabs = jnp.einsum("vhabc,xyc->vhaxby", by_row, sel_dc, precision=HIGHEST)
    tabs = tabs.reshape(len(oks), rpb.shape[0], qn, kn)
    tabs = jnp.where(jnp.asarray(np.stack(oks))[:, None], tabs, NEG_BIG)
    return jnp.concatenate([tabs, jnp.full_like(tabs[:1], NEG_BIG)], axis=0).astype(F32)


def _na_attention(q, k, v, rpb, batch, seq, n_ctx):
    n_pairs, n, _ = q.shape
    d = n_pairs * LANES
    qb = NA_Q_ROWS * GRID_W
    nblk = seq // qb
    assert seq % qb == 0 and nblk >= NA_K_BLOCKS and n_ctx == qb and (batch * seq) % n_ctx == 0
    assert seq // GRID_W >= NA_WIN_ROWS and NA_K_BLOCKS * NA_Q_ROWS == NA_Q_ROWS + NA_WIN_ROWS
    bias = _na_bias_tables(rpb)
    ctx_blk0 = batch * seq // n_ctx

    pp = NA_PAIRS_PER_STEP
    assert n_pairs % pp == 0

    def kv_spec(j):
        return pl.BlockSpec((pp, qb, LANES),
                            lambda b, c, i: (c, b * nblk + jnp.clip(i - 1, 0, nblk - NA_K_BLOCKS) + j, 0))

    ctx_spec = pl.BlockSpec((pp, n_ctx, LANES), lambda b, c, i: (c, ctx_blk0 + b, 0))
    q_row = lambda b, i: jnp.where(i < nblk, b * nblk + i, ctx_blk0 + b)
    q_spec = pl.BlockSpec((pp, qb, LANES), lambda b, c, i: (c, q_row(b, i), 0))
    variant = lambda i: jnp.where(i == 0, 0, jnp.where(i < nblk - 1, 1, jnp.where(i == nblk - 1, 2, 3)))
    return pl.pallas_call(
        _na_kernel,
        grid=(batch, n_pairs // pp, nblk + 1),
        in_specs=[q_spec] + [kv_spec(j) for j in range(NA_K_BLOCKS)] * 2
        + [ctx_spec, ctx_spec,
           pl.BlockSpec((1, 2 * pp, qb, NA_K_BLOCKS * qb), lambda b, c, i: (variant(i), c, 0, 0))],
        out_specs=pl.BlockSpec((qb, pp * LANES), lambda b, c, i: (q_row(b, i), c)),
        out_shape=jax.ShapeDtypeStruct((n, d), BF16),
        compiler_params=_params(("arbitrary", "arbitrary", "arbitrary")),
        name="na_attention",
    )(q, k, k, k, v, v, v, k, v, bias)


def _diff_scores(qm, k):
    s = _dot_nt(qm, k)
    return s, jnp.max(s, axis=1, keepdims=True)


def _diff_accumulate(s, m_cur, v_ref, m_prev, acc_prev):
    m_new = jnp.broadcast_to(m_cur, (s.shape[0], LANES)) if m_prev is None else jnp.maximum(m_prev, m_cur)
    m_wide = jnp.concatenate([m_new, m_new], axis=1)
    acc = None
    for c in range(0, s.shape[1], 2 * LANES):
        p = jnp.exp2(s[:, c:c + 2 * LANES] - m_wide).astype(BF16)
        pv = _dot(p, v_ref[0, c:c + 2 * LANES, :])
        acc = pv if acc is None else acc + pv
    if m_prev is not None:
        alpha = jnp.exp2(m_prev - m_new)
        acc = jnp.concatenate([alpha, alpha], axis=1) * acc_prev + acc
    return m_new, acc


def _diff_kernel(q_ref, k_ref, v_ref, kc_ref, vc_ref, lq1_ref, lk1_ref, lq2_ref, lk2_ref, sub_ref,
                 o_ref, m_sc, acc_sc, *, lambda_init):
    j = pl.program_id(3)
    tq = q_ref.shape[1]
    units = [(t, r) for r in range(0, tq, DIFF_STRIP) for t in range(2)]
    masks = _head_masks()

    def q_strip(t, r):
        qs = q_ref[0, r:r + DIFF_STRIP, :]
        return jnp.where(masks[t], qs, jnp.zeros_like(qs))

    @pl.when(j == 0)
    def _():
        for t, r in units:
            m, acc = _diff_accumulate(*_diff_scores(q_strip(t, r), kc_ref[0]), vc_ref, None, None)
            m_sc[t, r:r + DIFF_STRIP, :] = m
            acc_sc[t, r:r + DIFF_STRIP, :] = acc

    nxt = _diff_scores(q_strip(*units[0]), k_ref[0])
    for u, (t, r) in enumerate(units):
        cur = nxt
        if u + 1 < len(units):
            nxt = _diff_scores(q_strip(*units[u + 1]), k_ref[0])
        rows = slice(r, r + DIFF_STRIP)
        m, acc = _diff_accumulate(*cur, v_ref, m_sc[t, rows, :], acc_sc[t, rows, :])
        m_sc[t, rows, :] = m
        acc_sc[t, rows, :] = acc

    @pl.when(j == pl.num_programs(3) - 1)
    def _():
        lam = (jnp.exp(jnp.sum(lq1_ref[...] * lk1_ref[...], axis=-1, keepdims=True))
               - jnp.exp(jnp.sum(lq2_ref[...] * lk2_ref[...], axis=-1, keepdims=True)) + lambda_init)
        a1, a2 = acc_sc[0], acc_sc[1]
        o = a1[:, :LANES] / a1[:, LANES:] - lam * (a2[:, :LANES] / a2[:, LANES:])
        o_ref[...] = (_rms(o, sub_ref[...], DIFF_EPS) * (1.0 - lambda_init)).astype(o_ref.dtype)


def _diff_attention(q, k, v, lq1, lk1, lq2, lk2, subln, lambda_init, batch, seq, n_ctx):
    n_heads, vw = v.shape[0], v.shape[2]
    d = n_heads * LANES
    tq, tk = min(DIFF_TQ, seq), min(DIFF_TK, seq)
    assert seq % tq == 0 and seq % tk == 0 and tq % DIFF_STRIP == 0 and (batch * seq) % n_ctx == 0
    assert subln.shape[-1] == LANES and vw == 2 * LANES
    nq, nk = seq // tq, seq // tk
    ctx_blk0 = batch * seq // n_ctx
    q_spec = pl.BlockSpec((1, tq, LANES), lambda b, h, i, j: (h, b * nq + i, 0))
    k_spec = pl.BlockSpec((1, tk, LANES), lambda b, h, i, j: (h, b * nk + j, 0))
    v_spec = pl.BlockSpec((1, tk, vw), lambda b, h, i, j: (h, b * nk + j, 0))
    kc_spec = pl.BlockSpec((1, n_ctx, LANES), lambda b, h, i, j: (h, ctx_blk0 + b, 0))
    vc_spec = pl.BlockSpec((1, n_ctx, vw), lambda b, h, i, j: (h, ctx_blk0 + b, 0))
    vec = lambda a: a.reshape(1, -1).astype(F32)
    vec_spec = lambda a: pl.BlockSpec((1, a.shape[-1]), lambda b, h, i, j: (0, 0))
    return pl.pallas_call(
        functools.partial(_diff_kernel, lambda_init=lambda_init),
        grid=(batch, n_heads, nq, nk),
        in_specs=[q_spec, k_spec, v_spec, kc_spec, vc_spec,
                  vec_spec(lq1), vec_spec(lk1), vec_spec(lq2), vec_spec(lk2), vec_spec(subln)],
        out_specs=pl.BlockSpec((tq, LANES), lambda b, h, i, j: (b * nq + i, h)),
        out_shape=jax.ShapeDtypeStruct((batch * seq, d), BF16),
        scratch_shapes=[pltpu.VMEM((2, tq, LANES), F32), pltpu.VMEM((2, tq, vw), F32)],
        compiler_params=_params(("arbitrary", "arbitrary", "arbitrary", "arbitrary")),
        name="diff_attention",
    )(q, k, v, k, v, vec(lq1), vec(lk1), vec(lq2), vec(lk2), vec(subln))


def _lane_pack(cols, shape, dtype):
    lane = lax.broadcasted_iota(jnp.int32, shape, 1)
    out = jnp.zeros(shape, dtype)
    for kk, col in enumerate(cols):
        out = jnp.where(lane == kk, col.astype(dtype), out)
    return out


def _proj_post_kernel(o_ref, wo_ref, x_ref, npost_ref, g_ref, npre_ref, sh_ref, sc_ref, wr_ref, br_ref,
                      xo_ref, h_ref, idx_ref, gate_ref, rank_ref, cnt_ref, cnt_sc):
    @pl.when(pl.program_id(0) == 0)
    def _():
        cnt_sc[...] = jnp.zeros_like(cnt_sc)

    tm = o_ref.shape[0]
    part = tm // POST_PARTS
    parts = [slice(p * part, (p + 1) * part) for p in range(POST_PARTS)]
    proj = [_dot(o_ref[rows, :], wo_ref[...]) for rows in parts]
    hs = []
    for rows, a in zip(parts, proj):
        xn = x_ref[rows, :] + g_ref[0] * _rms(a, npost_ref[...], NORM_EPS)
        xo_ref[rows, :] = xn
        h = _rms(xn, npre_ref[...], NORM_EPS) * (1.0 + sc_ref[0]) + sh_ref[0]
        h_ref[rows, :] = h
        hs.append(h)
    all_logits = [jnp.dot(h, wr_ref[...], precision=HIGHEST, preferred_element_type=F32) + br_ref[...] for h in hs]

    lane = lax.broadcasted_iota(jnp.int32, (part, LANES), 1).astype(F32)
    row = lax.broadcasted_iota(jnp.int32, (part, part), 0)
    colm = lax.broadcasted_iota(jnp.int32, (part, part), 1)
    earlier = (colm < row).astype(BF16)
    routed = []
    for logits in all_logits:
        vals, idxs, hits = [], [], []
        work = logits
        for _ in range(TOP_K):
            m = jnp.max(work, axis=-1, keepdims=True)
            idx = jnp.min(jnp.where(work == m, lane, float(LANES)), axis=-1, keepdims=True)
            hit = lane == idx
            work = jnp.where(hit, -jnp.inf, work)
            vals.append(m)
            idxs.append(idx)
            hits.append(hit)
        ex = [jnp.exp(vv - vals[0]) for vv in vals]
        den = functools.reduce(jnp.add, ex)
        member = functools.reduce(jnp.logical_or, hits)
        within = _dot(earlier, member.astype(BF16))
        routed.append((idxs, [e / den for e in ex], hits, member, within))

    counts = cnt_sc[...]
    for rows, (idxs, gates, hits, member, within) in zip(parts, routed):
        before = within + counts
        ranks = [jnp.sum(jnp.where(hit, before, 0.0), axis=-1, keepdims=True) for hit in hits]
        counts = counts + jnp.sum(member.astype(F32), axis=0, keepdims=True)
        idx_ref[rows, :] = _lane_pack(idxs, (part, LANES), jnp.int32)
        gate_ref[rows, :] = _lane_pack(gates, (part, LANES), F32)
        rank_ref[rows, :] = _lane_pack(ranks, (part, LANES), jnp.int32)
    cnt_sc[...] = counts
    cnt_ref[...] = jnp.broadcast_to(counts, cnt_ref.shape)


def _proj_post(o, w_o, xs, norm_post, norm_pre, mod3, w_router, b_router, n_rows, tiles_per_batch, batch):
    d = xs.shape[1]
    n_exp = w_router.shape[1]
    wr = jnp.zeros((d, LANES), F32).at[:, :n_exp].set(w_router)
    br = jnp.full((1, LANES), NEG_BIG, F32).at[0, :n_exp].set(b_router)
    row = lambda t: (t, 0)
    const = lambda t: (0, 0)
    tm = TM_POST
    assert n_rows % tm == 0 and (tiles_per_batch * TM) % tm == 0
    mspec = lambda chunk: _mod_spec(d, chunk, tiles_per_batch * TM // tm, batch)
    wide = lambda dt: jax.ShapeDtypeStruct((n_rows, LANES), dt)
    return pl.pallas_call(
        _proj_post_kernel,
        grid=(n_rows // tm,),
        in_specs=[
            pl.BlockSpec((tm, d), row), pl.BlockSpec((d, d), const), pl.BlockSpec((tm, d), row),
            pl.BlockSpec((1, d), const), mspec(2), pl.BlockSpec((1, d), const), mspec(3), mspec(4),
            pl.BlockSpec((d, LANES), const), pl.BlockSpec((1, LANES), const),
        ],
        out_specs=[pl.BlockSpec((tm, d), row), pl.BlockSpec((tm, d), row),
                   pl.BlockSpec((tm, LANES), row), pl.BlockSpec((tm, LANES), row), pl.BlockSpec((tm, LANES), row),
                   pl.BlockSpec((SUBLANES, LANES), const)],
        out_shape=[jax.ShapeDtypeStruct((n_rows, d), F32), jax.ShapeDtypeStruct((n_rows, d), F32),
                   wide(jnp.int32), wide(F32), wide(jnp.int32), jax.ShapeDtypeStruct((SUBLANES, LANES), F32)],
        scratch_shapes=[pltpu.VMEM((1, LANES), F32)],
        compiler_params=_params(("arbitrary",)),
        name="proj_post_router",
    )(o, w_o, xs, norm_post.reshape(1, d), mod3, norm_pre.reshape(1, d), mod3, mod3, wr, br)


def _for_each_assignment(n_tokens, fn):
    def body(g, carry):
        for u in range(SUBLANES):
            for kk in range(TOP_K):
                fn(g, u, kk, g * (SUBLANES * TOP_K) + (u * TOP_K + kk))
        return carry
    lax.fori_loop(0, n_tokens // SUBLANES, body, 0)


def _dispatch_kernel(ends_ref, nu_ref, dest_ref, h_ref, buf_ref, zero_sc, sem, zero_sem):
    tm = h_ref.shape[0] * h_ref.shape[1]
    blk = zero_sc.shape[0]

    @pl.when(pl.program_id(0) == 0)
    def _():
        zero_sc[...] = jnp.zeros_like(zero_sc)
        zero_copy = lambda row: pltpu.make_async_copy(zero_sc, buf_ref.at[pl.ds(row, blk), :], zero_sem)
        n_exp = ends_ref.shape[0]
        non_empty = [ends_ref[e] > (ends_ref[e - 1] if e else 0) for e in range(n_exp)]
        for action in ("start", "wait"):
            for e in range(n_exp):
                @pl.when(non_empty[e])
                def _():
                    getattr(zero_copy(pl.multiple_of(ends_ref[e] - blk, blk)), action)()

            def tail(b, carry):
                getattr(zero_copy(pl.multiple_of(b * blk, blk)), action)()
                return carry
            lax.fori_loop(nu_ref[0], buf_ref.shape[0] // blk, tail, 0)

    def row_copy(g, u, kk, flat):
        row = dest_ref[flat]
        return pltpu.make_async_copy(h_ref.at[g, pl.ds(u, 1), :], buf_ref.at[pl.ds(row, 1), :], sem)

    _for_each_assignment(tm, lambda g, u, kk, flat: row_copy(g, u, kk, flat).start(priority=kk % 2))
    for _ in range(TOP_K):
        pltpu.make_async_copy(buf_ref.at[pl.ds(0, tm), :], buf_ref.at[pl.ds(0, tm), :], sem).wait()


def _dispatch(h, dest_flat, pad_ends, n_used, n_buf_rows):
    n, d = h.shape
    grid_spec = pltpu.PrefetchScalarGridSpec(
        num_scalar_prefetch=2,
        grid=(n // TM,),
        in_specs=[pl.BlockSpec((TM * TOP_K,), lambda t, ends, nu: (t,), memory_space=pltpu.SMEM),
                  pl.BlockSpec((TM // SUBLANES, SUBLANES, d), lambda t, ends, nu: (t, 0, 0))],
        out_specs=pl.BlockSpec(memory_space=pl.ANY),
        scratch_shapes=[pltpu.VMEM((EXPERT_ROWS, d), h.dtype), pltpu.SemaphoreType.DMA, pltpu.SemaphoreType.DMA],
    )
    return pl.pallas_call(
        _dispatch_kernel,
        grid_spec=grid_spec,
        out_shape=jax.ShapeDtypeStruct((n_buf_rows, d), h.dtype),
        compiler_params=_params(("arbitrary",), has_side_effects=True, disable_bounds_checks=True),
        name="moe_dispatch",
    )(pad_ends, n_used, dest_flat, h.reshape(n // SUBLANES, SUBLANES, d))


def _expert_kernel(be_ref, nu_ref, x_ref, wgu_ref, bgu_ref, wd_ref, bd_ref, o_ref, wgu_bf, wd_bf):
    b = pl.program_id(0)
    changed = jnp.logical_or(b == 0, be_ref[b] != be_ref[jnp.maximum(b - 1, 0)])

    @pl.when(changed)
    def _():
        wgu_bf[...] = wgu_ref[0].astype(BF16)
        wd_bf[...] = wd_ref[0].astype(BF16)

    @pl.when(b < nu_ref[0])
    def _():
        f = wd_bf.shape[0]
        gu = _dot(x_ref[...].astype(BF16), wgu_bf[...]) + bgu_ref[0]
        g = jnp.minimum(gu[:, :f], SWIGLU_LIMIT)
        u = jnp.clip(gu[:, f:], -SWIGLU_LIMIT, SWIGLU_LIMIT)
        act = (u + 1.0) * (g * jax.nn.sigmoid(SWIGLU_ALPHA * g))
        o_ref[...] = _dot(act.astype(BF16), wd_bf[...]) + bd_ref[0]

    @pl.when(b >= nu_ref[0])
    def _():
        o_ref[...] = jnp.zeros_like(o_ref)


def _experts(buf, block_expert, n_used, layer, w_gate_up, b_gate_up, w_down, b_down):
    n_rows, d = buf.shape
    depth, n_exp, _, f2 = w_gate_up.shape
    f = w_down.shape[2]
    n_blocks = n_rows // EXPERT_ROWS
    block_expert = block_expert + layer * n_exp
    w_gate_up, w_down = w_gate_up.reshape(depth * n_exp, d, f2), w_down.reshape(depth * n_exp, f, d)
    n_exp = depth * n_exp
    grid_spec = pltpu.PrefetchScalarGridSpec(
        num_scalar_prefetch=2,
        grid=(n_blocks,),
        in_specs=[
            pl.BlockSpec((EXPERT_ROWS, d), lambda b, be, nu: (b, 0)),
            pl.BlockSpec((1, d, f2), lambda b, be, nu: (be[b], 0, 0)),
            pl.BlockSpec((1, 1, f2), lambda b, be, nu: (be[b], 0, 0)),
            pl.BlockSpec((1, f, d), lambda b, be, nu: (be[b], 0, 0)),
            pl.BlockSpec((1, 1, d), lambda b, be, nu: (be[b], 0, 0)),
        ],
        out_specs=pl.BlockSpec((EXPERT_ROWS, d), lambda b, be, nu: (b, 0)),
        scratch_shapes=[pltpu.VMEM((d, f2), BF16), pltpu.VMEM((f, d), BF16)],
    )
    return pl.pallas_call(
        _expert_kernel,
        grid_spec=grid_spec,
        out_shape=jax.ShapeDtypeStruct((n_rows, d), F32),
        compiler_params=_params(("arbitrary",)),
        name="moe_experts",
    )(block_expert, n_used, buf, w_gate_up, b_gate_up.reshape(n_exp, 1, f2), w_down, b_down.reshape(n_exp, 1, d))


def _combine_kernel(dest_ref, gate_ref, x_ref, nw_ref, g_ref, y_ref, xo_ref, rows_sc, sem):
    def row_copy(g, u, kk, flat):
        row = dest_ref[flat]
        return pltpu.make_async_copy(y_ref.at[pl.ds(row, 1), :], rows_sc.at[kk, g, pl.ds(u, 1), :], sem)

    tm, d = x_ref.shape
    _for_each_assignment(tm, lambda g, u, kk, flat: row_copy(g, u, kk, flat).start(priority=kk % 2))
    for _ in range(TOP_K):
        pltpu.make_async_copy(y_ref.at[pl.ds(0, tm), :], y_ref.at[pl.ds(0, tm), :], sem).wait()
    gates = gate_ref[...]
    y = gates[:, 0:1] * rows_sc[0].reshape(tm, d)
    for kk in range(1, TOP_K):
        y = y + gates[:, kk:kk + 1] * rows_sc[kk].reshape(tm, d)
    xo_ref[...] = x_ref[...] + g_ref[0] * _rms(y, nw_ref[...], NORM_EPS)


def _combine(y_grouped, dest_flat, gates, xs, norm_w, mod3, n_rows, tiles_per_batch, batch):
    d = xs.shape[1]
    row = lambda t: (t, 0)
    return pl.pallas_call(
        _combine_kernel,
        grid=(n_rows // TM,),
        in_specs=[pl.BlockSpec((TM * TOP_K,), lambda t: (t,), memory_space=pltpu.SMEM),
                  pl.BlockSpec((TM, LANES), row), pl.BlockSpec((TM, d), row),
                  pl.BlockSpec((1, d), lambda t: (0, 0)), _mod_spec(d, 5, tiles_per_batch, batch),
                  pl.BlockSpec(memory_space=pl.ANY)],
        out_specs=pl.BlockSpec((TM, d), row),
        out_shape=jax.ShapeDtypeStruct((n_rows, d), F32),
        scratch_shapes=[pltpu.VMEM((TOP_K, TM // SUBLANES, SUBLANES, d), F32), pltpu.SemaphoreType.DMA],
        compiler_params=_params(("arbitrary",), disable_bounds_checks=True),
        name="moe_combine",
    )(dest_flat, gates, xs, norm_w.reshape(1, d), mod3, y_grouped)


def _moe_layer(h, idx, gates, rank, cnt, xs, norm_post, mod3, layer, w_gate_up, b_gate_up, w_down, b_down,
               n_rows, tiles_per_batch, batch):
    n_exp = w_gate_up.shape[1]
    blk = EXPERT_ROWS
    counts = cnt[0, :n_exp].astype(jnp.int32)
    padded = (counts + blk - 1) // blk * blk
    pad_ends = jnp.cumsum(padded)
    pad_starts = pad_ends - padded
    dest = (pad_starts[idx[:, :TOP_K]] + rank[:, :TOP_K]).reshape(-1)
    n_blocks = (n_rows * TOP_K + n_exp * (blk - 1) + blk - 1) // blk
    block_start = jnp.arange(n_blocks, dtype=jnp.int32) * blk
    block_expert = jnp.sum(block_start[:, None] >= pad_ends[None, :], axis=1)
    block_expert = jnp.minimum(block_expert, n_exp - 1).astype(jnp.int32)
    n_used = (pad_ends[-1:] // blk).astype(jnp.int32)
    buf = _dispatch(h, dest, pad_ends.astype(jnp.int32), n_used, n_blocks * blk)
    y_grouped = _experts(buf, block_expert, n_used, layer, w_gate_up, b_gate_up, w_down, b_down)
    return _combine(y_grouped, dest, gates, xs, norm_post, mod3, n_rows, tiles_per_batch, batch)


def kernel(x, c, ctx, c_ctx, w_mod, b_mod, norm_mix_pre, norm_mix_post, norm_ffn_pre, norm_ffn_post,
           na_w_qkv, na_w_o, na_rpb,
           diff_w_qkv, diff_w_o, diff_lambda_q1, diff_lambda_k1, diff_lambda_q2, diff_lambda_k2, diff_subln,
           moe_w_router, moe_b_router, moe_w_gate_up, moe_b_gate_up, moe_w_down, moe_b_down):
    batch, seq, d = x.shape
    n_ctx = ctx.shape[1]
    depth = w_mod.shape[0]
    assert depth == 2 and seq % TM == 0 and (batch * n_ctx) % TM == 0
    n_lat = batch * seq
    n_all = n_lat + batch * n_ctx
    tiles_per_batch = seq // TM
    xs = jnp.concatenate([x.reshape(n_lat, d), ctx.reshape(batch * n_ctx, d)], axis=0)
    mod = _adaln_mod(c, c_ctx, w_mod, b_mod).reshape(depth, SUBLANES, 1, 6 * d)

    na_dh = d // NA_HEADS
    q, k, v = _norm_qkv(xs, norm_mix_pre[0], mod[0], na_w_qkv[0].astype(BF16), tiles_per_batch, batch,
                        q_scale=na_dh ** -0.5 * LOG2_E)
    o = _na_attention(q, k, v, na_rpb[0], batch, seq, n_ctx)
    xs, h, idx, gates, rank, cnt = _proj_post(o, na_w_o[0].astype(BF16), xs, norm_mix_post[0], norm_ffn_pre[0],
                                              mod[0], moe_w_router[0], moe_b_router[0], n_all, tiles_per_batch, batch)
    xs = _moe_layer(h, idx, gates, rank, cnt, xs, norm_ffn_post[0], mod[0], 0, moe_w_gate_up, moe_b_gate_up,
                    moe_w_down, moe_b_down, n_all, tiles_per_batch, batch)

    diff_dh = d // DIFF_HEADS // 2
    lambda_init = 0.8 - 0.6 * math.exp(-0.3 * 1)
    cols = _pair_split_columns(2 * DIFF_HEADS, diff_dh)
    w1 = diff_w_qkv[0]
    w1 = jnp.concatenate([w1[:, :d][:, cols], w1[:, d:2 * d][:, cols], w1[:, 2 * d:]], axis=1).astype(BF16)
    tables = _rope_tables(seq, batch, batch * n_ctx, diff_dh)
    q, k, v = _norm_qkv(xs, norm_mix_pre[1], mod[1], w1, tiles_per_batch, batch,
                        q_scale=diff_dh ** -0.5 * LOG2_E, rope_tables=tables)
    o = _diff_attention(q, k, v, diff_lambda_q1[0], diff_lambda_k1[0], diff_lambda_q2[0], diff_lambda_k2[0],
                        diff_subln[0], lambda_init, batch, seq, n_ctx)
    xs, h, idx, gates, rank, cnt = _proj_post(o, diff_w_o[0].astype(BF16), xs, norm_mix_post[1], norm_ffn_pre[1],
                                              mod[1], moe_w_router[1], moe_b_router[1], n_lat, tiles_per_batch, batch)
    xs = _moe_layer(h, idx, gates, rank, cnt, xs, norm_ffn_post[1], mod[1], 1, moe_w_gate_up, moe_b_gate_up,
                    moe_w_down, moe_b_down, n_lat, tiles_per_batch, batch)
    return xs.reshape(batch, seq, d)
```
